```python
import math
import jax
import jax.numpy as jnp
from jax import lax
import numpy as np

D_MODEL = 2048
BATCH = 2
SEQ = 4096
DEPTH = 1
DEC_BATCH = 128
DEC_SEQ = 4
PAST_LEN = 2048
PAGE_SIZE = 128

HEAD_DIM = 128
N_HEADS_DSA = 8
N_HEADS_SB = 8
D_DSA = N_HEADS_DSA * HEAD_DIM
D_SB = N_HEADS_SB * HEAD_DIM
N_IDX_HEADS = 16
IDX_DIM = 64
IDX_W_SCALE = (N_IDX_HEADS * IDX_DIM) ** -0.5
TOP_K_MAX = 256
D_FF = 4 * D_MODEL
N_REL_BUCKETS = 32
REL_MAX_DISTANCE = 128
Q_BLOCK = 128
RMS_EPS = 1e-6
NEG_INF = -1e30
IN_SPLITS = (D_DSA, D_DSA, D_DSA, N_IDX_HEADS * IDX_DIM, IDX_DIM, N_IDX_HEADS,
             D_SB, D_SB, D_SB, D_MODEL, D_MODEL)
IN_COLS = sum(IN_SPLITS)

kernel_name = 'hybrid_dsa_stickbreaking_decode_step'


def rms_norm(x, gain):
    xf = x.astype(jnp.float32)
    y = xf * lax.rsqrt(jnp.mean(xf * xf, axis=-1, keepdims=True) + RMS_EPS)
    return (y * gain.astype(jnp.float32)).astype(x.dtype)


def split_heads(t, n_heads):
    return t.reshape(t.shape[:-1] + (n_heads, t.shape[-1] // n_heads))


def combined_projection(h, w_in):
    points = [int(c) for c in np.cumsum(IN_SPLITS)[:-1]]
    q, k, v, qi, ki, wi, qs, ks, vs, ga, gb = jnp.split(h @ w_in, points, axis=-1)
    return (split_heads(q, N_HEADS_DSA), split_heads(k, N_HEADS_DSA), split_heads(v, N_HEADS_DSA),
            split_heads(qi, N_IDX_HEADS), ki, wi * IDX_W_SCALE,
            split_heads(qs, N_HEADS_SB), split_heads(ks, N_HEADS_SB), split_heads(vs, N_HEADS_SB),
            ga, gb)


def rel_bucket(dist):
    max_exact = N_REL_BUCKETS // 2
    d = jnp.maximum(dist, 0)
    df = jnp.maximum(d, 1).astype(jnp.float32)
    log_part = jnp.log(df / max_exact) / math.log(REL_MAX_DISTANCE / max_exact)
    large = max_exact + (log_part * (N_REL_BUCKETS - max_exact)).astype(jnp.int32)
    return jnp.where(d < max_exact, d, jnp.minimum(large, N_REL_BUCKETS - 1))


def rel_bias_logits(dist, rel_bias):
    return jnp.swapaxes(rel_bias[rel_bucket(dist)].astype(jnp.float32), -1, -2)


def indexer_scores(qi, wi, ki):
    s = jax.nn.relu(jnp.einsum('bqhd,bld->bqhl', qi, ki).astype(jnp.float32))
    return jnp.einsum('bqh,bqhl->bql', wi.astype(jnp.float32), s)


def stick_breaking_weights(z, q_pos, key_pos):
    before = key_pos[None, :] < q_pos[:, None]
    log_keep = jnp.where(before, -jax.nn.softplus(z), 0.0)
    later = lax.cumsum(log_keep, axis=z.ndim - 1, reverse=True) - log_keep
    return jnp.where(before, jnp.exp(jax.nn.log_sigmoid(z) + later), 0.0)


def gather_past(pool, page_table):
    rows = pool[page_table]
    return rows.reshape((page_table.shape[0], PAST_LEN) + pool.shape[2:])


def dsa_prompt(q, k, v, qi, ki, wi, rel_bias):
    n_seq, n_tok = q.shape[0], q.shape[1]
    top_k = min(TOP_K_MAX, n_tok // 4)
    key_pos = jnp.arange(n_tok)
    scale = HEAD_DIM ** -0.5
    take_rows = jax.vmap(lambda rows, ids: rows[ids])

    def block(i):
        start = i * Q_BLOCK
        qb = lax.dynamic_slice_in_dim(q, start, Q_BLOCK, axis=1)
        qib = lax.dynamic_slice_in_dim(qi, start, Q_BLOCK, axis=1)
        wib = lax.dynamic_slice_in_dim(wi, start, Q_BLOCK, axis=1)
        q_pos = start + jnp.arange(Q_BLOCK)
        scores = indexer_scores(qib, wib, ki)
        scores = jnp.where(key_pos[None, None, :] <= q_pos[None, :, None], scores, NEG_INF)
        _, idx = lax.top_k(scores, top_k)
        k_sel = take_rows(k, idx)
        v_sel = take_rows(v, idx)
        logits = jnp.einsum('bqhd,bqkhd->bqhk', qb, k_sel).astype(jnp.float32) * scale
        logits = logits + rel_bias_logits(q_pos[None, :, None] - idx, rel_bias)
        logits = jnp.where((idx <= q_pos[None, :, None])[:, :, None, :], logits, NEG_INF)
        p = jax.nn.softmax(logits, axis=-1).astype(v.dtype)
        return jnp.einsum('bqhk,bqkhd->bqhd', p, v_sel)

    out = lax.map(block, jnp.arange(n_tok // Q_BLOCK))
    return jnp.moveaxis(out, 0, 1).reshape(n_seq, n_tok, D_DSA)


def sb_prompt(q, k, v):
    n_seq, n_tok = q.shape[0], q.shape[1]
    key_pos = jnp.arange(n_tok)
    scale = HEAD_DIM ** -0.5

    def block(i):
        start = i * Q_BLOCK
        qb = lax.dynamic_slice_in_dim(q, start, Q_BLOCK, axis=1)
        z = jnp.einsum('bqhd,bshd->bhqs', qb, k).astype(jnp.float32) * scale
        a = stick_breaking_weights(z, start + jnp.arange(Q_BLOCK), key_pos).astype(v.dtype)
        return jnp.einsum('bhqs,bshd->bqhd', a, v)

    out = lax.map(block, jnp.arange(n_tok // Q_BLOCK))
    return jnp.moveaxis(out, 0, 1).reshape(n_seq, n_tok, D_SB)


def dsa_sample(q, k_new, v_new, qi, ki_new, wi, pool_k, pool_v, pool_ki, page_table, rel_bias):
    n_seq, n_new = q.shape[0], q.shape[1]
    n_keys = PAST_LEN + n_new
    top_k = min(TOP_K_MAX, n_keys // 4)
    scale = HEAD_DIM ** -0.5
    q_pos = PAST_LEN + jnp.arange(n_new)
    key_pos = jnp.arange(n_keys)
    ki_all = jnp.concatenate([gather_past(pool_ki, page_table), ki_new], axis=1)
    scores = indexer_scores(qi, wi, ki_all)
    scores = jnp.where(key_pos[None, None, :] <= q_pos[None, :, None], scores, NEG_INF)
    _, idx = lax.top_k(scores, top_k)
    is_past = idx < PAST_LEN
    idx_p = jnp.minimum(idx, PAST_LEN - 1)
    phys = jax.vmap(lambda pt, ip: pt[ip])(page_table, idx_p // PAGE_SIZE)
    slot = phys * PAGE_SIZE + idx_p % PAGE_SIZE
    k_sel = pool_k.reshape((-1,) + pool_k.shape[2:])[slot]
    v_sel = pool_v.reshape((-1,) + pool_v.shape[2:])[slot]
    lp = jnp.einsum('bqhd,bqkhd->bqhk', q, k_sel).astype(jnp.float32) * scale
    lp = lp + rel_bias_logits(q_pos[None, :, None] - idx, rel_bias)
    lp = jnp.where(is_past[:, :, None, :], lp, NEG_INF)
    sel_new = jnp.any(idx[..., None] == q_pos[None, None, None, :], axis=2)
    sel_new = sel_new & (q_pos[None, :] <= q_pos[:, None])[None]
    ln = jnp.einsum('bqhd,bjhd->bqhj', q, k_new).astype(jnp.float32) * scale
    ln = ln + rel_bias_logits((q_pos[:, None] - q_pos[None, :])[None], rel_bias)
    ln = jnp.where(sel_new[:, :, None, :], ln, NEG_INF)
    p = jax.nn.softmax(jnp.concatenate([lp, ln], axis=-1), axis=-1).astype(v_new.dtype)
    out = (jnp.einsum('bqhk,bqkhd->bqhd', p[..., :top_k], v_sel)
           + jnp.einsum('bqhj,bjhd->bqhd', p[..., top_k:], v_new))
    return out.reshape(n_seq, n_new, D_DSA)


def sb_sample(q, k_new, v_new, pool_k, pool_v, page_table):
    n_seq, n_new = q.shape[0], q.shape[1]
    scale = HEAD_DIM ** -0.5
    k_past = gather_past(pool_k, page_table)
    v_past = gather_past(pool_v, page_table)
    z = jnp.concatenate([jnp.einsum('bqhd,bshd->bhqs', q, k_past),
                         jnp.einsum('bqhd,bjhd->bhqj', q, k_new)], axis=-1).astype(jnp.float32) * scale
    q_pos = PAST_LEN + jnp.arange(n_new)
    key_pos = jnp.arange(PAST_LEN + n_new)
    a = stick_breaking_weights(z, q_pos, key_pos).astype(v_new.dtype)
    out = (jnp.einsum('bhqs,bshd->bqhd', a[..., :PAST_LEN], v_past)
           + jnp.einsum('bhqj,bjhd->bqhd', a[..., PAST_LEN:], v_new))
    return out.reshape(n_seq, n_new, D_SB)


def merge_branches(o_a, o_b, ga, gb, w_out_dsa, w_out_sb, w_o):
    y = jax.nn.sigmoid(ga) * (o_a @ w_out_dsa) + jax.nn.sigmoid(gb) * (o_b @ w_out_sb)
    return y @ w_o


def sq_relu_mlp(x, w_up, w_down, g_pre, g_post):
    u = jax.nn.relu(rms_norm(x, g_pre) @ w_up)
    return x + rms_norm((u * u) @ w_down, g_post)


def setup_inputs(seed: int = 0) -> dict:
    key = jax.random.key(seed)
    keys = jax.random.split(key, 20)
    n_pages = PAST_LEN // PAGE_SIZE
    n_used = DEC_BATCH * n_pages
    n_pool = (n_used * 5 + 3) // 4

    def normal(k, shape, scale=1.0):
        return jax.random.normal(k, shape, jnp.float32) * scale

    dsa_shape = (DEPTH, n_pool, PAGE_SIZE, N_HEADS_DSA, HEAD_DIM)
    sb_shape = (DEPTH, n_pool, PAGE_SIZE, N_HEADS_SB, HEAD_DIM)
    page_table = jax.random.permutation(keys[7], n_pool)[:n_used].reshape(DEC_BATCH, n_pages).astype(jnp.int32)
    return {
        'x_prompt': normal(keys[0], (BATCH, SEQ, D_MODEL)),
        'x_sample': normal(keys[1], (DEC_BATCH, DEC_SEQ, D_MODEL)),
        'cache_k_dsa': normal(keys[2], dsa_shape),
        'cache_v_dsa': normal(keys[3], dsa_shape),
        'cache_k_idx': normal(keys[4], (DEPTH, n_pool, PAGE_SIZE, IDX_DIM)),
        'cache_k_sb': normal(keys[5], sb_shape),
        'cache_v_sb': normal(keys[6], sb_shape),
        'page_table': page_table,
        'rel_bias': normal(keys[8], (N_REL_BUCKETS, N_HEADS_DSA), 0.5),
        'w_in': normal(keys[9], (DEPTH, D_MODEL, IN_COLS), D_MODEL ** -0.5),
        'w_out_dsa': normal(keys[10], (DEPTH, D_DSA, D_MODEL), D_DSA ** -0.5),
        'w_out_sb': normal(keys[11], (DEPTH, D_SB, D_MODEL), D_SB ** -0.5),
        'w_o': normal(keys[12], (DEPTH, D_MODEL, D_MODEL), D_MODEL ** -0.5),
        'w_up': normal(keys[13], (DEPTH, D_MODEL, D_FF), D_MODEL ** -0.5),
        'w_down': normal(keys[14], (DEPTH, D_FF, D_MODEL), D_FF ** -0.5),
        'g_attn_pre': 1.0 + normal(keys[15], (DEPTH, D_MODEL), 0.05),
        'g_attn_post': 1.0 + normal(keys[16], (DEPTH, D_MODEL), 0.05),
        'g_mlp_pre': 1.0 + normal(keys[17], (DEPTH, D_MODEL), 0.05),
        'g_mlp_post': 1.0 + normal(keys[18], (DEPTH, D_MODEL), 0.05),
    }


def reference(x_prompt, x_sample, cache_k_dsa, cache_v_dsa, cache_k_idx, cache_k_sb, cache_v_sb,
              page_table, rel_bias, w_in, w_out_dsa, w_out_sb, w_o, w_up, w_down,
              g_attn_pre, g_attn_post, g_mlp_pre, g_mlp_post):
    xp, xs = x_prompt, x_sample
    rows_p, rows_s = [], []
    for l in range(DEPTH):
        q, k, v, qi, ki, wi, qs, ks, vs, ga, gb = combined_projection(rms_norm(xp, g_attn_pre[l]), w_in[l])
        o_a = dsa_prompt(q, k, v, qi, ki, wi, rel_bias)
        o_b = sb_prompt(qs, ks, vs)
        xp = xp + rms_norm(merge_branches(o_a, o_b, ga, gb, w_out_dsa[l], w_out_sb[l], w_o[l]), g_attn_post[l])
        xp = sq_relu_mlp(xp, w_up[l], w_down[l], g_mlp_pre[l], g_mlp_post[l])
        rows_p.append((k, v, ki, ks, vs))
        q, k, v, qi, ki, wi, qs, ks, vs, ga, gb = combined_projection(rms_norm(xs, g_attn_pre[l]), w_in[l])
        o_a = dsa_sample(q, k, v, qi, ki, wi, cache_k_dsa[l], cache_v_dsa[l], cache_k_idx[l], page_table, rel_bias)
        o_b = sb_sample(qs, ks, vs, cache_k_sb[l], cache_v_sb[l], page_table)
        xs = xs + rms_norm(merge_branches(o_a, o_b, ga, gb, w_out_dsa[l], w_out_sb[l], w_o[l]), g_attn_post[l])
        xs = sq_relu_mlp(xs, w_up[l], w_down[l], g_mlp_pre[l], g_mlp_post[l])
        rows_s.append((k, v, ki, ks, vs))
    k_dsa_p, v_dsa_p, k_idx_p, k_sb_p, v_sb_p = [jnp.stack(r, axis=0) for r in zip(*rows_p)]
    k_dsa_s, v_dsa_s, k_idx_s, k_sb_s, v_sb_s = [jnp.stack(r, axis=0) for r in zip(*rows_s)]
    return (xp, xs, k_dsa_p, v_dsa_p, k_idx_p, k_sb_p, v_sb_p, k_dsa_s, v_dsa_s, k_idx_s, k_sb_s, v_sb_s)
```

```python
import functools
import math

import numpy as np
import jax
import jax.numpy as jnp
from jax import lax
from jax.experimental import pallas as pl
from jax.experimental.pallas import tpu as pltpu

F32 = jnp.float32
BF16 = jnp.bfloat16
I32 = jnp.int32

HEAD_DIM = 128
N_HEADS = 8
N_IDX_HEADS = 16
IDX_DIM = 64
IDX_W_SCALE = (N_IDX_HEADS * IDX_DIM) ** -0.5
TOP_K_MAX = 256
N_REL_BUCKETS = 32
REL_MAX_DISTANCE = 128
RMS_EPS = 1e-6
NEG_INF = -1e30
ATTN_SCALE = HEAD_DIM ** -0.5

LANES = 128
SUBLANES = 8
D_ATT = N_HEADS * HEAD_DIM

C_QD, C_KD, C_VD, C_QI, C_QS, C_KS, C_VS = (k * D_ATT for k in range(7))
C_GA = 7 * D_ATT
PROJ_TN = 512


def _key_of(x):
    b = int(np.float32(x).view(np.int32))
    return b if b >= 0 else b ^ 0x7FFFFFFF


KEY_NEG_INF = _key_of(NEG_INF)
INT_MIN = -(2 ** 31)


def _cparams(n_axes, vmem_mb=48):
    return pltpu.CompilerParams(dimension_semantics=("arbitrary",) * n_axes,
                                vmem_limit_bytes=vmem_mb * 1024 * 1024)


def _dot(a, b):
    return jnp.dot(a, b, preferred_element_type=F32)


def _dot_nt(a, b):
    return lax.dot_general(a, b, (((1,), (1,)), ((), ())), preferred_element_type=F32)


def _split_dot(x, m01, passes):
    out = None
    rem = x
    for p in range(passes):
        piece = rem.astype(BF16)
        d = _dot(piece, m01)
        out = d if out is None else out + d
        if p + 1 < passes:
            rem = rem - piece.astype(F32)
    return out


def _softplus(z):
    return jnp.maximum(z, 0.0) + jnp.log1p(jnp.exp(-jnp.abs(z)))


def _float_key(s):
    s = jnp.where(s == 0.0, 0.0, s)
    bits = lax.bitcast_convert_type(s, I32)
    return jnp.where(bits >= 0, bits, bits ^ 0x7FFFFFFF)


def _rel_bucket_np(dist):
    max_exact = N_REL_BUCKETS // 2
    d = np.maximum(dist, 0)
    df = np.maximum(d, 1).astype(np.float32)
    log_part = np.log(df / np.float32(max_exact)) / np.float32(math.log(REL_MAX_DISTANCE / max_exact))
    large = max_exact + (log_part * np.float32(N_REL_BUCKETS - max_exact)).astype(np.int32)
    return np.where(d < max_exact, d, np.minimum(large, N_REL_BUCKETS - 1)).astype(np.int32)


def _rms_cast_kernel(x_ref, g_ref, o_ref):
    x = x_ref[...]
    ms = jnp.mean(x * x, axis=-1, keepdims=True)
    o_ref[...] = (x * lax.rsqrt(ms + RMS_EPS) * g_ref[...]).astype(o_ref.dtype)


def rms_cast(x, g, tm=256):
    m, d = x.shape
    tm = min(tm, m)
    return pl.pallas_call(
        _rms_cast_kernel,
        grid=(m // tm,),
        in_specs=[pl.BlockSpec((tm, d), lambda i: (i, 0)), pl.BlockSpec((1, d), lambda i: (0, 0))],
        out_specs=pl.BlockSpec((tm, d), lambda i: (i, 0)),
        out_shape=jax.ShapeDtypeStruct((m, d), BF16),
        compiler_params=_cparams(1),
        name="rms_cast",
    )(x, g.reshape(1, d))


def _mm_kernel(a_ref, w_ref, o_ref):
    o_ref[...] = _dot(a_ref[...], w_ref[...]).astype(o_ref.dtype)


def matmul(a, w, tm, tn, out_dtype=F32):
    m, k = a.shape
    n = w.shape[1]
    return pl.pallas_call(
        _mm_kernel,
        grid=(m // tm, n // tn),
        in_specs=[pl.BlockSpec((tm, k), lambda i, j: (i, 0)), pl.BlockSpec((k, tn), lambda i, j: (0, j))],
        out_specs=pl.BlockSpec((tm, tn), lambda i, j: (i, j)),
        out_shape=jax.ShapeDtypeStruct((m, n), out_dtype),
        compiler_params=_cparams(2),
        name="in_proj",
    )(a, w)


def _merge_kernel(oa_ref, ob_ref, w1_ref, w2_ref, ga_ref, gb_ref, y_ref):
    ya = _dot(oa_ref[...].astype(BF16), w1_ref[...])
    yb = _dot(ob_ref[...].astype(BF16), w2_ref[...])
    sa = 1.0 / (1.0 + jnp.exp(-ga_ref[...]))
    sb = 1.0 / (1.0 + jnp.exp(-gb_ref[...]))
    y_ref[...] = (sa * ya + sb * yb).astype(y_ref.dtype)


def merge_branches(o_a, o_b, w1, w2, proj, tm, tn=512):
    m = o_a.shape[0]
    d = w1.shape[1]
    ga0 = C_GA // tn
    gb0 = (C_GA + d) // tn
    return pl.pallas_call(
        _merge_kernel,
        grid=(m // tm, d // tn),
        in_specs=[pl.BlockSpec((tm, D_ATT), lambda i, j: (i, 0)),
                  pl.BlockSpec((tm, D_ATT), lambda i, j: (i, 0)),
                  pl.BlockSpec((D_ATT, tn), lambda i, j: (0, j)),
                  pl.BlockSpec((D_ATT, tn), lambda i, j: (0, j)),
                  pl.BlockSpec((tm, tn), lambda i, j: (i, ga0 + j)),
                  pl.BlockSpec((tm, tn), lambda i, j: (i, gb0 + j))],
        out_specs=pl.BlockSpec((tm, tn), lambda i, j: (i, j)),
        out_shape=jax.ShapeDtypeStruct((m, d), BF16),
        compiler_params=_cparams(2),
        name="merge",
    )(o_a, o_b, w1, w2, proj, proj)


def _proj_norm_kernel(y_ref, wo_ref, x_ref, gpost_ref, gpre_ref, x1_ref, h2_ref):
    a = _dot(y_ref[...], wo_ref[...])
    r = a * lax.rsqrt(jnp.mean(a * a, axis=-1, keepdims=True) + RMS_EPS) * gpost_ref[...]
    x1 = x_ref[...] + r
    x1_ref[...] = x1
    h2 = x1 * lax.rsqrt(jnp.mean(x1 * x1, axis=-1, keepdims=True) + RMS_EPS) * gpre_ref[...]
    h2_ref[...] = h2.astype(h2_ref.dtype)


def proj_norm(y, wo, x, g_post, g_pre, tm=256):
    m, d = x.shape
    tm = min(tm, m)
    row = lambda i: (i, 0)
    fix = lambda i: (0, 0)
    return pl.pallas_call(
        _proj_norm_kernel,
        grid=(m // tm,),
        in_specs=[pl.BlockSpec((tm, d), row), pl.BlockSpec((d, d), fix), pl.BlockSpec((tm, d), row),
                  pl.BlockSpec((1, d), fix), pl.BlockSpec((1, d), fix)],
        out_specs=[pl.BlockSpec((tm, d), row), pl.BlockSpec((tm, d), row)],
        out_shape=[jax.ShapeDtypeStruct((m, d), F32), jax.ShapeDtypeStruct((m, d), BF16)],
        compiler_params=_cparams(1),
        name="proj_norm",
    )(y, wo, x, g_post.reshape(1, d), g_pre.reshape(1, d))


def _mlp_kernel(h_ref, wu_ref, wd_ref, x_ref, g_ref, o_ref, acc_ref):
    k = pl.program_id(1)

    @pl.when(k == 0)
    def _():
        acc_ref[...] = jnp.zeros_like(acc_ref)

    u = jnp.maximum(_dot(h_ref[...], wu_ref[...]), 0.0)
    acc_ref[...] += _dot((u * u).astype(BF16), wd_ref[...])

    @pl.when(k == pl.num_programs(1) - 1)
    def _():
        a = acc_ref[...]
        o_ref[...] = x_ref[...] + a * lax.rsqrt(jnp.mean(a * a, axis=-1, keepdims=True) + RMS_EPS) * g_ref[...]


def mlp(h2, w_up, w_down, x1, g_post, tm, tk=512):
    m, d = x1.shape
    ff = w_up.shape[1]
    return pl.pallas_call(
        _mlp_kernel,
        grid=(m // tm, ff // tk),
        in_specs=[pl.BlockSpec((tm, d), lambda i, k: (i, 0)),
                  pl.BlockSpec((d, tk), lambda i, k: (0, k)),
                  pl.BlockSpec((tk, d), lambda i, k: (k, 0)),
                  pl.BlockSpec((tm, d), lambda i, k: (i, 0)),
                  pl.BlockSpec((1, d), lambda i, k: (0, 0))],
        out_specs=pl.BlockSpec((tm, d), lambda i, k: (i, 0)),
        out_shape=jax.ShapeDtypeStruct((m, d), F32),
        scratch_shapes=[pltpu.VMEM((tm, d), F32)],
        compiler_params=_cparams(2),
        name="mlp",
    )(h2, w_up, w_down, x1, g_post.reshape(1, d))


QB = 128


def _dsa_mask_prompt_kernel(qi_ref, sm_all_ref, sm_blk_ref, mask_ref, kia_ref, kib_ref, keys_ref, mstar_ref,
                            *, seq, top_k):
    i = pl.program_id(1)
    nq = seq // QB

    @pl.when(i == 0)
    def _():
        sm = sm_all_ref[...]
        lane = lax.broadcasted_iota(I32, sm.shape, 1)
        kia_ref[...] = jnp.where(lane < IDX_DIM, sm, 0.0).astype(BF16)
        kib_ref[...] = jnp.where(lane >= IDX_DIM, pltpu.roll(sm, IDX_DIM, axis=1), 0.0).astype(BF16)

    w_t = sm_blk_ref[...].T * IDX_W_SCALE
    qi = qi_ref[...].astype(BF16)
    q_pos = i * QB + lax.broadcasted_iota(I32, (QB, QB), 1)
    row = lax.broadcasted_iota(I32, (QB, QB), 0)

    def score_chunk(c, carry):
        r0 = pl.multiple_of(c * QB, QB)
        ka = kia_ref[pl.ds(r0, QB), :]
        kb = kib_ref[pl.ds(r0, QB), :]
        acc = jnp.zeros((QB, QB), F32)
        for p in range(N_IDX_HEADS // 2):
            qp = qi[:, p * LANES:(p + 1) * LANES]
            s0 = jnp.maximum(_dot_nt(ka, qp), 0.0)
            s1 = jnp.maximum(_dot_nt(kb, qp), 0.0)
            acc = acc + s0 * w_t[IDX_DIM + 2 * p:IDX_DIM + 2 * p + 1, :]
            acc = acc + s1 * w_t[IDX_DIM + 2 * p + 1:IDX_DIM + 2 * p + 2, :]
        sc = jnp.where(c * QB + row <= q_pos, acc, NEG_INF)
        keys_ref[pl.ds(r0, QB), :] = _float_key(sc)
        return carry

    lax.fori_loop(0, i + 1, score_chunk, 0)

    n_tail = (nq - 1 - i) * QB

    def count(pred_fn):
        def body(c, acc):
            k = keys_ref[pl.ds(pl.multiple_of(c * QB, QB), QB), :]
            ones = jnp.where(pred_fn(k, c), 1, 0)
            return acc + jnp.sum(ones.reshape(QB // SUBLANES, SUBLANES, QB), axis=0)

        acc = lax.fori_loop(0, i + 1, body, jnp.zeros((SUBLANES, QB), I32))
        return jnp.sum(acc, axis=0, keepdims=True)

    def count_ge(cand):
        return count(lambda k, c: k >= cand) + jnp.where(KEY_NEG_INF >= cand, n_tail, 0)

    base = jnp.where(count_ge(jnp.zeros((1, QB), I32)) >= top_k, 0, INT_MIN).astype(I32)

    def bit_body(t, base):
        cand = base | lax.shift_left(jnp.int32(1), 30 - t)
        return jnp.where(count_ge(cand) >= top_k, cand, base)

    v = lax.fori_loop(0, 31, bit_body, base)

    n_gt = count(lambda k, c: k > v) + jnp.where(KEY_NEG_INF > v, n_tail, 0)
    need = top_k - n_gt
    n_eq = count(lambda k, c: k == v)
    mstar_ref[...] = jnp.full((SUBLANES, QB), seq, I32)

    @pl.when(jnp.max(jnp.where(n_eq > need, 1, 0)) > 0)
    def _():
        def idx_bit(t, m):
            cand = m | lax.shift_left(jnp.int32(1), (seq - 1).bit_length() - 1 - t)
            below = count(lambda k, c: (k == v) & (c * QB + row < cand))
            return jnp.where(below < need, cand, m)

        m = lax.fori_loop(0, (seq - 1).bit_length(), idx_bit, jnp.zeros((1, QB), I32))
        mstar_ref[...] = jnp.broadcast_to(m, (SUBLANES, QB))

    mstar = mstar_ref[0:1, :]

    def emit(c, carry):
        k = keys_ref[pl.ds(pl.multiple_of(c * QB, QB), QB), :]
        k_pos = c * QB + row
        sel = (k > v) | ((k == v) & (k_pos <= mstar))
        m_t = jnp.where(sel & (k_pos <= q_pos), 1.0, 0.0)
        mask_ref[c] = m_t.T.astype(mask_ref.dtype)
        return carry

    lax.fori_loop(0, i + 1, emit, 0)

    def clear(c, carry):
        mask_ref[c] = jnp.zeros((QB, QB), mask_ref.dtype)
        return carry

    lax.fori_loop(i + 1, nq, clear, 0)


def dsa_mask_prompt(proj, batch, seq, c_small):
    nq = seq // QB
    top_k = min(TOP_K_MAX, seq // 4)
    return pl.pallas_call(
        functools.partial(_dsa_mask_prompt_kernel, seq=seq, top_k=top_k),
        grid=(batch, nq),
        in_specs=[pl.BlockSpec((QB, N_IDX_HEADS * IDX_DIM), lambda b, i: (b * nq + i, C_QI // (N_IDX_HEADS * IDX_DIM))),
                  pl.BlockSpec((seq, LANES), lambda b, i: (b, c_small // LANES)),
                  pl.BlockSpec((QB, LANES), lambda b, i: (b * nq + i, c_small // LANES))],
        out_specs=pl.BlockSpec((None, nq, QB, QB), lambda b, i: (b * nq + i, 0, 0, 0)),
        out_shape=jax.ShapeDtypeStruct((batch * nq, nq, QB, QB), BF16),
        scratch_shapes=[pltpu.VMEM((seq, LANES), BF16), pltpu.VMEM((seq, LANES), BF16),
                        pltpu.VMEM((seq, QB), I32), pltpu.VMEM((SUBLANES, QB), I32)],
        compiler_params=_cparams(2),
        name="dsa_mask_prompt",
    )(proj, proj, proj)


def _bias_tables(rb_ref, bkt_ref, h, n_tab):
    out = []
    for t in range(n_tab):
        bk = bkt_ref[t]
        acc = jnp.zeros(bk.shape, F32)
        for b in range(N_REL_BUCKETS):
            acc = jnp.where(bk == b, rb_ref[b, h], acc)
        out.append(acc)
    return out


def _dsa_attn_prompt_kernel(rb_ref, q_ref, k_ref, v_ref, mask_ref, bkt_ref, o_ref, tab_ref, lg_ref, kb_ref, vb_ref):
    h = pl.program_id(1)
    i = pl.program_id(2)

    @pl.when(i == 0)
    def _():
        kb_ref[...] = k_ref[...].astype(BF16)
        vb_ref[...] = v_ref[...].astype(BF16)
        t0, t1 = _bias_tables(rb_ref, bkt_ref, h, 2)
        tab_ref[0] = t0
        tab_ref[1] = t1
        tab_ref[2] = jnp.full((QB, QB), rb_ref[N_REL_BUCKETS - 1, h], F32)

    q = q_ref[...].astype(BF16)

    def logits_chunk(c, mx):
        kc = kb_ref[pl.ds(pl.multiple_of(c * QB, QB), QB), :]
        s = _dot_nt(q, kc) * ATTN_SCALE + tab_ref[jnp.minimum(i - c, 2)]
        s = jnp.where(mask_ref[c].astype(F32) > 0.5, s, NEG_INF)
        lg_ref[c] = s
        return jnp.maximum(mx, s)

    mx = lax.fori_loop(0, i + 1, logits_chunk, jnp.full((QB, QB), NEG_INF, F32))
    m = jnp.max(mx, axis=1, keepdims=True)

    def pv_chunk(c, carry):
        acc, l = carry
        p = jnp.exp(lg_ref[c] - m)
        vc = vb_ref[pl.ds(pl.multiple_of(c * QB, QB), QB), :]
        return acc + _dot(p.astype(BF16), vc), l + p

    acc, l = lax.fori_loop(0, i + 1, pv_chunk, (jnp.zeros((QB, HEAD_DIM), F32), jnp.zeros((QB, QB), F32)))
    o_ref[...] = (acc / jnp.sum(l, axis=1, keepdims=True)).astype(o_ref.dtype)


def _prompt_bucket_tables():
    r = np.arange(QB)[:, None]
    u = np.arange(QB)[None, :]
    assert _rel_bucket_np(np.array([QB + 1]))[0] == N_REL_BUCKETS - 1
    return np.stack([_rel_bucket_np(r - u), _rel_bucket_np(QB + r - u)]).astype(np.int32)


def dsa_attn_prompt(proj, mask, rel_bias, batch, seq):
    nq = seq // QB
    hd = HEAD_DIM
    return pl.pallas_call(
        _dsa_attn_prompt_kernel,
        grid=(batch, N_HEADS, nq),
        in_specs=[pl.BlockSpec(memory_space=pltpu.SMEM),
                  pl.BlockSpec((QB, hd), lambda b, h, i: (b * nq + i, C_QD // hd + h)),
                  pl.BlockSpec((seq, hd), lambda b, h, i: (b, C_KD // hd + h)),
                  pl.BlockSpec((seq, hd), lambda b, h, i: (b, C_VD // hd + h)),
                  pl.BlockSpec((None, nq, QB, QB), lambda b, h, i: (b * nq + i, 0, 0, 0)),
                  pl.BlockSpec((2, QB, QB), lambda b, h, i: (0, 0, 0))],
        out_specs=pl.BlockSpec((QB, hd), lambda b, h, i: (b * nq + i, h)),
        out_shape=jax.ShapeDtypeStruct((batch * seq, D_ATT), BF16),
        scratch_shapes=[pltpu.VMEM((3, QB, QB), F32), pltpu.VMEM((nq, QB, QB), F32),
                        pltpu.VMEM((seq, hd), BF16), pltpu.VMEM((seq, hd), BF16)],
        compiler_params=_cparams(3),
        name="dsa_attn_prompt",
    )(rel_bias, proj, proj, proj, mask, jnp.asarray(_prompt_bucket_tables()))


SB_T = 256


def _suffix_sum_matrix(n):
    s = np.arange(n)
    tri = (s[:, None] > s[None, :]).astype(np.float32)
    return np.concatenate([tri, np.ones((n, n), np.float32)], axis=1)


def _sb_prompt_kernel(q_ref, k_ref, v_ref, r_ref, o_ref, kb_ref, vb_ref, acc_ref, run_ref):
    i = pl.program_id(2)

    @pl.when(i == 0)
    def _():
        kb_ref[...] = k_ref[...].astype(BF16)
        vb_ref[...] = v_ref[...].astype(BF16)

    q = q_ref[...].astype(BF16)
    acc_ref[...] = jnp.zeros_like(acc_ref)
    run_ref[...] = jnp.zeros_like(run_ref)
    row = lax.broadcasted_iota(I32, (SB_T, SB_T), 0)
    col = lax.broadcasted_iota(I32, (SB_T, SB_T), 1)

    def chunk(step, carry):
        c = i - step
        r0 = pl.multiple_of(c * SB_T, SB_T)
        z = _dot_nt(q, kb_ref[pl.ds(r0, SB_T), :]) * ATTN_SCALE
        sp = _softplus(z)
        before = col < row + step * SB_T
        log_keep = jnp.where(before, -sp, 0.0)
        cs = _split_dot(log_keep, r_ref[...], 2)
        later = cs[:, :SB_T] + run_ref[...]
        a = jnp.where(before, jnp.exp(z - sp + later), 0.0)
        acc_ref[...] += _dot(a.astype(BF16), vb_ref[pl.ds(r0, SB_T), :])
        run_ref[...] += cs[:, SB_T:]
        return carry

    lax.fori_loop(0, i + 1, chunk, 0)
    o_ref[...] = acc_ref[...].astype(o_ref.dtype)


def sb_prompt(proj, batch, seq):
    nq = seq // SB_T
    hd = HEAD_DIM
    r = jnp.asarray(_suffix_sum_matrix(SB_T), BF16)
    return pl.pallas_call(
        _sb_prompt_kernel,
        grid=(batch, N_HEADS, nq),
        in_specs=[pl.BlockSpec((SB_T, hd), lambda b, h, i: (b * nq + i, C_QS // hd + h)),
                  pl.BlockSpec((seq, hd), lambda b, h, i: (b, C_KS // hd + h)),
                  pl.BlockSpec((seq, hd), lambda b, h, i: (b, C_VS // hd + h)),
                  pl.BlockSpec((SB_T, 2 * SB_T), lambda b, h, i: (0, 0))],
        out_specs=pl.BlockSpec((SB_T, hd), lambda b, h, i: (b * nq + i, h)),
        out_shape=jax.ShapeDtypeStruct((batch * seq, D_ATT), BF16),
        scratch_shapes=[pltpu.VMEM((seq, hd), BF16), pltpu.VMEM((seq, hd), BF16),
                        pltpu.VMEM((SB_T, hd), F32), pltpu.VMEM((SB_T, SB_T), F32)],
        compiler_params=_cparams(3),
        name="sb_prompt",
    )(proj, proj, proj, r)


TOK_PAD = SUBLANES


def _pad_rows(x, rows):
    return jnp.concatenate([x, jnp.zeros((rows - x.shape[0], x.shape[1]), x.dtype)], axis=0)


def _head_rows(page_ref, h, page_size):
    return page_ref[pl.ds(h, page_size, stride=N_HEADS), :].astype(BF16)


def _page_specs(n_pages, rows, cols):
    return [pl.BlockSpec((None, rows, cols), functools.partial(lambda b, pt, p: (pt[b, p], 0, 0), p=p))
            for p in range(n_pages)]


def _sb_sample_kernel(pt_ref, q_ref, kn_ref, vn_ref, r_ref, *rest, n_pages, page_size):
    k_pages = rest[:n_pages]
    v_pages = rest[n_pages:2 * n_pages]
    o_ref = rest[2 * n_pages]
    grp = N_HEADS * TOK_PAD
    q = [q_ref[:, h * HEAD_DIM:(h + 1) * HEAD_DIM].astype(BF16) for h in range(N_HEADS)]
    row = lax.broadcasted_iota(I32, (grp, page_size), 0) % TOK_PAD
    lane = lax.broadcasted_iota(I32, (grp, page_size), 1)
    run = jnp.zeros((grp, page_size), F32)
    acc = [jnp.zeros((TOK_PAD, HEAD_DIM), F32) for _ in range(N_HEADS)]
    for g in [n_pages] + list(range(n_pages - 1, -1, -1)):
        if g == n_pages:
            k_h = [_pad_rows(kn_ref[:, h * HEAD_DIM:(h + 1) * HEAD_DIM], page_size).astype(BF16) for h in range(N_HEADS)]
            v_h = [_pad_rows(vn_ref[:, h * HEAD_DIM:(h + 1) * HEAD_DIM], page_size).astype(BF16) for h in range(N_HEADS)]
            before = lane < row
        else:
            k_h = [_head_rows(k_pages[g], h, page_size) for h in range(N_HEADS)]
            v_h = [_head_rows(v_pages[g], h, page_size) for h in range(N_HEADS)]
            before = None
        z = jnp.concatenate([_dot_nt(q[h], k_h[h]) for h in range(N_HEADS)], axis=0) * ATTN_SCALE
        sp = _softplus(z)
        log_keep = -sp if before is None else jnp.where(before, -sp, 0.0)
        cs = _split_dot(log_keep, r_ref[...], 3)
        e = jnp.exp(z - sp + cs[:, :page_size] + run)
        a = e if before is None else jnp.where(before, e, 0.0)
        run = run + cs[:, page_size:]
        for h in range(N_HEADS):
            acc[h] = acc[h] + _dot(a[h * TOK_PAD:(h + 1) * TOK_PAD, :].astype(BF16), v_h[h])
    for h in range(N_HEADS):
        o_ref[:, h * HEAD_DIM:(h + 1) * HEAD_DIM] = acc[h]


def sb_sample(proj8, k_pool, v_pool, page_table):
    n_seq = proj8.shape[0]
    n_pages = page_table.shape[1]
    page_size = k_pool.shape[1] // N_HEADS
    assert page_size == LANES
    r = jnp.asarray(_suffix_sum_matrix(page_size), BF16)
    tok = lambda col: pl.BlockSpec((None, TOK_PAD, D_ATT), lambda b, pt: (b, 0, col // D_ATT))
    grid_spec = pltpu.PrefetchScalarGridSpec(
        num_scalar_prefetch=1,
        grid=(n_seq,),
        in_specs=[tok(C_QS), tok(C_KS), tok(C_VS), pl.BlockSpec((page_size, 2 * page_size), lambda b, pt: (0, 0))]
                 + _page_specs(n_pages, page_size * N_HEADS, HEAD_DIM)
                 + _page_specs(n_pages, page_size * N_HEADS, HEAD_DIM),
        out_specs=pl.BlockSpec((None, TOK_PAD, D_ATT), lambda b, pt: (b, 0, 0)),
    )
    return pl.pallas_call(
        functools.partial(_sb_sample_kernel, n_pages=n_pages, page_size=page_size),
        grid_spec=grid_spec,
        out_shape=jax.ShapeDtypeStruct((n_seq, TOK_PAD, D_ATT), F32),
        compiler_params=_cparams(1, vmem_mb=56),
        name="sb_sample",
    )(page_table, proj8, proj8, proj8, r, *([k_pool] * n_pages), *([v_pool] * n_pages))


def _dsa_scores_sample_kernel(pt_ref, qi_ref, w_ref, kin_ref, *rest, n_pages, page_size, n_new):
    ki_pages = rest[:n_pages]
    o_ref = rest[n_pages]
    qi = qi_ref[...].astype(BF16)
    w = w_ref[...] * IDX_W_SCALE
    o_ref[...] = jnp.zeros_like(o_ref)
    for g in range(n_pages + 1):
        ki = (kin_ref if g == n_pages else ki_pages[g])[...].astype(BF16)
        s = jnp.maximum(_dot_nt(qi, ki), 0.0) * w
        s = jnp.sum(s.reshape(n_new, N_IDX_HEADS, page_size), axis=1)
        o_ref[0:n_new, g * page_size:(g + 1) * page_size] = s


def dsa_scores_sample(qi3, w_rep, ki_new_pad, ki_pool, page_table):
    n_seq = qi3.shape[0]
    n_new = qi3.shape[1] // N_IDX_HEADS
    n_pages = page_table.shape[1]
    page_size = ki_pool.shape[1]
    assert page_size == LANES
    seq_blk = lambda r, c: pl.BlockSpec((None, r, c), lambda b, pt: (b, 0, 0))
    grid_spec = pltpu.PrefetchScalarGridSpec(
        num_scalar_prefetch=1,
        grid=(n_seq,),
        in_specs=[seq_blk(n_new * N_IDX_HEADS, IDX_DIM), seq_blk(n_new * N_IDX_HEADS, LANES),
                  seq_blk(page_size, IDX_DIM)] + _page_specs(n_pages, page_size, IDX_DIM),
        out_specs=seq_blk(TOK_PAD, (n_pages + 1) * page_size),
    )
    return pl.pallas_call(
        functools.partial(_dsa_scores_sample_kernel, n_pages=n_pages, page_size=page_size, n_new=n_new),
        grid_spec=grid_spec,
        out_shape=jax.ShapeDtypeStruct((n_seq, TOK_PAD, (n_pages + 1) * page_size), F32),
        compiler_params=_cparams(1),
        name="dsa_scores_sample",
    )(page_table, qi3, w_rep, ki_new_pad, *([ki_pool] * n_pages))


MASK_ROWS = 128


def _dsa_mask_sample_kernel(s_ref, ones_ref, m_ref, *, n_past, n_new, top_k):
    n_chunks = s_ref.shape[1] // LANES
    shape = (MASK_ROWS, LANES)
    tok = lax.broadcasted_iota(I32, shape, 0) % TOK_PAD
    lane = lax.broadcasted_iota(I32, shape, 1)
    ones = ones_ref[...]
    keys, pos, exists = [], [], []
    for c in range(n_chunks):
        s = s_ref[:, c * LANES:(c + 1) * LANES]
        k_pos = c * LANES + lane
        s = jnp.where(k_pos <= n_past + tok, s, NEG_INF)
        keys.append(_float_key(s))
        pos.append(k_pos)
        exists.append(None if (c + 1) * LANES <= n_past + n_new else k_pos < n_past + n_new)

    def count(pred_fn):
        part = jnp.zeros(shape, F32)
        for c in range(n_chunks):
            p = pred_fn(keys[c], pos[c])
            if exists[c] is not None:
                p = p & exists[c]
            part = part + jnp.where(p, 1.0, 0.0)
        return _dot(part.astype(BF16), ones)

    base = jnp.where(count(lambda k, p: k >= 0) >= top_k, 0, INT_MIN).astype(I32)

    def bit_body(t, base):
        cand = base | lax.shift_left(jnp.int32(1), 30 - t)
        return jnp.where(count(lambda k, p: k >= cand) >= top_k, cand, base)

    v = lax.fori_loop(0, 31, bit_body, base)
    need = top_k - count(lambda k, p: k > v)

    n_bits = (n_chunks * LANES - 1).bit_length()

    def idx_bit(t, m):
        cand = m | lax.shift_left(jnp.int32(1), n_bits - 1 - t)
        below = count(lambda k, p: (k == v) & (p < cand))
        return jnp.where(below < need, cand, m)

    mstar = lax.fori_loop(0, n_bits, idx_bit, jnp.zeros(shape, I32))
    for c in range(n_chunks):
        sel = (keys[c] > v) | ((keys[c] == v) & (pos[c] <= mstar))
        sel = sel & (pos[c] <= n_past + tok) & (tok < n_new)
        if exists[c] is not None:
            sel = sel & exists[c]
        m_ref[:, c * LANES:(c + 1) * LANES] = jnp.where(sel, 1.0, 0.0)


def dsa_mask_sample(scores2, n_past, n_new):
    rows, width = scores2.shape
    top_k = min(TOP_K_MAX, (n_past + n_new) // 4)
    return pl.pallas_call(
        functools.partial(_dsa_mask_sample_kernel, n_past=n_past, n_new=n_new, top_k=top_k),
        grid=(rows // MASK_ROWS,),
        in_specs=[pl.BlockSpec((MASK_ROWS, width), lambda i: (i, 0)), pl.BlockSpec((LANES, LANES), lambda i: (0, 0))],
        out_specs=pl.BlockSpec((MASK_ROWS, width), lambda i: (i, 0)),
        out_shape=jax.ShapeDtypeStruct((rows, width), F32),
        compiler_params=_cparams(1),
        name="dsa_mask_sample",
    )(scores2, jnp.ones((LANES, LANES), BF16))


def _sample_bucket_tables(n_new, page_size):
    j = (np.arange(N_HEADS * TOK_PAD) % TOK_PAD)[:, None]
    u = np.arange(page_size)[None, :]
    assert _rel_bucket_np(np.array([page_size + 1]))[0] == N_REL_BUCKETS - 1
    return np.stack([_rel_bucket_np(page_size + j - u), _rel_bucket_np(j - u)]).astype(np.int32)


def _dsa_attn_sample_kernel(pt_ref, rb_ref, q_ref, kn_ref, vn_ref, m_ref, bkt_ref, *rest, n_pages, page_size, n_new):
    k_pages = rest[:n_pages]
    v_pages = rest[n_pages:2 * n_pages]
    o_ref = rest[2 * n_pages]
    tab_ref = rest[2 * n_pages + 1]
    lg_ref = rest[2 * n_pages + 2]
    grp = N_HEADS * TOK_PAD

    @pl.when(pl.program_id(0) == 0)
    def _():
        for h in range(N_HEADS):
            rows = slice(h * TOK_PAD, (h + 1) * TOK_PAD)
            for t in range(2):
                bk = bkt_ref[t, rows, :]
                acc = jnp.zeros(bk.shape, F32)
                for b in range(N_REL_BUCKETS):
                    acc = jnp.where(bk == b, rb_ref[b, h], acc)
                tab_ref[t, rows, :] = acc
            tab_ref[2, rows, :] = jnp.full((TOK_PAD, page_size), rb_ref[N_REL_BUCKETS - 1, h], F32)

    q = [q_ref[:, h * HEAD_DIM:(h + 1) * HEAD_DIM].astype(BF16) for h in range(N_HEADS)]
    mx = jnp.full((grp, page_size), NEG_INF, F32)
    for g in range(n_pages + 1):
        if g == n_pages:
            k_h = [_pad_rows(kn_ref[:, h * HEAD_DIM:(h + 1) * HEAD_DIM], page_size).astype(BF16) for h in range(N_HEADS)]
            bias = tab_ref[1]
        else:
            k_h = [_head_rows(k_pages[g], h, page_size) for h in range(N_HEADS)]
            bias = tab_ref[0] if g == n_pages - 1 else tab_ref[2]
        s = jnp.concatenate([_dot_nt(q[h], k_h[h]) for h in range(N_HEADS)], axis=0) * ATTN_SCALE + bias
        sel = jnp.tile(m_ref[:, g * page_size:(g + 1) * page_size], (N_HEADS, 1))
        s = jnp.where(sel > 0.5, s, NEG_INF)
        lg_ref[g] = s
        mx = jnp.maximum(mx, s)
    m = jnp.max(mx, axis=1, keepdims=True)
    l = jnp.zeros((grp, page_size), F32)
    acc = [jnp.zeros((TOK_PAD, HEAD_DIM), F32) for _ in range(N_HEADS)]
    for g in range(n_pages + 1):
        p = jnp.exp(lg_ref[g] - m)
        l = l + p
        for h in range(N_HEADS):
            if g == n_pages:
                v_h = _pad_rows(vn_ref[:, h * HEAD_DIM:(h + 1) * HEAD_DIM], page_size).astype(BF16)
            else:
                v_h = _head_rows(v_pages[g], h, page_size)
            acc[h] = acc[h] + _dot(p[h * TOK_PAD:(h + 1) * TOK_PAD, :].astype(BF16), v_h)
    inv = 1.0 / jnp.sum(l, axis=1, keepdims=True)
    for h in range(N_HEADS):
        o_ref[:, h * HEAD_DIM:(h + 1) * HEAD_DIM] = acc[h] * inv[h * TOK_PAD:(h + 1) * TOK_PAD, :]


def dsa_attn_sample(proj8, mask3, rel_bias, k_pool, v_pool, page_table, n_new):
    n_seq = proj8.shape[0]
    n_pages = page_table.shape[1]
    page_size = k_pool.shape[1] // N_HEADS
    assert page_size == LANES
    grp = N_HEADS * TOK_PAD
    width = (n_pages + 1) * page_size
    tok = lambda col: pl.BlockSpec((None, TOK_PAD, D_ATT), lambda b, pt: (b, 0, col // D_ATT))
    grid_spec = pltpu.PrefetchScalarGridSpec(
        num_scalar_prefetch=1,
        grid=(n_seq,),
        in_specs=[pl.BlockSpec(memory_space=pltpu.SMEM), tok(C_QD), tok(C_KD), tok(C_VD),
                  pl.BlockSpec((None, TOK_PAD, width), lambda b, pt: (b, 0, 0)),
                  pl.BlockSpec((2, grp, page_size), lambda b, pt: (0, 0, 0))]
                 + _page_specs(n_pages, page_size * N_HEADS, HEAD_DIM)
                 + _page_specs(n_pages, page_size * N_HEADS, HEAD_DIM),
        out_specs=pl.BlockSpec((None, TOK_PAD, D_ATT), lambda b, pt: (b, 0, 0)),
        scratch_shapes=[pltpu.VMEM((3, grp, page_size), F32), pltpu.VMEM((n_pages + 1, grp, page_size), F32)],
    )
    return pl.pallas_call(
        functools.partial(_dsa_attn_sample_kernel, n_pages=n_pages, page_size=page_size, n_new=n_new),
        grid_spec=grid_spec,
        out_shape=jax.ShapeDtypeStruct((n_seq, TOK_PAD, D_ATT), F32),
        compiler_params=_cparams(1, vmem_mb=56),
        name="dsa_attn_sample",
    )(page_table, rel_bias, proj8, proj8, proj8, mask3, jnp.asarray(_sample_bucket_tables(n_new, page_size)),
      *([k_pool] * n_pages), *([v_pool] * n_pages))


def _rearranged_w_in(w_in, d_model):
    c_ki = 4 * D_ATT
    c_after = c_ki + IDX_DIM + N_IDX_HEADS
    wide = jnp.concatenate([w_in[:, :c_ki], w_in[:, c_after:]], axis=1)
    c_small = wide.shape[1]
    narrow = w_in[:, c_ki:c_after]
    pad = PROJ_TN - narrow.shape[1]
    w = jnp.concatenate([wide, narrow, jnp.zeros((w_in.shape[0], pad), w_in.dtype)], axis=1)
    return w.astype(BF16), c_small


def _tail(x, proj, o_a, o_b, weights, gains):
    w1, w2, wo, wu, wd = weights
    g_attn_post, g_mlp_pre, g_mlp_post = gains
    tm = min(512, x.shape[0])
    y = merge_branches(o_a, o_b, w1, w2, proj, tm)
    x1, h2 = proj_norm(y, wo, x, g_attn_post, g_mlp_pre)
    return mlp(h2, wu, wd, x1, g_mlp_post, tm)


def kernel(x_prompt, x_sample, cache_k_dsa, cache_v_dsa, cache_k_idx, cache_k_sb, cache_v_sb, page_table, rel_bias,
           w_in, w_out_dsa, w_out_sb, w_o, w_up, w_down, g_attn_pre, g_attn_post, g_mlp_pre, g_mlp_post):
    batch, seq, d_model = x_prompt.shape
    n_seq, n_new, _ = x_sample.shape
    depth = w_in.shape[0]
    n_pool, page_size = cache_k_idx.shape[1], cache_k_idx.shape[2]
    n_past = page_table.shape[1] * page_size

    xp = x_prompt.reshape(batch * seq, d_model)
    xs = x_sample.reshape(n_seq * n_new, d_model)
    rows_p, rows_s = [], []
    for l in range(depth):
        w_in_r, c_small = _rearranged_w_in(w_in[l], d_model)
        weights = tuple(w[l].astype(BF16) for w in (w_out_dsa, w_out_sb, w_o, w_up, w_down))
        gains = (g_attn_post[l], g_mlp_pre[l], g_mlp_post[l])

        hp = rms_cast(xp, g_attn_pre[l])
        proj = matmul(hp, w_in_r, tm=min(1024, hp.shape[0]), tn=PROJ_TN)
        mask = dsa_mask_prompt(proj, batch, seq, c_small)
        o_a = dsa_attn_prompt(proj, mask, rel_bias, batch, seq)
        o_b = sb_prompt(proj, batch, seq)
        xp_new = _tail(xp, proj, o_a, o_b, weights, gains)
        heads = lambda c: proj[:, c:c + D_ATT].reshape(batch, seq, N_HEADS, HEAD_DIM)
        rows_p.append((heads(C_KD), heads(C_VD), proj[:, c_small:c_small + IDX_DIM].reshape(batch, seq, IDX_DIM),
                       heads(C_KS), heads(C_VS)))
        xp = xp_new

        hs = rms_cast(xs, g_attn_pre[l])
        proj_s = matmul(hs, w_in_r, tm=min(512, hs.shape[0]), tn=PROJ_TN)
        proj8 = jnp.pad(proj_s[:, :C_GA].reshape(n_seq, n_new, C_GA), ((0, 0), (0, TOK_PAD - n_new), (0, 0)))
        qi3 = proj_s[:, C_QI:C_QI + N_IDX_HEADS * IDX_DIM].reshape(n_seq, n_new * N_IDX_HEADS, IDX_DIM)
        w_rep = jnp.broadcast_to(
            proj_s[:, c_small + IDX_DIM:c_small + IDX_DIM + N_IDX_HEADS].reshape(n_seq, n_new * N_IDX_HEADS, 1),
            (n_seq, n_new * N_IDX_HEADS, LANES))
        ki_new = proj_s[:, c_small:c_small + IDX_DIM].reshape(n_seq, n_new, IDX_DIM)
        ki_new_pad = jnp.pad(ki_new, ((0, 0), (0, page_size - n_new), (0, 0)))
        pool2 = lambda c: c[l].reshape(n_pool, page_size * N_HEADS, HEAD_DIM)
        scores = dsa_scores_sample(qi3, w_rep, ki_new_pad, cache_k_idx[l], page_table)
        mask_s = dsa_mask_sample(scores.reshape(n_seq * TOK_PAD, scores.shape[2]), n_past, n_new)
        o_a = dsa_attn_sample(proj8, mask_s.reshape(scores.shape), rel_bias, pool2(cache_k_dsa), pool2(cache_v_dsa),
                              page_table, n_new)
        o_b = sb_sample(proj8, pool2(cache_k_sb), pool2(cache_v_sb), page_table)
        unpad = lambda o: o[:, :n_new, :].reshape(n_seq * n_new, D_ATT)
        xs_new = _tail(xs, proj_s, unpad(o_a), unpad(o_b), weights, gains)
        heads_s = lambda c: proj_s[:, c:c + D_ATT].reshape(n_seq, n_new, N_HEADS, HEAD_DIM)
        rows_s.append((heads_s(C_KD), heads_s(C_VD), ki_new, heads_s(C_KS), heads_s(C_VS)))
        xs = xs_new

    outs_p = [jnp.stack(r, axis=0) for r in zip(*rows_p)]
    outs_s = [jnp.stack(r, axis=0) for r in zip(*rows_s)]
    return (xp.reshape(batch, seq, d_model), xs.reshape(n_seq, n_new, d_model), *outs_p, *outs_s)
```

```python
import functools
import math

import numpy as np
import jax
import jax.numpy as jnp
from jax import lax
from jax.experimental import pallas as pl
from jax.experimental.pallas import tpu as pltpu

F32 = jnp.float32
BF16 = jnp.bfloat16
I32 = jnp.int32

HEAD_DIM = 128
N_HEADS = 8
N_IDX_HEADS = 16
IDX_DIM = 64
IDX_W_SCALE = (N_IDX_HEADS * IDX_DIM) ** -0.5
TOP_K_MAX = 256
N_REL_BUCKETS = 32
REL_MAX_DISTANCE = 128
RMS_EPS = 1e-6
NEG_INF = -1e30
ATTN_SCALE = HEAD_DIM ** -0.5

LANES = 128
SUBLANES = 8
D_ATT = N_HEADS * HEAD_DIM

C_QD, C_KD, C_VD, C_QI, C_QS, C_KS, C_VS = (k * D_ATT for k in range(7))
C_GA = 7 * D_ATT
PROJ_TN = 512


def _key_of(x):
    b = int(np.float32(x).view(np.int32))
    return b if b >= 0 else b ^ 0x7FFFFFFF


KEY_NEG_INF = _key_of(NEG_INF)
INT_MIN = -(2 ** 31)


def _cparams(n_axes, vmem_mb=48):
    return pltpu.CompilerParams(dimension_semantics=("arbitrary",) * n_axes,
                                vmem_limit_bytes=vmem_mb * 1024 * 1024)


def _dot(a, b):
    return jnp.dot(a, b, preferred_element_type=F32)


def _dot_nt(a, b):
    return lax.dot_general(a, b, (((1,), (1,)), ((), ())), preferred_element_type=F32)


def _split_dot(x, m01, passes):
    out = None
    rem = x
    for p in range(passes):
        piece = rem.astype(BF16)
        d = _dot(piece, m01)
        out = d if out is None else out + d
        if p + 1 < passes:
            rem = rem - piece.astype(F32)
    return out


def _softplus(z):
    return jnp.maximum(z, 0.0) + jnp.log1p(jnp.exp(-jnp.abs(z)))


def _float_key(s):
    s = jnp.where(s == 0.0, 0.0, s)
    bits = lax.bitcast_convert_type(s, I32)
    return jnp.where(bits >= 0, bits, bits ^ 0x7FFFFFFF)


def _rel_bucket_np(dist):
    max_exact = N_REL_BUCKETS // 2
    d = np.maximum(dist, 0)
    df = np.maximum(d, 1).astype(np.float32)
    log_part = np.log(df / np.float32(max_exact)) / np.float32(math.log(REL_MAX_DISTANCE / max_exact))
    large = max_exact + (log_part * np.float32(N_REL_BUCKETS - max_exact)).astype(np.int32)
    return np.where(d < max_exact, d, np.minimum(large, N_REL_BUCKETS - 1)).astype(np.int32)


def _rms_cast_kernel(x_ref, g_ref, o_ref):
    x = x_ref[...]
    ms = jnp.mean(x * x, axis=-1, keepdims=True)
    o_ref[...] = (x * lax.rsqrt(ms + RMS_EPS) * g_ref[...]).astype(o_ref.dtype)


def rms_cast(x, g, tm=256):
    m, d = x.shape
    tm = min(tm, m)
    return pl.pallas_call(
        _rms_cast_kernel,
        grid=(m // tm,),
        in_specs=[pl.BlockSpec((tm, d), lambda i: (i, 0)), pl.BlockSpec((1, d), lambda i: (0, 0))],
        out_specs=pl.BlockSpec((tm, d), lambda i: (i, 0)),
        out_shape=jax.ShapeDtypeStruct((m, d), BF16),
        compiler_params=_cparams(1),
        name="rms_cast",
    )(x, g.reshape(1, d))


def _mm_kernel(a_ref, w_ref, o_ref):
    o_ref[...] = _dot(a_ref[...], w_ref[...]).astype(o_ref.dtype)


def matmul(a, w, tm, tn, out_dtype=F32):
    m, k = a.shape
    n = w.shape[1]
    return pl.pallas_call(
        _mm_kernel,
        grid=(m // tm, n // tn),
        in_specs=[pl.BlockSpec((tm, k), lambda i, j: (i, 0)), pl.BlockSpec((k, tn), lambda i, j: (0, j))],
        out_specs=pl.BlockSpec((tm, tn), lambda i, j: (i, j)),
        out_shape=jax.ShapeDtypeStruct((m, n), out_dtype),
        compiler_params=_cparams(2),
        name="in_proj",
    )(a, w)


def _merge_kernel(oa_ref, ob_ref, w1_ref, w2_ref, ga_ref, gb_ref, y_ref):
    ya = _dot(oa_ref[...].astype(BF16), w1_ref[...])
    yb = _dot(ob_ref[...].astype(BF16), w2_ref[...])
    sa = 1.0 / (1.0 + jnp.exp(-ga_ref[...]))
    sb = 1.0 / (1.0 + jnp.exp(-gb_ref[...]))
    y_ref[...] = (sa * ya + sb * yb).astype(y_ref.dtype)


def merge_branches(o_a, o_b, w1, w2, proj, tm, tn=512):
    m = o_a.shape[0]
    d = w1.shape[1]
    ga0 = C_GA // tn
    gb0 = (C_GA + d) // tn
    return pl.pallas_call(
        _merge_kernel,
        grid=(m // tm, d // tn),
        in_specs=[pl.BlockSpec((tm, D_ATT), lambda i, j: (i, 0)),
                  pl.BlockSpec((tm, D_ATT), lambda i, j: (i, 0)),
                  pl.BlockSpec((D_ATT, tn), lambda i, j: (0, j)),
                  pl.BlockSpec((D_ATT, tn), lambda i, j: (0, j)),
                  pl.BlockSpec((tm, tn), lambda i, j: (i, ga0 + j)),
                  pl.BlockSpec((tm, tn), lambda i, j: (i, gb0 + j))],
        out_specs=pl.BlockSpec((tm, tn), lambda i, j: (i, j)),
        out_shape=jax.ShapeDtypeStruct((m, d), BF16),
        compiler_params=_cparams(2),
        name="merge",
    )(o_a, o_b, w1, w2, proj, proj)


def _proj_norm_kernel(y_ref, wo_ref, x_ref, gpost_ref, gpre_ref, x1_ref, h2_ref):
    a = _dot(y_ref[...], wo_ref[...])
    r = a * lax.rsqrt(jnp.mean(a * a, axis=-1, keepdims=True) + RMS_EPS) * gpost_ref[...]
    x1 = x_ref[...] + r
    x1_ref[...] = x1
    h2 = x1 * lax.rsqrt(jnp.mean(x1 * x1, axis=-1, keepdims=True) + RMS_EPS) * gpre_ref[...]
    h2_ref[...] = h2.astype(h2_ref.dtype)


def proj_norm(y, wo, x, g_post, g_pre, tm=256):
    m, d = x.shape
    tm = min(tm, m)
    row = lambda i: (i, 0)
    fix = lambda i: (0, 0)
    return pl.pallas_call(
        _proj_norm_kernel,
        grid=(m // tm,),
        in_specs=[pl.BlockSpec((tm, d), row), pl.BlockSpec((d, d), fix), pl.BlockSpec((tm, d), row),
                  pl.BlockSpec((1, d), fix), pl.BlockSpec((1, d), fix)],
        out_specs=[pl.BlockSpec((tm, d), row), pl.BlockSpec((tm, d), row)],
        out_shape=[jax.ShapeDtypeStruct((m, d), F32), jax.ShapeDtypeStruct((m, d), BF16)],
        compiler_params=_cparams(1),
        name="proj_norm",
    )(y, wo, x, g_post.reshape(1, d), g_pre.reshape(1, d))


def _mlp_kernel(h_ref, wu_ref, wd_ref, x_ref, g_ref, o_ref, acc_ref):
    k = pl.program_id(1)

    @pl.when(k == 0)
    def _():
        acc_ref[...] = jnp.zeros_like(acc_ref)

    u = jnp.maximum(_dot(h_ref[...], wu_ref[...]), 0.0)
    acc_ref[...] += _dot((u * u).astype(BF16), wd_ref[...])

    @pl.when(k == pl.num_programs(1) - 1)
    def _():
        a = acc_ref[...]
        o_ref[...] = x_ref[...] + a * lax.rsqrt(jnp.mean(a * a, axis=-1, keepdims=True) + RMS_EPS) * g_ref[...]


def mlp(h2, w_up, w_down, x1, g_post, tm, tk=512):
    m, d = x1.shape
    ff = w_up.shape[1]
    return pl.pallas_call(
        _mlp_kernel,
        grid=(m // tm, ff // tk),
        in_specs=[pl.BlockSpec((tm, d), lambda i, k: (i, 0)),
                  pl.BlockSpec((d, tk), lambda i, k: (0, k)),
                  pl.BlockSpec((tk, d), lambda i, k: (k, 0)),
                  pl.BlockSpec((tm, d), lambda i, k: (i, 0)),
                  pl.BlockSpec((1, d), lambda i, k: (0, 0))],
        out_specs=pl.BlockSpec((tm, d), lambda i, k: (i, 0)),
        out_shape=jax.ShapeDtypeStruct((m, d), F32),
        scratch_shapes=[pltpu.VMEM((tm, d), F32)],
        compiler_params=_cparams(2),
        name="mlp",
    )(h2, w_up, w_down, x1, g_post.reshape(1, d))


QB = LANES
KC = 512
SCORE_KC = 256
AQ = 256


def _dsa_mask_prompt_kernel(qi_ref, sm_all_ref, sm_blk_ref, mask_ref, kia_ref, kib_ref, keys_ref, mstar_ref, qq_ref,
                            *, seq, top_k):
    i = pl.program_id(1)
    n_chunks = (i * QB) // KC + 1

    @pl.when(i == 0)
    def _():
        sm = sm_all_ref[...]
        lane = lax.broadcasted_iota(I32, sm.shape, 1)
        kia_ref[...] = jnp.where(lane < IDX_DIM, sm, 0.0).astype(BF16)
        kib_ref[...] = jnp.where(lane >= IDX_DIM, pltpu.roll(sm, IDX_DIM, axis=1), 0.0).astype(BF16)

    w_t = sm_blk_ref[...].T * IDX_W_SCALE
    qi = qi_ref[...].astype(BF16)
    for g in range(N_IDX_HEADS // 4):
        qq_ref[g] = jnp.concatenate([qi[:, (2 * g) * LANES:(2 * g + 1) * LANES],
                                     qi[:, (2 * g + 1) * LANES:(2 * g + 2) * LANES]], axis=0)
    q_pos_s = i * QB + lax.broadcasted_iota(I32, (SCORE_KC, QB), 1)
    row_s = lax.broadcasted_iota(I32, (SCORE_KC, QB), 0)

    def score_chunk(c, carry):
        r0 = pl.multiple_of(c * SCORE_KC, SCORE_KC)
        ka = kia_ref[pl.ds(r0, SCORE_KC), :]
        kb = kib_ref[pl.ds(r0, SCORE_KC), :]
        acc = jnp.zeros((SCORE_KC, QB), F32)
        for g in range(N_IDX_HEADS // 4):
            qq = qq_ref[g]
            for odd, k_half in ((0, ka), (1, kb)):
                s = jnp.maximum(_dot_nt(k_half, qq), 0.0)
                for j in range(2):
                    head = 4 * g + 2 * j + odd
                    acc = acc + s[:, j * LANES:(j + 1) * LANES] * w_t[IDX_DIM + head:IDX_DIM + head + 1, :]
        sc = jnp.where(c * SCORE_KC + row_s <= q_pos_s, acc, NEG_INF)
        keys_ref[pl.ds(r0, SCORE_KC), :] = _float_key(sc)
        return carry

    lax.fori_loop(0, n_chunks * (KC // SCORE_KC), score_chunk, 0)

    n_tail = seq - n_chunks * KC
    q_pos = i * QB + lax.broadcasted_iota(I32, (KC, QB), 1)
    row = lax.broadcasted_iota(I32, (KC, QB), 0)

    def count(pred_fn):
        def body(c, acc):
            k = keys_ref[pl.ds(pl.multiple_of(c * KC, KC), KC), :]
            ones = jnp.where(pred_fn(k, c), 1, 0)
            return acc + jnp.sum(ones.reshape(KC // SUBLANES, SUBLANES, QB), axis=0)

        acc = lax.fori_loop(0, n_chunks, body, jnp.zeros((SUBLANES, QB), I32))
        return jnp.sum(acc, axis=0, keepdims=True)

    def count_ge(cand):
        return count(lambda k, c: k >= cand) + jnp.where(KEY_NEG_INF >= cand, n_tail, 0)

    base = jnp.where(count_ge(jnp.zeros((1, QB), I32)) >= top_k, 0, INT_MIN).astype(I32)

    def bit_body(t, base):
        cand = base | lax.shift_left(jnp.int32(1), 30 - t)
        return jnp.where(count_ge(cand) >= top_k, cand, base)

    v = lax.fori_loop(0, 31, bit_body, base)

    n_gt = count(lambda k, c: k > v) + jnp.where(KEY_NEG_INF > v, n_tail, 0)
    need = top_k - n_gt
    n_eq = count(lambda k, c: k == v)
    mstar_ref[...] = jnp.full((SUBLANES, QB), seq, I32)

    @pl.when(jnp.max(jnp.where(n_eq > need, 1, 0)) > 0)
    def _():
        def idx_bit(t, m):
            cand = m | lax.shift_left(jnp.int32(1), (seq - 1).bit_length() - 1 - t)
            below = count(lambda k, c: (k == v) & (c * KC + row < cand))
            return jnp.where(below < need, cand, m)

        m = lax.fori_loop(0, (seq - 1).bit_length(), idx_bit, jnp.zeros((1, QB), I32))
        mstar_ref[...] = jnp.broadcast_to(m, (SUBLANES, QB))

    mstar = mstar_ref[0:1, :]

    def emit(c, carry):
        k = keys_ref[pl.ds(pl.multiple_of(c * KC, KC), KC), :]
        k_pos = c * KC + row
        sel = (k > v) | ((k == v) & (k_pos <= mstar))
        m_t = jnp.where(sel & (k_pos <= q_pos), 1.0, 0.0)
        mask_ref[c] = m_t.T.astype(mask_ref.dtype)
        return carry

    lax.fori_loop(0, n_chunks, emit, 0)

    def clear(c, carry):
        mask_ref[c] = jnp.zeros((QB, KC), mask_ref.dtype)
        return carry

    lax.fori_loop(n_chunks, seq // KC, clear, 0)


def dsa_mask_prompt(proj, batch, seq, c_small):
    assert seq % KC == 0 and KC % AQ == 0 and AQ % QB == 0
    nq = seq // QB
    per = AQ // QB
    top_k = min(TOP_K_MAX, seq // 4)
    return pl.pallas_call(
        functools.partial(_dsa_mask_prompt_kernel, seq=seq, top_k=top_k),
        grid=(batch, nq),
        in_specs=[pl.BlockSpec((QB, N_IDX_HEADS * IDX_DIM), lambda b, i: (b * nq + i, C_QI // (N_IDX_HEADS * IDX_DIM))),
                  pl.BlockSpec((seq, LANES), lambda b, i: (b, c_small // LANES)),
                  pl.BlockSpec((QB, LANES), lambda b, i: (b * nq + i, c_small // LANES))],
        out_specs=pl.BlockSpec((None, seq // KC, None, QB, KC), lambda b, i: ((b * nq + i) // per, 0, i % per, 0, 0)),
        out_shape=jax.ShapeDtypeStruct((batch * seq // AQ, seq // KC, per, QB, KC), BF16),
        scratch_shapes=[pltpu.VMEM((seq, LANES), BF16), pltpu.VMEM((seq, LANES), BF16),
                        pltpu.VMEM((seq, QB), I32), pltpu.VMEM((SUBLANES, QB), I32),
                        pltpu.VMEM((N_IDX_HEADS // 4, 2 * QB, LANES), BF16)],
        compiler_params=_cparams(2),
        name="dsa_mask_prompt",
    )(proj, proj, proj)


def _bucket_thresholds():
    d = np.arange(4 * REL_MAX_DISTANCE)
    b = _rel_bucket_np(d)
    assert np.all(np.diff(b) >= 0) and b[-1] == N_REL_BUCKETS - 1
    return [int(np.argmax(b > j)) for j in range(N_REL_BUCKETS - 1)]


def _bias_of_distance(rb_ref, h, d, thresholds):
    acc = jnp.full(d.shape, rb_ref[N_REL_BUCKETS - 1, h], F32)
    for j in range(N_REL_BUCKETS - 2, -1, -1):
        acc = jnp.where(d < thresholds[j], rb_ref[j, h], acc)
    return acc


def _lane_fold(x, op):
    out = x[:, :LANES]
    for j in range(1, x.shape[1] // LANES):
        out = op(out, x[:, j * LANES:(j + 1) * LANES])
    return out


def _dsa_attn_prompt_kernel(rb_ref, q_ref, k_ref, v_ref, mask_ref, o_ref, tab_ref, lg_ref, kb_ref, vb_ref, acc_ref,
                            l_ref, *, thresholds):
    h = pl.program_id(1)
    i = pl.program_id(2)
    per = KC // AQ
    part = i % per
    c_diag = i // per

    @pl.when(i == 0)
    def _():
        kb_ref[...] = k_ref[...].astype(BF16)
        vb_ref[...] = v_ref[...].astype(BF16)
        row = lax.broadcasted_iota(I32, (AQ, KC), 0)
        col = lax.broadcasted_iota(I32, (AQ, KC), 1)
        for p in range(per):
            for back in range(2):
                tab_ref[3 * p + back] = _bias_of_distance(rb_ref, h, p * AQ + back * KC + row - col, thresholds)
            tab_ref[3 * p + 2] = jnp.full((AQ, KC), rb_ref[N_REL_BUCKETS - 1, h], F32)

    q = q_ref[...].astype(BF16)

    def logits_chunk(c, mx):
        kc = kb_ref[pl.ds(pl.multiple_of(c * KC, KC), KC), :]
        s = _dot_nt(q, kc) * ATTN_SCALE + tab_ref[3 * part + jnp.minimum(c_diag - c, 2)]
        s = jnp.where(mask_ref[c].reshape(AQ, KC).astype(F32) > 0.5, s, NEG_INF)
        lg_ref[c] = s
        return jnp.maximum(mx, _lane_fold(s, jnp.maximum))

    mx = lax.fori_loop(0, c_diag + 1, logits_chunk, jnp.full((AQ, LANES), NEG_INF, F32))
    m = jnp.max(mx, axis=1, keepdims=True)
    acc_ref[...] = jnp.zeros_like(acc_ref)
    l_ref[...] = jnp.zeros_like(l_ref)

    def pv_chunk(c, carry):
        p = jnp.exp(lg_ref[c] - m)
        l_ref[...] += _lane_fold(p, jnp.add)
        acc_ref[...] += _dot(p.astype(BF16), vb_ref[pl.ds(pl.multiple_of(c * KC, KC), KC), :])
        return carry

    lax.fori_loop(0, c_diag + 1, pv_chunk, 0)
    o_ref[...] = (acc_ref[...] / jnp.sum(l_ref[...], axis=1, keepdims=True)).astype(o_ref.dtype)


def dsa_attn_prompt(proj, mask, rel_bias, batch, seq):
    nq = seq // AQ
    nk = seq // KC
    hd = HEAD_DIM
    thresholds = _bucket_thresholds()
    assert thresholds[-1] <= KC + 1
    return pl.pallas_call(
        functools.partial(_dsa_attn_prompt_kernel, thresholds=thresholds),
        grid=(batch, N_HEADS, nq),
        in_specs=[pl.BlockSpec(memory_space=pltpu.SMEM),
                  pl.BlockSpec((AQ, hd), lambda b, h, i: (b * nq + i, C_QD // hd + h)),
                  pl.BlockSpec((seq, hd), lambda b, h, i: (b, C_KD // hd + h)),
                  pl.BlockSpec((seq, hd), lambda b, h, i: (b, C_VD // hd + h)),
                  pl.BlockSpec((None, nk, AQ // QB, QB, KC), lambda b, h, i: (b * nq + i, 0, 0, 0, 0))],
        out_specs=pl.BlockSpec((AQ, hd), lambda b, h, i: (b * nq + i, h)),
        out_shape=jax.ShapeDtypeStruct((batch * seq, D_ATT), BF16),
        scratch_shapes=[pltpu.VMEM((3 * (KC // AQ), AQ, KC), F32), pltpu.VMEM((nk, AQ, KC), F32),
                        pltpu.VMEM((seq, hd), BF16), pltpu.VMEM((seq, hd), BF16),
                        pltpu.VMEM((AQ, hd), F32), pltpu.VMEM((AQ, LANES), F32)],
        compiler_params=_cparams(3),
        name="dsa_attn_prompt",
    )(rel_bias, proj, proj, proj, mask)


SB_T = 256
EXP_ZERO_BELOW = -104.0


def _suffix_sum_matrix(n):
    s = np.arange(n)
    tri = (s[:, None] > s[None, :]).astype(np.float32)
    return np.concatenate([tri, np.ones((n, n), np.float32)], axis=1)


def _sb_prompt_kernel(q_ref, k_ref, v_ref, r_ref, o_ref, kb_ref, vb_ref, acc_ref, run_ref):
    i = pl.program_id(2)

    @pl.when(i == 0)
    def _():
        kb_ref[...] = k_ref[...].astype(BF16)
        vb_ref[...] = v_ref[...].astype(BF16)

    q = q_ref[...].astype(BF16)
    acc_ref[...] = jnp.zeros_like(acc_ref)
    run_ref[...] = jnp.zeros_like(run_ref)
    row = lax.broadcasted_iota(I32, (SB_T, SB_T), 0)
    col = lax.broadcasted_iota(I32, (SB_T, SB_T), 1)

    def chunk(carry):
        step, _ = carry
        c = i - step
        r0 = pl.multiple_of(c * SB_T, SB_T)
        z = _dot_nt(q, kb_ref[pl.ds(r0, SB_T), :]) * ATTN_SCALE
        sp = _softplus(z)
        before = col < row + step * SB_T
        log_keep = jnp.where(before, -sp, 0.0)
        cs = _split_dot(log_keep, r_ref[...], 2)
        later = cs[:, :SB_T] + run_ref[...]
        a = jnp.where(before, jnp.exp(z - sp + later), 0.0)
        acc_ref[...] += _dot(a.astype(BF16), vb_ref[pl.ds(r0, SB_T), :])
        run = run_ref[...] + cs[:, SB_T:]
        run_ref[...] = run
        live = jnp.max(run[:, :LANES]) >= EXP_ZERO_BELOW
        return step + 1, live.astype(I32)

    lax.while_loop(lambda carry: (carry[0] <= i) & (carry[1] > 0), chunk, (jnp.int32(0), jnp.int32(1)))
    o_ref[...] = acc_ref[...].astype(o_ref.dtype)


def sb_prompt(proj, batch, seq):
    nq = seq // SB_T
    hd = HEAD_DIM
    r = jnp.asarray(_suffix_sum_matrix(SB_T), BF16)
    return pl.pallas_call(
        _sb_prompt_kernel,
        grid=(batch, N_HEADS, nq),
        in_specs=[pl.BlockSpec((SB_T, hd), lambda b, h, i: (b * nq + i, C_QS // hd + h)),
                  pl.BlockSpec((seq, hd), lambda b, h, i: (b, C_KS // hd + h)),
                  pl.BlockSpec((seq, hd), lambda b, h, i: (b, C_VS // hd + h)),
                  pl.BlockSpec((SB_T, 2 * SB_T), lambda b, h, i: (0, 0))],
        out_specs=pl.BlockSpec((SB_T, hd), lambda b, h, i: (b * nq + i, h)),
        out_shape=jax.ShapeDtypeStruct((batch * seq, D_ATT), BF16),
        scratch_shapes=[pltpu.VMEM((seq, hd), BF16), pltpu.VMEM((seq, hd), BF16),
                        pltpu.VMEM((SB_T, hd), F32), pltpu.VMEM((SB_T, SB_T), F32)],
        compiler_params=_cparams(3),
        name="sb_prompt",
    )(proj, proj, proj, r)


TOK_PAD = SUBLANES


def _pad_rows(x, rows):
    return jnp.concatenate([x, jnp.zeros((rows - x.shape[0], x.shape[1]), x.dtype)], axis=0)


def _head_rows(page_ref, h, page_size):
    return page_ref[pl.ds(h, page_size, stride=N_HEADS), :].astype(BF16)


def _page_specs(n_pages, rows, cols):
    return [pl.BlockSpec((None, rows, cols), functools.partial(lambda b, pt, p: (pt[b, p], 0, 0), p=p))
            for p in range(n_pages)]


def _sb_sample_kernel(pt_ref, q_ref, kn_ref, vn_ref, r_ref, k_hbm, v_hbm, o_ref, kbuf, vbuf, sems, run_ref, acc_ref,
                      *, n_pages, page_size, n_new, n_pre):
    b = pl.program_id(0)
    grp = N_HEADS * TOK_PAD
    spare = 2 * n_pre

    def page_copies(seq, page, slot):
        phys = pt_ref[seq, page]
        return (pltpu.make_async_copy(k_hbm.at[phys], kbuf.at[slot], sems.at[0, slot]),
                pltpu.make_async_copy(v_hbm.at[phys], vbuf.at[slot], sems.at[1, slot]))

    def prefetch(seq, action):
        for j in range(n_pre):
            for cp in page_copies(seq, n_pages - 1 - j, (seq % 2) * n_pre + j):
                action(cp)

    @pl.when(b == 0)
    def _():
        prefetch(b, lambda cp: cp.start())

    @pl.when(b + 1 < pl.num_programs(0))
    def _():
        prefetch(b + 1, lambda cp: cp.start())

    q = [q_ref[:, h * HEAD_DIM:(h + 1) * HEAD_DIM].astype(BF16) for h in range(N_HEADS)]
    row = lax.broadcasted_iota(I32, (grp, page_size), 0) % TOK_PAD
    lane = lax.broadcasted_iota(I32, (grp, page_size), 1)
    run_ref[...] = jnp.zeros_like(run_ref)
    acc_ref[...] = jnp.zeros_like(acc_ref)

    def process(k_h, v_h, before):
        z = jnp.concatenate([_dot_nt(q[h], k_h[h]) for h in range(N_HEADS)], axis=0) * ATTN_SCALE
        sp = _softplus(z)
        log_keep = -sp if before is None else jnp.where(before, -sp, 0.0)
        cs = _split_dot(log_keep, r_ref[...], 3)
        run = run_ref[...]
        e = jnp.exp(z - sp + cs[:, :page_size] + run)
        a = e if before is None else jnp.where(before, e, 0.0)
        run = run + cs[:, page_size:]
        run_ref[...] = run
        for h in range(N_HEADS):
            rows = slice(h * TOK_PAD, (h + 1) * TOK_PAD)
            acc_ref[rows, :] += _dot(a[rows, :].astype(BF16), v_h[h])
        return (jnp.max(jnp.where(row < n_new, run, NEG_INF)) >= EXP_ZERO_BELOW).astype(I32)

    def process_slot(slot):
        k_h = [kbuf[slot, pl.ds(h, page_size, stride=N_HEADS), :].astype(BF16) for h in range(N_HEADS)]
        v_h = [vbuf[slot, pl.ds(h, page_size, stride=N_HEADS), :].astype(BF16) for h in range(N_HEADS)]
        return process(k_h, v_h, None)

    process([_pad_rows(kn_ref[:, h * HEAD_DIM:(h + 1) * HEAD_DIM], page_size).astype(BF16) for h in range(N_HEADS)],
            [_pad_rows(vn_ref[:, h * HEAD_DIM:(h + 1) * HEAD_DIM], page_size).astype(BF16) for h in range(N_HEADS)],
            lane < row)
    prefetch(b, lambda cp: cp.wait())
    live = jnp.int32(1)
    for j in range(n_pre):
        live = lax.cond(live > 0, functools.partial(process_slot, (b % 2) * n_pre + j), lambda: jnp.int32(0))

    def fetch_and_process(carry):
        page, _ = carry
        for cp in page_copies(b, page, spare):
            cp.start()
        for cp in page_copies(b, page, spare):
            cp.wait()
        return page - 1, process_slot(spare)

    lax.while_loop(lambda carry: (carry[0] >= 0) & (carry[1] > 0), fetch_and_process,
                   (jnp.int32(n_pages - n_pre - 1), live))
    for h in range(N_HEADS):
        o_ref[:, h * HEAD_DIM:(h + 1) * HEAD_DIM] = acc_ref[h * TOK_PAD:(h + 1) * TOK_PAD, :]


def sb_sample(proj8, k_pool, v_pool, page_table, n_new):
    n_seq = proj8.shape[0]
    n_pages = page_table.shape[1]
    page_size = k_pool.shape[1] // N_HEADS
    assert page_size == LANES
    n_pre = min(2, n_pages)
    r = jnp.asarray(_suffix_sum_matrix(page_size), BF16)
    tok = lambda col: pl.BlockSpec((None, TOK_PAD, D_ATT), lambda b, pt: (b, 0, col // D_ATT))
    page_buf = pltpu.VMEM((2 * n_pre + 1, page_size * N_HEADS, HEAD_DIM), F32)
    grid_spec = pltpu.PrefetchScalarGridSpec(
        num_scalar_prefetch=1,
        grid=(n_seq,),
        in_specs=[tok(C_QS), tok(C_KS), tok(C_VS), pl.BlockSpec((page_size, 2 * page_size), lambda b, pt: (0, 0)),
                  pl.BlockSpec(memory_space=pl.ANY), pl.BlockSpec(memory_space=pl.ANY)],
        out_specs=pl.BlockSpec((None, TOK_PAD, D_ATT), lambda b, pt: (b, 0, 0)),
        scratch_shapes=[page_buf, page_buf, pltpu.SemaphoreType.DMA((2, 2 * n_pre + 1)),
                        pltpu.VMEM((N_HEADS * TOK_PAD, page_size), F32), pltpu.VMEM((N_HEADS * TOK_PAD, HEAD_DIM), F32)],
    )
    return pl.pallas_call(
        functools.partial(_sb_sample_kernel, n_pages=n_pages, page_size=page_size, n_new=n_new, n_pre=n_pre),
        grid_spec=grid_spec,
        out_shape=jax.ShapeDtypeStruct((n_seq, TOK_PAD, D_ATT), F32),
        compiler_params=_cparams(1),
        name="sb_sample",
    )(page_table, proj8, proj8, proj8, r, k_pool, v_pool)


def _dsa_scores_sample_kernel(pt_ref, qi_ref, w_ref, kin_ref, *rest, n_pages, page_size, n_new):
    ki_pages = rest[:n_pages]
    o_ref = rest[n_pages]
    qi = qi_ref[...].astype(BF16)
    w = w_ref[...] * IDX_W_SCALE
    o_ref[...] = jnp.zeros_like(o_ref)
    for g in range(n_pages + 1):
        ki = (kin_ref if g == n_pages else ki_pages[g])[...].astype(BF16)
        s = jnp.maximum(_dot_nt(qi, ki), 0.0) * w
        s = jnp.sum(s.reshape(n_new, N_IDX_HEADS, page_size), axis=1)
        o_ref[0:n_new, g * page_size:(g + 1) * page_size] = s


def dsa_scores_sample(qi3, w_rep, ki_new_pad, ki_pool, page_table):
    n_seq = qi3.shape[0]
    n_new = qi3.shape[1] // N_IDX_HEADS
    n_pages = page_table.shape[1]
    page_size = ki_pool.shape[1]
    assert page_size == LANES
    seq_blk = lambda r, c: pl.BlockSpec((None, r, c), lambda b, pt: (b, 0, 0))
    grid_spec = pltpu.PrefetchScalarGridSpec(
        num_scalar_prefetch=1,
        grid=(n_seq,),
        in_specs=[seq_blk(n_new * N_IDX_HEADS, IDX_DIM), seq_blk(n_new * N_IDX_HEADS, LANES),
                  seq_blk(page_size, IDX_DIM)] + _page_specs(n_pages, page_size, IDX_DIM),
        out_specs=seq_blk(TOK_PAD, (n_pages + 1) * page_size),
    )
    return pl.pallas_call(
        functools.partial(_dsa_scores_sample_kernel, n_pages=n_pages, page_size=page_size, n_new=n_new),
        grid_spec=grid_spec,
        out_shape=jax.ShapeDtypeStruct((n_seq, TOK_PAD, (n_pages + 1) * page_size), F32),
        compiler_params=_cparams(1),
        name="dsa_scores_sample",
    )(page_table, qi3, w_rep, ki_new_pad, *([ki_pool] * n_pages))


MASK_ROWS = 128


def _dsa_mask_sample_kernel(s_ref, ones_ref, m_ref, *, n_past, n_new, top_k):
    n_chunks = s_ref.shape[1] // LANES
    shape = (MASK_ROWS, LANES)
    tok = lax.broadcasted_iota(I32, shape, 0) % TOK_PAD
    lane = lax.broadcasted_iota(I32, shape, 1)
    ones = ones_ref[...]
    keys, pos, exists = [], [], []
    for c in range(n_chunks):
        s = s_ref[:, c * LANES:(c + 1) * LANES]
        k_pos = c * LANES + lane
        s = jnp.where(k_pos <= n_past + tok, s, NEG_INF)
        keys.append(_float_key(s))
        pos.append(k_pos)
        exists.append(None if (c + 1) * LANES <= n_past + n_new else k_pos < n_past + n_new)

    def count(pred_fn):
        part = jnp.zeros(shape, F32)
        for c in range(n_chunks):
            p = pred_fn(keys[c], pos[c])
            if exists[c] is not None:
                p = p & exists[c]
            part = part + jnp.where(p, 1.0, 0.0)
        return _dot(part.astype(BF16), ones)

    base = jnp.where(count(lambda k, p: k >= 0) >= top_k, 0, INT_MIN).astype(I32)

    def bit_body(t, base):
        cand = base | lax.shift_left(jnp.int32(1), 30 - t)
        return jnp.where(count(lambda k, p: k >= cand) >= top_k, cand, base)

    v = lax.fori_loop(0, 31, bit_body, base)
    need = top_k - count(lambda k, p: k > v)

    n_bits = (n_chunks * LANES - 1).bit_length()

    def idx_bit(t, m):
        cand = m | lax.shift_left(jnp.int32(1), n_bits - 1 - t)
        below = count(lambda k, p: (k == v) & (p < cand))
        return jnp.where(below < need, cand, m)

    mstar = lax.fori_loop(0, n_bits, idx_bit, jnp.zeros(shape, I32))
    for c in range(n_chunks):
        sel = (keys[c] > v) | ((keys[c] == v) & (pos[c] <= mstar))
        sel = sel & (pos[c] <= n_past + tok) & (tok < n_new)
        if exists[c] is not None:
            sel = sel & exists[c]
        m_ref[:, c * LANES:(c + 1) * LANES] = jnp.where(sel, 1.0, 0.0)


def dsa_mask_sample(scores2, n_past, n_new):
    rows, width = scores2.shape
    top_k = min(TOP_K_MAX, (n_past + n_new) // 4)
    return pl.pallas_call(
        functools.partial(_dsa_mask_sample_kernel, n_past=n_past, n_new=n_new, top_k=top_k),
        grid=(rows // MASK_ROWS,),
        in_specs=[pl.BlockSpec((MASK_ROWS, width), lambda i: (i, 0)), pl.BlockSpec((LANES, LANES), lambda i: (0, 0))],
        out_specs=pl.BlockSpec((MASK_ROWS, width), lambda i: (i, 0)),
        out_shape=jax.ShapeDtypeStruct((rows, width), F32),
        compiler_params=_cparams(1),
        name="dsa_mask_sample",
    )(scores2, jnp.ones((LANES, LANES), BF16))


def _sample_bucket_tables(n_new, page_size):
    j = (np.arange(N_HEADS * TOK_PAD) % TOK_PAD)[:, None]
    u = np.arange(page_size)[None, :]
    assert _rel_bucket_np(np.array([page_size + 1]))[0] == N_REL_BUCKETS - 1
    return np.stack([_rel_bucket_np(page_size + j - u), _rel_bucket_np(j - u)]).astype(np.int32)


def _dsa_attn_sample_kernel(pt_ref, rb_ref, q_ref, kn_ref, vn_ref, m_ref, bkt_ref, *rest, n_pages, page_size, n_new):
    k_pages = rest[:n_pages]
    v_pages = rest[n_pages:2 * n_pages]
    o_ref = rest[2 * n_pages]
    tab_ref = rest[2 * n_pages + 1]
    lg_ref = rest[2 * n_pages + 2]
    grp = N_HEADS * TOK_PAD

    @pl.when(pl.program_id(0) == 0)
    def _():
        for h in range(N_HEADS):
            rows = slice(h * TOK_PAD, (h + 1) * TOK_PAD)
            for t in range(2):
                bk = bkt_ref[t, rows, :]
                acc = jnp.zeros(bk.shape, F32)
                for b in range(N_REL_BUCKETS):
                    acc = jnp.where(bk == b, rb_ref[b, h], acc)
                tab_ref[t, rows, :] = acc
            tab_ref[2, rows, :] = jnp.full((TOK_PAD, page_size), rb_ref[N_REL_BUCKETS - 1, h], F32)

    q = [q_ref[:, h * HEAD_DIM:(h + 1) * HEAD_DIM].astype(BF16) for h in range(N_HEADS)]
    mx = jnp.full((grp, page_size), NEG_INF, F32)
    for g in range(n_pages + 1):
        if g == n_pages:
            k_h = [_pad_rows(kn_ref[:, h * HEAD_DIM:(h + 1) * HEAD_DIM], page_size).astype(BF16) for h in range(N_HEADS)]
            bias = tab_ref[1]
        else:
            k_h = [_head_rows(k_pages[g], h, page_size) for h in range(N_HEADS)]
            bias = tab_ref[0] if g == n_pages - 1 else tab_ref[2]
        s = jnp.concatenate([_dot_nt(q[h], k_h[h]) for h in range(N_HEADS)], axis=0) * ATTN_SCALE + bias
        sel = jnp.tile(m_ref[:, g * page_size:(g + 1) * page_size], (N_HEADS, 1))
        s = jnp.where(sel > 0.5, s, NEG_INF)
        lg_ref[g] = s
        mx = jnp.maximum(mx, s)
    m = jnp.max(mx, axis=1, keepdims=True)
    l = jnp.zeros((grp, page_size), F32)
    acc = [jnp.zeros((TOK_PAD, HEAD_DIM), F32) for _ in range(N_HEADS)]
    for g in range(n_pages + 1):
        p = jnp.exp(lg_ref[g] - m)
        l = l + p
        for h in range(N_HEADS):
            if g == n_pages:
                v_h = _pad_rows(vn_ref[:, h * HEAD_DIM:(h + 1) * HEAD_DIM], page_size).astype(BF16)
            else:
                v_h = _head_rows(v_pages[g], h, page_size)
            acc[h] = acc[h] + _dot(p[h * TOK_PAD:(h + 1) * TOK_PAD, :].astype(BF16), v_h)
    inv = 1.0 / jnp.sum(l, axis=1, keepdims=True)
    for h in range(N_HEADS):
        o_ref[:, h * HEAD_DIM:(h + 1) * HEAD_DIM] = acc[h] * inv[h * TOK_PAD:(h + 1) * TOK_PAD, :]


def dsa_attn_sample(proj8, mask3, rel_bias, k_pool, v_pool, page_table, n_new):
    n_seq = proj8.shape[0]
    n_pages = page_table.shape[1]
    page_size = k_pool.shape[1] // N_HEADS
    assert page_size == LANES
    grp = N_HEADS * TOK_PAD
    width = (n_pages + 1) * page_size
    tok = lambda col: pl.BlockSpec((None, TOK_PAD, D_ATT), lambda b, pt: (b, 0, col // D_ATT))
    grid_spec = pltpu.PrefetchScalarGridSpec(
        num_scalar_prefetch=1,
        grid=(n_seq,),
        in_specs=[pl.BlockSpec(memory_space=pltpu.SMEM), tok(C_QD), tok(C_KD), tok(C_VD),
                  pl.BlockSpec((None, TOK_PAD, width), lambda b, pt: (b, 0, 0)),
                  pl.BlockSpec((2, grp, page_size), lambda b, pt: (0, 0, 0))]
                 + _page_specs(n_pages, page_size * N_HEADS, HEAD_DIM)
                 + _page_specs(n_pages, page_size * N_HEADS, HEAD_DIM),
        out_specs=pl.BlockSpec((None, TOK_PAD, D_ATT), lambda b, pt: (b, 0, 0)),
        scratch_shapes=[pltpu.VMEM((3, grp, page_size), F32), pltpu.VMEM((n_pages + 1, grp, page_size), F32)],
    )
    return pl.pallas_call(
        functools.partial(_dsa_attn_sample_kernel, n_pages=n_pages, page_size=page_size, n_new=n_new),
        grid_spec=grid_spec,
        out_shape=jax.ShapeDtypeStruct((n_seq, TOK_PAD, D_ATT), F32),
        compiler_params=_cparams(1, vmem_mb=56),
        name="dsa_attn_sample",
    )(page_table, rel_bias, proj8, proj8, proj8, mask3, jnp.asarray(_sample_bucket_tables(n_new, page_size)),
      *([k_pool] * n_pages), *([v_pool] * n_pages))


def _rearranged_w_in(w_in, d_model):
    c_ki = 4 * D_ATT
    c_after = c_ki + IDX_DIM + N_IDX_HEADS
    wide = jnp.concatenate([w_in[:, :c_ki], w_in[:, c_after:]], axis=1)
    c_small = wide.shape[1]
    narrow = w_in[:, c_ki:c_after]
    pad = PROJ_TN - narrow.shape[1]
    w = jnp.concatenate([wide, narrow, jnp.zeros((w_in.shape[0], pad), w_in.dtype)], axis=1)
    return w.astype(BF16), c_small


def _tail(x, proj, o_a, o_b, weights, gains):
    w1, w2, wo, wu, wd = weights
    g_attn_post, g_mlp_pre, g_mlp_post = gains
    tm = min(512, x.shape[0])
    y = merge_branches(o_a, o_b, w1, w2, proj, tm)
    x1, h2 = proj_norm(y, wo, x, g_attn_post, g_mlp_pre)
    return mlp(h2, wu, wd, x1, g_mlp_post, tm)


def kernel(x_prompt, x_sample, cache_k_dsa, cache_v_dsa, cache_k_idx, cache_k_sb, cache_v_sb, page_table, rel_bias,
           w_in, w_out_dsa, w_out_sb, w_o, w_up, w_down, g_attn_pre, g_attn_post, g_mlp_pre, g_mlp_post):
    batch, seq, d_model = x_prompt.shape
    n_seq, n_new, _ = x_sample.shape
    depth = w_in.shape[0]
    n_pool, page_size = cache_k_idx.shape[1], cache_k_idx.shape[2]
    n_past = page_table.shape[1] * page_size

    xp = x_prompt.reshape(batch * seq, d_model)
    xs = x_sample.reshape(n_seq * n_new, d_model)
    rows_p, rows_s = [], []
    for l in range(depth):
        w_in_r, c_small = _rearranged_w_in(w_in[l], d_model)
        weights = tuple(w[l].astype(BF16) for w in (w_out_dsa, w_out_sb, w_o, w_up, w_down))
        gains = (g_attn_post[l], g_mlp_pre[l], g_mlp_post[l])

        hp = rms_cast(xp, g_attn_pre[l])
        proj = matmul(hp, w_in_r, tm=min(1024, hp.shape[0]), tn=PROJ_TN)
        mask = dsa_mask_prompt(proj, batch, seq, c_small)
        o_a = dsa_attn_prompt(proj, mask, rel_bias, batch, seq)
        o_b = sb_prompt(proj, batch, seq)
        xp_new = _tail(xp, proj, o_a, o_b, weights, gains)
        heads = lambda c: proj[:, c:c + D_ATT].reshape(batch, seq, N_HEADS, HEAD_DIM)
        rows_p.append((heads(C_KD), heads(C_VD), proj[:, c_small:c_small + IDX_DIM].reshape(batch, seq, IDX_DIM),
                       heads(C_KS), heads(C_VS)))
        xp = xp_new

        hs = rms_cast(xs, g_attn_pre[l])
        proj_s = matmul(hs, w_in_r, tm=min(512, hs.shape[0]), tn=PROJ_TN)
        proj8 = jnp.pad(proj_s[:, :C_GA].reshape(n_seq, n_new, C_GA), ((0, 0), (0, TOK_PAD - n_new), (0, 0)))
        qi3 = proj_s[:, C_QI:C_QI + N_IDX_HEADS * IDX_DIM].reshape(n_seq, n_new * N_IDX_HEADS, IDX_DIM)
        w_rep = jnp.broadcast_to(
            proj_s[:, c_small + IDX_DIM:c_small + IDX_DIM + N_IDX_HEADS].reshape(n_seq, n_new * N_IDX_HEADS, 1),
            (n_seq, n_new * N_IDX_HEADS, LANES))
        ki_new = proj_s[:, c_small:c_small + IDX_DIM].reshape(n_seq, n_new, IDX_DIM)
        ki_new_pad = jnp.pad(ki_new, ((0, 0), (0, page_size - n_new), (0, 0)))
        pool2 = lambda c: c[l].reshape(n_pool, page_size * N_HEADS, HEAD_DIM)
        scores = dsa_scores_sample(qi3, w_rep, ki_new_pad, cache_k_idx[l], page_table)
        mask_s = dsa_mask_sample(scores.reshape(n_seq * TOK_PAD, scores.shape[2]), n_past, n_new)
        o_a = dsa_attn_sample(proj8, mask_s.reshape(scores.shape), rel_bias, pool2(cache_k_dsa), pool2(cache_v_dsa),
                              page_table, n_new)
        o_b = sb_sample(proj8, pool2(cache_k_sb), pool2(cache_v_sb), page_table, n_new)
        unpad = lambda o: o[:, :n_new, :].reshape(n_seq * n_new, D_ATT)
        xs_new = _tail(xs, proj_s, unpad(o_a), unpad(o_b), weights, gains)
        heads_s = lambda c: proj_s[:, c:c + D_ATT].reshape(n_seq, n_new, N_HEADS, HEAD_DIM)
        rows_s.append((heads_s(C_KD), heads_s(C_VD), ki_new, heads_s(C_KS), heads_s(C_VS)))
        xs = xs_new

    outs_p = [jnp.stack(r, axis=0) for r in zip(*rows_p)]
    outs_s = [jnp.stack(r, axis=0) for r in zip(*rows_s)]
    return (xp.reshape(batch, seq, d_model), xs.reshape(n_seq, n_new, d_model), *outs_p, *outs_s)
```

```python
import functools
import math

import numpy as np
import jax
import jax.numpy as jnp
from jax import lax
from jax.experimental import pallas as pl
from jax.experimental.pallas import tpu as pltpu

F32 = jnp.float32
BF16 = jnp.bfloat16
I32 = jnp.int32

HEAD_DIM = 128
N_HEADS = 8
N_IDX_HEADS = 16
IDX_DIM = 64
IDX_W_SCALE = (N_IDX_HEADS * IDX_DIM) ** -0.5
TOP_K_MAX = 256
N_REL_BUCKETS = 32
REL_MAX_DISTANCE = 128
RMS_EPS = 1e-6
NEG_INF = -1e30
ATTN_SCALE = HEAD_DIM ** -0.5
LOG2E = math.log2(math.e)

LANES = 128
SUBLANES = 8
D_ATT = N_HEADS * HEAD_DIM

C_QD, C_KD, C_VD, C_QI, C_QS, C_KS, C_VS = (k * D_ATT for k in range(7))
C_GA = 7 * D_ATT
PROJ_TN = 512


def _key_of(x):
    b = int(np.float32(x).view(np.int32))
    return b if b >= 0 else b ^ 0x7FFFFFFF


KEY_NEG_INF = _key_of(NEG_INF)
INT_MIN = -(2 ** 31)


def _cparams(n_axes, vmem_mb=48):
    return pltpu.CompilerParams(dimension_semantics=("arbitrary",) * n_axes,
                                vmem_limit_bytes=vmem_mb * 1024 * 1024)


def _dot(a, b):
    return jnp.dot(a, b, preferred_element_type=F32)


def _dot_nt(a, b):
    return lax.dot_general(a, b, (((1,), (1,)), ((), ())), preferred_element_type=F32)


def _split_dot(x, m01, passes):
    out = None
    rem = x
    for p in range(passes):
        piece = rem.astype(BF16)
        d = _dot(piece, m01)
        out = d if out is None else out + d
        if p + 1 < passes:
            rem = rem - piece.astype(F32)
    return out


def _softplus(z):
    return jnp.maximum(z, 0.0) + jnp.log1p(jnp.exp(-jnp.abs(z)))


def _float_key(s):
    s = jnp.where(s == 0.0, 0.0, s)
    bits = lax.bitcast_convert_type(s, I32)
    return jnp.where(bits >= 0, bits, bits ^ 0x7FFFFFFF)


def _rel_bucket_np(dist):
    max_exact = N_REL_BUCKETS // 2
    d = np.maximum(dist, 0)
    df = np.maximum(d, 1).astype(np.float32)
    log_part = np.log(df / np.float32(max_exact)) / np.float32(math.log(REL_MAX_DISTANCE / max_exact))
    large = max_exact + (log_part * np.float32(N_REL_BUCKETS - max_exact)).astype(np.int32)
    return np.where(d < max_exact, d, np.minimum(large, N_REL_BUCKETS - 1)).astype(np.int32)


def _rms_cast_kernel(x_ref, g_ref, o_ref):
    x = x_ref[...]
    ms = jnp.mean(x * x, axis=-1, keepdims=True)
    o_ref[...] = (x * lax.rsqrt(ms + RMS_EPS) * g_ref[...]).astype(o_ref.dtype)


def rms_cast(x, g, tm=256):
    m, d = x.shape
    tm = min(tm, m)
    return pl.pallas_call(
        _rms_cast_kernel,
        grid=(m // tm,),
        in_specs=[pl.BlockSpec((tm, d), lambda i: (i, 0)), pl.BlockSpec((1, d), lambda i: (0, 0))],
        out_specs=pl.BlockSpec((tm, d), lambda i: (i, 0)),
        out_shape=jax.ShapeDtypeStruct((m, d), BF16),
        compiler_params=_cparams(1),
        name="rms_cast",
    )(x, g.reshape(1, d))


def _in_proj_kernel(h_ref, wt_ref, o_ref, wb_ref):
    @pl.when(pl.program_id(1) == 0)
    def _():
        wb_ref[...] = wt_ref[...].astype(BF16)

    o_ref[...] = _dot_nt(h_ref[...], wb_ref[...])


def in_proj(h, w_t, tm):
    m, k = h.shape
    in_cols = w_t.shape[0]
    c_ki = 4 * D_ATT
    c_after = c_ki + IDX_DIM + N_IDX_HEADS
    assert c_ki % PROJ_TN == 0 and (in_cols - c_after) % PROJ_TN == 0 and c_ki + PROJ_TN <= in_cols
    n_a = c_ki // PROJ_TN
    n_b = (in_cols - c_after) // PROJ_TN

    assert c_after % SUBLANES == 0 and PROJ_TN % SUBLANES == 0
    tile8, after8, ki8 = PROJ_TN // SUBLANES, c_after // SUBLANES, c_ki // SUBLANES

    def w_row(j):
        return jnp.where(j < n_a, j * tile8, jnp.where(j < n_a + n_b, after8 + (j - n_a) * tile8, ki8)) * SUBLANES

    n_tiles = n_a + n_b + 1
    proj = pl.pallas_call(
        _in_proj_kernel,
        grid=(n_tiles, m // tm),
        in_specs=[pl.BlockSpec((tm, k), lambda j, i: (i, 0)),
                  pl.BlockSpec((pl.Element(PROJ_TN), pl.Element(k)), lambda j, i: (w_row(j), 0))],
        out_specs=pl.BlockSpec((tm, PROJ_TN), lambda j, i: (i, j)),
        out_shape=jax.ShapeDtypeStruct((m, n_tiles * PROJ_TN), F32),
        scratch_shapes=[pltpu.VMEM((PROJ_TN, k), BF16)],
        compiler_params=_cparams(2),
        name="in_proj",
    )(h, w_t)
    return proj, (n_a + n_b) * PROJ_TN


def _merge_kernel(oa_ref, ob_ref, w1_ref, w2_ref, ga_ref, gb_ref, y_ref):
    ya = _dot(oa_ref[...].astype(BF16), w1_ref[...])
    yb = _dot(ob_ref[...].astype(BF16), w2_ref[...])
    sa = 1.0 / (1.0 + jnp.exp(-ga_ref[...]))
    sb = 1.0 / (1.0 + jnp.exp(-gb_ref[...]))
    y_ref[...] = (sa * ya + sb * yb).astype(y_ref.dtype)


def merge_branches(o_a, o_b, w1, w2, proj, tm, tn=512):
    m = o_a.shape[0]
    d = w1.shape[1]
    ga0 = C_GA // tn
    gb0 = (C_GA + d) // tn
    return pl.pallas_call(
        _merge_kernel,
        grid=(m // tm, d // tn),
        in_specs=[pl.BlockSpec((tm, D_ATT), lambda i, j: (i, 0)),
                  pl.BlockSpec((tm, D_ATT), lambda i, j: (i, 0)),
                  pl.BlockSpec((D_ATT, tn), lambda i, j: (0, j)),
                  pl.BlockSpec((D_ATT, tn), lambda i, j: (0, j)),
                  pl.BlockSpec((tm, tn), lambda i, j: (i, ga0 + j)),
                  pl.BlockSpec((tm, tn), lambda i, j: (i, gb0 + j))],
        out_specs=pl.BlockSpec((tm, tn), lambda i, j: (i, j)),
        out_shape=jax.ShapeDtypeStruct((m, d), BF16),
        compiler_params=_cparams(2),
        name="merge",
    )(o_a, o_b, w1, w2, proj, proj)


def _proj_norm_kernel(y_ref, wo_ref, x_ref, gpost_ref, gpre_ref, x1_ref, h2_ref):
    a = _dot(y_ref[...], wo_ref[...])
    r = a * lax.rsqrt(jnp.mean(a * a, axis=-1, keepdims=True) + RMS_EPS) * gpost_ref[...]
    x1 = x_ref[...] + r
    x1_ref[...] = x1
    h2 = x1 * lax.rsqrt(jnp.mean(x1 * x1, axis=-1, keepdims=True) + RMS_EPS) * gpre_ref[...]
    h2_ref[...] = h2.astype(h2_ref.dtype)


def proj_norm(y, wo, x, g_post, g_pre, tm=256):
    m, d = x.shape
    tm = min(tm, m)
    row = lambda i: (i, 0)
    fix = lambda i: (0, 0)
    return pl.pallas_call(
        _proj_norm_kernel,
        grid=(m // tm,),
        in_specs=[pl.BlockSpec((tm, d), row), pl.BlockSpec((d, d), fix), pl.BlockSpec((tm, d), row),
                  pl.BlockSpec((1, d), fix), pl.BlockSpec((1, d), fix)],
        out_specs=[pl.BlockSpec((tm, d), row), pl.BlockSpec((tm, d), row)],
        out_shape=[jax.ShapeDtypeStruct((m, d), F32), jax.ShapeDtypeStruct((m, d), BF16)],
        compiler_params=_cparams(1),
        name="proj_norm",
    )(y, wo, x, g_post.reshape(1, d), g_pre.reshape(1, d))


def _mlp_kernel(h_ref, wu_ref, wd_ref, x_ref, g_ref, o_ref, acc_ref):
    k = pl.program_id(1)

    @pl.when(k == 0)
    def _():
        acc_ref[...] = jnp.zeros_like(acc_ref)

    u = jnp.maximum(_dot(h_ref[...], wu_ref[...]), 0.0)
    acc_ref[...] += _dot((u * u).astype(BF16), wd_ref[...])

    @pl.when(k == pl.num_programs(1) - 1)
    def _():
        a = acc_ref[...]
        o_ref[...] = x_ref[...] + a * lax.rsqrt(jnp.mean(a * a, axis=-1, keepdims=True) + RMS_EPS) * g_ref[...]


def mlp(h2, w_up, w_down, x1, g_post, tm, tk=512):
    m, d = x1.shape
    ff = w_up.shape[1]
    return pl.pallas_call(
        _mlp_kernel,
        grid=(m // tm, ff // tk),
        in_specs=[pl.BlockSpec((tm, d), lambda i, k: (i, 0)),
                  pl.BlockSpec((d, tk), lambda i, k: (0, k)),
                  pl.BlockSpec((tk, d), lambda i, k: (k, 0)),
                  pl.BlockSpec((tm, d), lambda i, k: (i, 0)),
                  pl.BlockSpec((1, d), lambda i, k: (0, 0))],
        out_specs=pl.BlockSpec((tm, d), lambda i, k: (i, 0)),
        out_shape=jax.ShapeDtypeStruct((m, d), F32),
        scratch_shapes=[pltpu.VMEM((tm, d), F32)],
        compiler_params=_cparams(2),
        name="mlp",
    )(h2, w_up, w_down, x1, g_post.reshape(1, d))


QB = LANES
KC = 512
SCORE_KC = 512
AQ = 256


def _dsa_mask_prompt_kernel(qi_ref, sm_all_ref, sm_blk_ref, mask_ref, kia_ref, kib_ref, keys_ref, mstar_ref, qq_ref,
                            *, seq, top_k):
    i = pl.program_id(1)
    n_chunks = (i * QB) // KC + 1

    @pl.when(i == 0)
    def _():
        sm = sm_all_ref[...]
        lane = lax.broadcasted_iota(I32, sm.shape, 1)
        kia_ref[...] = jnp.where(lane < IDX_DIM, sm, 0.0).astype(BF16)
        kib_ref[...] = jnp.where(lane >= IDX_DIM, pltpu.roll(sm, IDX_DIM, axis=1), 0.0).astype(BF16)

    w_t = sm_blk_ref[...].T * IDX_W_SCALE
    qi = qi_ref[...].astype(BF16)
    for g in range(N_IDX_HEADS // 4):
        qq_ref[g] = jnp.concatenate([qi[:, (2 * g) * LANES:(2 * g + 1) * LANES],
                                     qi[:, (2 * g + 1) * LANES:(2 * g + 2) * LANES]], axis=0)
    q_pos_s = i * QB + lax.broadcasted_iota(I32, (SCORE_KC, QB), 1)
    row_s = lax.broadcasted_iota(I32, (SCORE_KC, QB), 0)

    def score_chunk(c, carry):
        r0 = pl.multiple_of(c * SCORE_KC, SCORE_KC)
        ka = kia_ref[pl.ds(r0, SCORE_KC), :]
        kb = kib_ref[pl.ds(r0, SCORE_KC), :]
        acc = jnp.zeros((SCORE_KC, QB), F32)
        for g in range(N_IDX_HEADS // 4):
            qq = qq_ref[g]
            for odd, k_half in ((0, ka), (1, kb)):
                s = jnp.maximum(_dot_nt(k_half, qq), 0.0)
                for j in range(2):
                    head = 4 * g + 2 * j + odd
                    acc = acc + s[:, j * LANES:(j + 1) * LANES] * w_t[IDX_DIM + head:IDX_DIM + head + 1, :]
        sc = jnp.where(c * SCORE_KC + row_s <= q_pos_s, acc, NEG_INF)
        keys_ref[pl.ds(r0, SCORE_KC), :] = _float_key(sc)
        return carry

    lax.fori_loop(0, n_chunks * (KC // SCORE_KC), score_chunk, 0)

    n_tail = seq - n_chunks * KC
    q_pos = i * QB + lax.broadcasted_iota(I32, (KC, QB), 1)
    row = lax.broadcasted_iota(I32, (KC, QB), 0)

    def count(pred_fn):
        def body(c, acc):
            k = keys_ref[pl.ds(pl.multiple_of(c * KC, KC), KC), :]
            ones = jnp.where(pred_fn(k, c), 1, 0)
            return acc + jnp.sum(ones.reshape(KC // SUBLANES, SUBLANES, QB), axis=0)

        acc = lax.fori_loop(0, n_chunks, body, jnp.zeros((SUBLANES, QB), I32))
        return jnp.sum(acc, axis=0, keepdims=True)

    def count_ge(cand):
        return count(lambda k, c: k >= cand) + jnp.where(KEY_NEG_INF >= cand, n_tail, 0)

    base = jnp.where(count_ge(jnp.zeros((1, QB), I32)) >= top_k, 0, INT_MIN).astype(I32)

    def bit_body(t, base):
        cand = base | lax.shift_left(jnp.int32(1), 30 - t)
        return jnp.where(count_ge(cand) >= top_k, cand, base)

    v = lax.fori_loop(0, 31, bit_body, base)

    n_gt = count(lambda k, c: k > v) + jnp.where(KEY_NEG_INF > v, n_tail, 0)
    need = top_k - n_gt
    n_eq = count(lambda k, c: k == v)
    mstar_ref[...] = jnp.full((SUBLANES, QB), seq, I32)

    @pl.when(jnp.max(jnp.where(n_eq > need, 1, 0)) > 0)
    def _():
        def idx_bit(t, m):
            cand = m | lax.shift_left(jnp.int32(1), (seq - 1).bit_length() - 1 - t)
            below = count(lambda k, c: (k == v) & (c * KC + row < cand))
            return jnp.where(below < need, cand, m)

        m = lax.fori_loop(0, (seq - 1).bit_length(), idx_bit, jnp.zeros((1, QB), I32))
        mstar_ref[...] = jnp.broadcast_to(m, (SUBLANES, QB))

    mstar = mstar_ref[0:1, :]

    def emit(c, carry):
        k = keys_ref[pl.ds(pl.multiple_of(c * KC, KC), KC), :]
        k_pos = c * KC + row
        sel = (k > v) | ((k == v) & (k_pos <= mstar))
        m_t = jnp.where(sel & (k_pos <= q_pos), 0.0, NEG_INF)
        mask_ref[c] = m_t.T.astype(mask_ref.dtype)
        return carry

    lax.fori_loop(0, n_chunks, emit, 0)

    def clear(c, carry):
        mask_ref[c] = jnp.full((QB, KC), NEG_INF, mask_ref.dtype)
        return carry

    lax.fori_loop(n_chunks, seq // KC, clear, 0)


def dsa_mask_prompt(proj, batch, seq, c_small):
    assert seq % KC == 0 and KC % AQ == 0 and AQ % QB == 0
    nq = seq // QB
    per = AQ // QB
    top_k = min(TOP_K_MAX, seq // 4)
    return pl.pallas_call(
        functools.partial(_dsa_mask_prompt_kernel, seq=seq, top_k=top_k),
        grid=(batch, nq),
        in_specs=[pl.BlockSpec((QB, N_IDX_HEADS * IDX_DIM), lambda b, i: (b * nq + i, C_QI // (N_IDX_HEADS * IDX_DIM))),
                  pl.BlockSpec((seq, LANES), lambda b, i: (b, c_small // LANES)),
                  pl.BlockSpec((QB, LANES), lambda b, i: (b * nq + i, c_small // LANES))],
        out_specs=pl.BlockSpec((None, seq // KC, None, QB, KC), lambda b, i: ((b * nq + i) // per, 0, i % per, 0, 0)),
        out_shape=jax.ShapeDtypeStruct((batch * seq // AQ, seq // KC, per, QB, KC), BF16),
        scratch_shapes=[pltpu.VMEM((seq, LANES), BF16), pltpu.VMEM((seq, LANES), BF16),
                        pltpu.VMEM((seq, QB), I32), pltpu.VMEM((SUBLANES, QB), I32),
                        pltpu.VMEM((N_IDX_HEADS // 4, 2 * QB, LANES), BF16)],
        compiler_params=_cparams(2),
        name="dsa_mask_prompt",
    )(proj, proj, proj)


def _bucket_thresholds():
    d = np.arange(4 * REL_MAX_DISTANCE)
    b = _rel_bucket_np(d)
    assert np.all(np.diff(b) >= 0) and b[-1] == N_REL_BUCKETS - 1
    return [int(np.argmax(b > j)) for j in range(N_REL_BUCKETS - 1)]


def _bias_of_distance(rb_ref, h, d, thresholds):
    acc = jnp.full(d.shape, rb_ref[N_REL_BUCKETS - 1, h], F32)
    for j in range(N_REL_BUCKETS - 2, -1, -1):
        acc = jnp.where(d < thresholds[j], rb_ref[j, h], acc)
    return acc


def _lane_fold(x, op):
    out = x[:, :LANES]
    for j in range(1, x.shape[1] // LANES):
        out = op(out, x[:, j * LANES:(j + 1) * LANES])
    return out


def _dsa_attn_prompt_kernel(rb_ref, q_ref, k_ref, v_ref, mask_ref, o_ref, tab_ref, lg_ref, kb_ref, vb_ref, acc_ref,
                            l_ref, *, thresholds, unroll):
    h = pl.program_id(1)
    i = pl.program_id(2)
    per = KC // AQ
    part = i % per
    c_diag = i // per

    @pl.when(i == 0)
    def _():
        kb_ref[...] = k_ref[...].astype(BF16)
        vb_ref[...] = v_ref[...].astype(BF16)
        row = lax.broadcasted_iota(I32, (AQ, KC), 0)
        col = lax.broadcasted_iota(I32, (AQ, KC), 1)
        for p in range(per):
            for back in range(2):
                tab_ref[3 * p + back] = LOG2E * _bias_of_distance(rb_ref, h, p * AQ + back * KC + row - col,
                                                                  thresholds)
            tab_ref[3 * p + 2] = jnp.full((AQ, KC), LOG2E * rb_ref[N_REL_BUCKETS - 1, h], F32)

    q = q_ref[...].astype(BF16)

    n_trips = c_diag // unroll + 1

    def logits_chunks(t, mx):
        for u in range(unroll):
            c = t * unroll + u
            kc = kb_ref[pl.ds(pl.multiple_of(c * KC, KC), KC), :]
            s = _dot_nt(q, kc) * (ATTN_SCALE * LOG2E) + tab_ref[3 * part + jnp.clip(c_diag - c, 0, 2)]
            s = s + mask_ref[c].reshape(AQ, KC).astype(F32)
            lg_ref[c] = s
            mx = jnp.maximum(mx, _lane_fold(s, jnp.maximum))
        return mx

    mx = lax.fori_loop(0, n_trips, logits_chunks, jnp.full((AQ, LANES), NEG_INF, F32))
    m = jnp.max(mx, axis=1, keepdims=True)
    acc_ref[...] = jnp.zeros_like(acc_ref)
    l_ref[...] = jnp.zeros_like(l_ref)

    def pv_chunks(t, carry):
        for u in range(unroll):
            c = t * unroll + u
            p = jnp.exp2(lg_ref[c] - m)
            l_ref[...] += _lane_fold(p, jnp.add)
            acc_ref[...] += _dot(p.astype(BF16), vb_ref[pl.ds(pl.multiple_of(c * KC, KC), KC), :])
        return carry

    lax.fori_loop(0, n_trips, pv_chunks, 0)
    o_ref[...] = (acc_ref[...] / jnp.sum(l_ref[...], axis=1, keepdims=True)).astype(o_ref.dtype)


def dsa_attn_prompt(proj, mask, rel_bias, batch, seq):
    nq = seq // AQ
    nk = seq // KC
    hd = HEAD_DIM
    thresholds = _bucket_thresholds()
    assert thresholds[-1] <= KC + 1
    return pl.pallas_call(
        functools.partial(_dsa_attn_prompt_kernel, thresholds=thresholds, unroll=2 if nk % 2 == 0 else 1),
        grid=(batch, N_HEADS, nq),
        in_specs=[pl.BlockSpec(memory_space=pltpu.SMEM),
                  pl.BlockSpec((AQ, hd), lambda b, h, i: (b * nq + i, C_QD // hd + h)),
                  pl.BlockSpec((seq, hd), lambda b, h, i: (b, C_KD // hd + h)),
                  pl.BlockSpec((seq, hd), lambda b, h, i: (b, C_VD // hd + h)),
                  pl.BlockSpec((None, nk, AQ // QB, QB, KC), lambda b, h, i: (b * nq + i, 0, 0, 0, 0))],
        out_specs=pl.BlockSpec((AQ, hd), lambda b, h, i: (b * nq + i, h)),
        out_shape=jax.ShapeDtypeStruct((batch * seq, D_ATT), BF16),
        scratch_shapes=[pltpu.VMEM((3 * (KC // AQ), AQ, KC), F32), pltpu.VMEM((nk, AQ, KC), F32),
                        pltpu.VMEM((seq, hd), BF16), pltpu.VMEM((seq, hd), BF16),
                        pltpu.VMEM((AQ, hd), F32), pltpu.VMEM((AQ, LANES), F32)],
        compiler_params=_cparams(3),
        name="dsa_attn_prompt",
    )(rel_bias, proj, proj, proj, mask)


SB_T = 256
SB_HEADS = 2
EXP_ZERO_BELOW = -104.0


def _suffix_sum_matrix(n):
    s = np.arange(n)
    tri = (s[:, None] > s[None, :]).astype(np.float32)
    return np.concatenate([tri, np.ones((n, n), np.float32)], axis=1)


def _sb_prompt_kernel(q_ref, k_ref, v_ref, r_ref, o_ref, kb_ref, vb_ref, acc_ref, run_ref):
    i = pl.program_id(2)

    @pl.when(i == 0)
    def _():
        kb_ref[...] = k_ref[...].astype(BF16)
        vb_ref[...] = v_ref[...].astype(BF16)

    q = q_ref[...].astype(BF16)
    acc_ref[...] = jnp.zeros_like(acc_ref)
    run_ref[...] = jnp.zeros_like(run_ref)
    row = lax.broadcasted_iota(I32, (SB_T, SB_T), 0)
    col = lax.broadcasted_iota(I32, (SB_T, SB_T), 1)

    def chunk(carry):
        step, _ = carry
        c = i - step
        r0 = pl.multiple_of(c * SB_T, SB_T)
        before = col < row + step * SB_T
        live = None
        for hh in range(SB_HEADS):
            cols = slice(hh * HEAD_DIM, (hh + 1) * HEAD_DIM)
            z = _dot_nt(q[:, cols], kb_ref[pl.ds(r0, SB_T), cols]) * ATTN_SCALE
            sp = _softplus(z)
            log_keep = jnp.where(before, -sp, 0.0)
            cs = _split_dot(log_keep, r_ref[...], 2)
            later = cs[:, :SB_T] + run_ref[hh]
            a = jnp.where(before, jnp.exp(z - sp + later), 0.0)
            acc_ref[:, cols] += _dot(a.astype(BF16), vb_ref[pl.ds(r0, SB_T), cols])
            run = run_ref[hh] + cs[:, SB_T:]
            run_ref[hh] = run
            top = jnp.max(run[:, :LANES])
            live = top if live is None else jnp.maximum(live, top)
        return step + 1, (live >= EXP_ZERO_BELOW).astype(I32)

    lax.while_loop(lambda carry: (carry[0] <= i) & (carry[1] > 0), chunk, (jnp.int32(0), jnp.int32(1)))
    o_ref[...] = acc_ref[...].astype(o_ref.dtype)


def sb_prompt(proj, batch, seq):
    nq = seq // SB_T
    w = SB_HEADS * HEAD_DIM
    r = jnp.asarray(_suffix_sum_matrix(SB_T), BF16)
    return pl.pallas_call(
        _sb_prompt_kernel,
        grid=(batch, N_HEADS // SB_HEADS, nq),
        in_specs=[pl.BlockSpec((SB_T, w), lambda b, h, i: (b * nq + i, C_QS // w + h)),
                  pl.BlockSpec((seq, w), lambda b, h, i: (b, C_KS // w + h)),
                  pl.BlockSpec((seq, w), lambda b, h, i: (b, C_VS // w + h)),
                  pl.BlockSpec((SB_T, 2 * SB_T), lambda b, h, i: (0, 0))],
        out_specs=pl.BlockSpec((SB_T, w), lambda b, h, i: (b * nq + i, h)),
        out_shape=jax.ShapeDtypeStruct((batch * seq, D_ATT), BF16),
        scratch_shapes=[pltpu.VMEM((seq, w), BF16), pltpu.VMEM((seq, w), BF16),
                        pltpu.VMEM((SB_T, w), F32), pltpu.VMEM((SB_HEADS, SB_T, SB_T), F32)],
        compiler_params=_cparams(3),
        name="sb_prompt",
    )(proj, proj, proj, r)


TOK_PAD = SUBLANES


def _pad_rows(x, rows):
    return jnp.concatenate([x, jnp.zeros((rows - x.shape[0], x.shape[1]), x.dtype)], axis=0)


def _head_rows(page_ref, h, page_size):
    return page_ref[pl.ds(h, page_size, stride=N_HEADS), :].astype(BF16)


def _page_specs(n_pages, rows, cols):
    return [pl.BlockSpec((None, rows, cols), functools.partial(lambda b, pt, p: (pt[b, p], 0, 0), p=p))
            for p in range(n_pages)]


def _sb_sample_kernel(pt_ref, q_ref, kn_ref, vn_ref, r_ref, k_hbm, v_hbm, o_ref, kbuf, vbuf, sems, run_ref, acc_ref,
                      *, n_pages, page_size, n_new, n_pre):
    b = pl.program_id(0)
    grp = N_HEADS * TOK_PAD
    spare = 2 * n_pre

    def page_copies(seq, page, slot):
        phys = pt_ref[seq, page]
        return (pltpu.make_async_copy(k_hbm.at[phys], kbuf.at[slot], sems.at[0, slot]),
                pltpu.make_async_copy(v_hbm.at[phys], vbuf.at[slot], sems.at[1, slot]))

    def prefetch(seq, action):
        for j in range(n_pre):
            for cp in page_copies(seq, n_pages - 1 - j, (seq % 2) * n_pre + j):
                action(cp)

    @pl.when(b == 0)
    def _():
        prefetch(b, lambda cp: cp.start())

    @pl.when(b + 1 < pl.num_programs(0))
    def _():
        prefetch(b + 1, lambda cp: cp.start())

    q = [q_ref[:, h * HEAD_DIM:(h + 1) * HEAD_DIM].astype(BF16) for h in range(N_HEADS)]
    row = lax.broadcasted_iota(I32, (grp, page_size), 0) % TOK_PAD
    lane = lax.broadcasted_iota(I32, (grp, page_size), 1)
    run_ref[...] = jnp.zeros_like(run_ref)
    acc_ref[...] = jnp.zeros_like(acc_ref)

    def process(k_h, v_h, before):
        z = jnp.concatenate([_dot_nt(q[h], k_h[h]) for h in range(N_HEADS)], axis=0) * ATTN_SCALE
        sp = _softplus(z)
        log_keep = -sp if before is None else jnp.where(before, -sp, 0.0)
        cs = _split_dot(log_keep, r_ref[...], 3)
        run = run_ref[...]
        e = jnp.exp(z - sp + cs[:, :page_size] + run)
        a = e if before is None else jnp.where(before, e, 0.0)
        run = run + cs[:, page_size:]
        run_ref[...] = run
        for h in range(N_HEADS):
            rows = slice(h * TOK_PAD, (h + 1) * TOK_PAD)
            acc_ref[rows, :] += _dot(a[rows, :].astype(BF16), v_h[h])
        return (jnp.max(jnp.where(row < n_new, run, NEG_INF)) >= EXP_ZERO_BELOW).astype(I32)

    def process_slot(slot):
        k_h = [kbuf[slot, pl.ds(h, page_size, stride=N_HEADS), :].astype(BF16) for h in range(N_HEADS)]
        v_h = [vbuf[slot, pl.ds(h, page_size, stride=N_HEADS), :].astype(BF16) for h in range(N_HEADS)]
        return process(k_h, v_h, None)

    process([_pad_rows(kn_ref[:, h * HEAD_DIM:(h + 1) * HEAD_DIM], page_size).astype(BF16) for h in range(N_HEADS)],
            [_pad_rows(vn_ref[:, h * HEAD_DIM:(h + 1) * HEAD_DIM], page_size).astype(BF16) for h in range(N_HEADS)],
            lane < row)
    prefetch(b, lambda cp: cp.wait())
    live = jnp.int32(1)
    for j in range(n_pre):
        live = lax.cond(live > 0, functools.partial(process_slot, (b % 2) * n_pre + j), lambda: jnp.int32(0))

    def fetch_and_process(carry):
        page, _ = carry
        for cp in page_copies(b, page, spare):
            cp.start()
        for cp in page_copies(b, page, spare):
            cp.wait()
        return page - 1, process_slot(spare)

    lax.while_loop(lambda carry: (carry[0] >= 0) & (carry[1] > 0), fetch_and_process,
                   (jnp.int32(n_pages - n_pre - 1), live))
    for h in range(N_HEADS):
        o_ref[:, h * HEAD_DIM:(h + 1) * HEAD_DIM] = acc_ref[h * TOK_PAD:(h + 1) * TOK_PAD, :]


def sb_sample(proj8, k_pool, v_pool, page_table, n_new):
    n_seq = proj8.shape[0]
    n_pages = page_table.shape[1]
    page_size = k_pool.shape[1] // N_HEADS
    assert page_size == LANES
    n_pre = min(2, n_pages)
    r = jnp.asarray(_suffix_sum_matrix(page_size), BF16)
    tok = lambda col: pl.BlockSpec((None, TOK_PAD, D_ATT), lambda b, pt: (b, 0, col // D_ATT))
    page_buf = pltpu.VMEM((2 * n_pre + 1, page_size * N_HEADS, HEAD_DIM), F32)
    grid_spec = pltpu.PrefetchScalarGridSpec(
        num_scalar_prefetch=1,
        grid=(n_seq,),
        in_specs=[tok(C_QS), tok(C_KS), tok(C_VS), pl.BlockSpec((page_size, 2 * page_size), lambda b, pt: (0, 0)),
                  pl.BlockSpec(memory_space=pl.ANY), pl.BlockSpec(memory_space=pl.ANY)],
        out_specs=pl.BlockSpec((None, TOK_PAD, D_ATT), lambda b, pt: (b, 0, 0)),
        scratch_shapes=[page_buf, page_buf, pltpu.SemaphoreType.DMA((2, 2 * n_pre + 1)),
                        pltpu.VMEM((N_HEADS * TOK_PAD, page_size), F32), pltpu.VMEM((N_HEADS * TOK_PAD, HEAD_DIM), F32)],
    )
    return pl.pallas_call(
        functools.partial(_sb_sample_kernel, n_pages=n_pages, page_size=page_size, n_new=n_new, n_pre=n_pre),
        grid_spec=grid_spec,
        out_shape=jax.ShapeDtypeStruct((n_seq, TOK_PAD, D_ATT), F32),
        compiler_params=_cparams(1),
        name="sb_sample",
    )(page_table, proj8, proj8, proj8, r, k_pool, v_pool)


def _dsa_scores_sample_kernel(pt_ref, qi_ref, w_ref, kin_ref, *rest, n_pages, page_size, n_new):
    ki_pages = rest[:n_pages]
    o_ref = rest[n_pages]
    qi = qi_ref[...].astype(BF16)
    w = w_ref[...] * IDX_W_SCALE
    o_ref[...] = jnp.zeros_like(o_ref)
    for g in range(n_pages + 1):
        ki_t = (kin_ref if g == n_pages else ki_pages[g])[...].astype(BF16)
        s = jnp.maximum(_dot(qi, ki_t), 0.0) * w
        s = jnp.sum(s.reshape(n_new, N_IDX_HEADS, page_size), axis=1)
        o_ref[0:n_new, g * page_size:(g + 1) * page_size] = s


def dsa_scores_sample(qi3, w_rep, ki_new_t, ki_pool_t, page_table):
    n_seq = qi3.shape[0]
    n_new = qi3.shape[1] // N_IDX_HEADS
    n_pages = page_table.shape[1]
    page_size = ki_pool_t.shape[2]
    assert page_size == LANES
    seq_blk = lambda r, c: pl.BlockSpec((None, r, c), lambda b, pt: (b, 0, 0))
    grid_spec = pltpu.PrefetchScalarGridSpec(
        num_scalar_prefetch=1,
        grid=(n_seq,),
        in_specs=[seq_blk(n_new * N_IDX_HEADS, IDX_DIM), seq_blk(n_new * N_IDX_HEADS, LANES),
                  seq_blk(IDX_DIM, page_size)] + _page_specs(n_pages, IDX_DIM, page_size),
        out_specs=seq_blk(TOK_PAD, (n_pages + 1) * page_size),
    )
    return pl.pallas_call(
        functools.partial(_dsa_scores_sample_kernel, n_pages=n_pages, page_size=page_size, n_new=n_new),
        grid_spec=grid_spec,
        out_shape=jax.ShapeDtypeStruct((n_seq, TOK_PAD, (n_pages + 1) * page_size), F32),
        compiler_params=_cparams(1),
        name="dsa_scores_sample",
    )(page_table, qi3, w_rep, ki_new_t, *([ki_pool_t] * n_pages))


MASK_ROWS = 128


def _dsa_mask_sample_kernel(s_ref, ones_ref, m_ref, *, n_past, n_new, top_k):
    n_chunks = s_ref.shape[1] // LANES
    shape = (MASK_ROWS, LANES)
    tok = lax.broadcasted_iota(I32, shape, 0) % TOK_PAD
    lane = lax.broadcasted_iota(I32, shape, 1)
    ones = ones_ref[...]
    keys, pos, exists = [], [], []
    for c in range(n_chunks):
        s = s_ref[:, c * LANES:(c + 1) * LANES]
        k_pos = c * LANES + lane
        s = jnp.where(k_pos <= n_past + tok, s, NEG_INF)
        keys.append(_float_key(s))
        pos.append(k_pos)
        exists.append(None if (c + 1) * LANES <= n_past + n_new else k_pos < n_past + n_new)

    def count(pred_fn):
        part = jnp.zeros(shape, F32)
        for c in range(n_chunks):
            p = pred_fn(keys[c], pos[c])
            if exists[c] is not None:
                p = p & exists[c]
            part = part + jnp.where(p, 1.0, 0.0)
        return _dot(part.astype(BF16), ones)

    base = jnp.where(count(lambda k, p: k >= 0) >= top_k, 0, INT_MIN).astype(I32)

    def bit_body(t, base):
        cand = base | lax.shift_left(jnp.int32(1), 30 - t)
        return jnp.where(count(lambda k, p: k >= cand) >= top_k, cand, base)

    v = lax.fori_loop(0, 31, bit_body, base)
    need = top_k - count(lambda k, p: k > v)

    n_bits = (n_chunks * LANES - 1).bit_length()

    def idx_bit(t, m):
        cand = m | lax.shift_left(jnp.int32(1), n_bits - 1 - t)
        below = count(lambda k, p: (k == v) & (p < cand))
        return jnp.where(below < need, cand, m)

    mstar = lax.fori_loop(0, n_bits, idx_bit, jnp.zeros(shape, I32))
    for c in range(n_chunks):
        sel = (keys[c] > v) | ((keys[c] == v) & (pos[c] <= mstar))
        sel = sel & (pos[c] <= n_past + tok) & (tok < n_new)
        if exists[c] is not None:
            sel = sel & exists[c]
        m_ref[:, c * LANES:(c + 1) * LANES] = jnp.where(sel, 1.0, 0.0)


def dsa_mask_sample(scores2, n_past, n_new):
    rows, width = scores2.shape
    top_k = min(TOP_K_MAX, (n_past + n_new) // 4)
    return pl.pallas_call(
        functools.partial(_dsa_mask_sample_kernel, n_past=n_past, n_new=n_new, top_k=top_k),
        grid=(rows // MASK_ROWS,),
        in_specs=[pl.BlockSpec((MASK_ROWS, width), lambda i: (i, 0)), pl.BlockSpec((LANES, LANES), lambda i: (0, 0))],
        out_specs=pl.BlockSpec((MASK_ROWS, width), lambda i: (i, 0)),
        out_shape=jax.ShapeDtypeStruct((rows, width), F32),
        compiler_params=_cparams(1),
        name="dsa_mask_sample",
    )(scores2, jnp.ones((LANES, LANES), BF16))


def _sample_bucket_tables(n_new, page_size):
    j = (np.arange(N_HEADS * TOK_PAD) % TOK_PAD)[:, None]
    u = np.arange(page_size)[None, :]
    assert _rel_bucket_np(np.array([page_size + 1]))[0] == N_REL_BUCKETS - 1
    return np.stack([_rel_bucket_np(page_size + j - u), _rel_bucket_np(j - u)]).astype(np.int32)


def _dsa_attn_sample_kernel(pt_ref, rb_ref, q_ref, kn_ref, vn_ref, m_ref, bkt_ref, *rest, n_pages, page_size, n_new):
    k_pages = rest[:n_pages]
    v_pages = rest[n_pages:2 * n_pages]
    o_ref = rest[2 * n_pages]
    tab_ref = rest[2 * n_pages + 1]
    lg_ref = rest[2 * n_pages + 2]
    grp = N_HEADS * TOK_PAD

    @pl.when(pl.program_id(0) == 0)
    def _():
        for h in range(N_HEADS):
            rows = slice(h * TOK_PAD, (h + 1) * TOK_PAD)
            for t in range(2):
                bk = bkt_ref[t, rows, :]
                acc = jnp.zeros(bk.shape, F32)
                for b in range(N_REL_BUCKETS):
                    acc = jnp.where(bk == b, rb_ref[b, h], acc)
                tab_ref[t, rows, :] = acc
            tab_ref[2, rows, :] = jnp.full((TOK_PAD, page_size), rb_ref[N_REL_BUCKETS - 1, h], F32)

    q = [q_ref[:, h * HEAD_DIM:(h + 1) * HEAD_DIM].astype(BF16) for h in range(N_HEADS)]
    mx = jnp.full((grp, page_size), NEG_INF, F32)
    for g in range(n_pages + 1):
        if g == n_pages:
            k_h = [_pad_rows(kn_ref[:, h * HEAD_DIM:(h + 1) * HEAD_DIM], page_size).astype(BF16) for h in range(N_HEADS)]
            bias = tab_ref[1]
        else:
            k_h = [_head_rows(k_pages[g], h, page_size) for h in range(N_HEADS)]
            bias = tab_ref[0] if g == n_pages - 1 else tab_ref[2]
        s = jnp.concatenate([_dot_nt(q[h], k_h[h]) for h in range(N_HEADS)], axis=0) * ATTN_SCALE + bias
        sel = jnp.tile(m_ref[:, g * page_size:(g + 1) * page_size], (N_HEADS, 1))
        s = jnp.where(sel > 0.5, s, NEG_INF)
        lg_ref[g] = s
        mx = jnp.maximum(mx, s)
    m = jnp.max(mx, axis=1, keepdims=True)
    l = jnp.zeros((grp, page_size), F32)
    acc = [jnp.zeros((TOK_PAD, HEAD_DIM), F32) for _ in range(N_HEADS)]
    for g in range(n_pages + 1):
        p = jnp.exp(lg_ref[g] - m)
        l = l + p
        for h in range(N_HEADS):
            if g == n_pages:
                v_h = _pad_rows(vn_ref[:, h * HEAD_DIM:(h + 1) * HEAD_DIM], page_size).astype(BF16)
            else:
                v_h = _head_rows(v_pages[g], h, page_size)
            acc[h] = acc[h] + _dot(p[h * TOK_PAD:(h + 1) * TOK_PAD, :].astype(BF16), v_h)
    inv = 1.0 / jnp.sum(l, axis=1, keepdims=True)
    for h in range(N_HEADS):
        o_ref[:, h * HEAD_DIM:(h + 1) * HEAD_DIM] = acc[h] * inv[h * TOK_PAD:(h + 1) * TOK_PAD, :]


def dsa_attn_sample(proj8, mask3, rel_bias, k_pool, v_pool, page_table, n_new):
    n_seq = proj8.shape[0]
    n_pages = page_table.shape[1]
    page_size = k_pool.shape[1] // N_HEADS
    assert page_size == LANES
    grp = N_HEADS * TOK_PAD
    width = (n_pages + 1) * page_size
    tok = lambda col: pl.BlockSpec((None, TOK_PAD, D_ATT), lambda b, pt: (b, 0, col // D_ATT))
    grid_spec = pltpu.PrefetchScalarGridSpec(
        num_scalar_prefetch=1,
        grid=(n_seq,),
        in_specs=[pl.BlockSpec(memory_space=pltpu.SMEM), tok(C_QD), tok(C_KD), tok(C_VD),
                  pl.BlockSpec((None, TOK_PAD, width), lambda b, pt: (b, 0, 0)),
                  pl.BlockSpec((2, grp, page_size), lambda b, pt: (0, 0, 0))]
                 + _page_specs(n_pages, page_size * N_HEADS, HEAD_DIM)
                 + _page_specs(n_pages, page_size * N_HEADS, HEAD_DIM),
        out_specs=pl.BlockSpec((None, TOK_PAD, D_ATT), lambda b, pt: (b, 0, 0)),
        scratch_shapes=[pltpu.VMEM((3, grp, page_size), F32), pltpu.VMEM((n_pages + 1, grp, page_size), F32)],
    )
    return pl.pallas_call(
        functools.partial(_dsa_attn_sample_kernel, n_pages=n_pages, page_size=page_size, n_new=n_new),
        grid_spec=grid_spec,
        out_shape=jax.ShapeDtypeStruct((n_seq, TOK_PAD, D_ATT), F32),
        compiler_params=_cparams(1, vmem_mb=56),
        name="dsa_attn_sample",
    )(page_table, rel_bias, proj8, proj8, proj8, mask3, jnp.asarray(_sample_bucket_tables(n_new, page_size)),
      *([k_pool] * n_pages), *([v_pool] * n_pages))


def _tail(x, proj, o_a, o_b, weights, gains):
    w1, w2, wo, wu, wd = weights
    g_attn_post, g_mlp_pre, g_mlp_post = gains
    tm = min(512, x.shape[0])
    y = merge_branches(o_a, o_b, w1, w2, proj, tm)
    x1, h2 = proj_norm(y, wo, x, g_attn_post, g_mlp_pre)
    return mlp(h2, wu, wd, x1, g_mlp_post, tm)


def kernel(x_prompt, x_sample, cache_k_dsa, cache_v_dsa, cache_k_idx, cache_k_sb, cache_v_sb, page_table, rel_bias,
           w_in, w_out_dsa, w_out_sb, w_o, w_up, w_down, g_attn_pre, g_attn_post, g_mlp_pre, g_mlp_post):
    batch, seq, d_model = x_prompt.shape
    n_seq, n_new, _ = x_sample.shape
    depth = w_in.shape[0]
    n_pool, page_size = cache_k_idx.shape[1], cache_k_idx.shape[2]
    n_past = page_table.shape[1] * page_size

    xp = x_prompt.reshape(batch * seq, d_model)
    xs = x_sample.reshape(n_seq * n_new, d_model)
    rows_p, rows_s = [], []
    for l in range(depth):
        w_in_t = jnp.swapaxes(w_in[l], 0, 1)
        weights = tuple(w[l].astype(BF16) for w in (w_out_dsa, w_out_sb, w_o, w_up, w_down))
        gains = (g_attn_post[l], g_mlp_pre[l], g_mlp_post[l])

        hp = rms_cast(xp, g_attn_pre[l])
        proj, c_small = in_proj(hp, w_in_t, tm=min(1024, hp.shape[0]))
        mask = dsa_mask_prompt(proj, batch, seq, c_small)
        o_a = dsa_attn_prompt(proj, mask, rel_bias, batch, seq)
        o_b = sb_prompt(proj, batch, seq)
        xp_new = _tail(xp, proj, o_a, o_b, weights, gains)
        heads = lambda c: proj[:, c:c + D_ATT].reshape(batch, seq, N_HEADS, HEAD_DIM)
        rows_p.append((heads(C_KD), heads(C_VD), proj[:, c_small:c_small + IDX_DIM].reshape(batch, seq, IDX_DIM),
                       heads(C_KS), heads(C_VS)))
        xp = xp_new

        hs = rms_cast(xs, g_attn_pre[l])
        proj_s, _ = in_proj(hs, w_in_t, tm=min(512, hs.shape[0]))
        proj8 = jnp.pad(proj_s[:, :C_GA].reshape(n_seq, n_new, C_GA), ((0, 0), (0, TOK_PAD - n_new), (0, 0)))
        qi3 = proj_s[:, C_QI:C_QI + N_IDX_HEADS * IDX_DIM].reshape(n_seq, n_new * N_IDX_HEADS, IDX_DIM)
        w_rep = jnp.broadcast_to(
            proj_s[:, c_small + IDX_DIM:c_small + IDX_DIM + N_IDX_HEADS].reshape(n_seq, n_new * N_IDX_HEADS, 1),
            (n_seq, n_new * N_IDX_HEADS, LANES))
        ki_new = proj_s[:, c_small:c_small + IDX_DIM].reshape(n_seq, n_new, IDX_DIM)
        ki_new_t = jnp.pad(jnp.swapaxes(ki_new, 1, 2), ((0, 0), (0, 0), (0, page_size - n_new)))
        pool2 = lambda c: c[l].reshape(n_pool, page_size * N_HEADS, HEAD_DIM)
        scores = dsa_scores_sample(qi3, w_rep, ki_new_t, jnp.swapaxes(cache_k_idx[l], 1, 2), page_table)
        mask_s = dsa_mask_sample(scores.reshape(n_seq * TOK_PAD, scores.shape[2]), n_past, n_new)
        o_a = dsa_attn_sample(proj8, mask_s.reshape(scores.shape), rel_bias, pool2(cache_k_dsa), pool2(cache_v_dsa),
                              page_table, n_new)
        o_b = sb_sample(proj8, pool2(cache_k_sb), pool2(cache_v_sb), page_table, n_new)
        unpad = lambda o: o[:, :n_new, :].reshape(n_seq * n_new, D_ATT)
        xs_new = _tail(xs, proj_s, unpad(o_a), unpad(o_b), weights, gains)
        heads_s = lambda c: proj_s[:, c:c + D_ATT].reshape(n_seq, n_new, N_HEADS, HEAD_DIM)
        rows_s.append((heads_s(C_KD), heads_s(C_VD), ki_new, heads_s(C_KS), heads_s(C_VS)))
        xs = xs_new

    outs_p = [jnp.stack(r, axis=0) for r in zip(*rows_p)]
    outs_s = [jnp.stack(r, axis=0) for r in zip(*rows_s)]
    return (xp.reshape(batch, seq, d_model), xs.reshape(n_seq, n_new, d_model), *outs_p, *outs_s)
```

```python
import functools
import math

import numpy as np
import jax
import jax.numpy as jnp
from jax import lax
from jax.experimental import pallas as pl
from jax.experimental.pallas import tpu as pltpu

F32 = jnp.float32
BF16 = jnp.bfloat16
I32 = jnp.int32

HEAD_DIM = 128
N_HEADS = 8
N_IDX_HEADS = 16
IDX_DIM = 64
IDX_W_SCALE = (N_IDX_HEADS * IDX_DIM) ** -0.5
TOP_K_MAX = 256
N_REL_BUCKETS = 32
REL_MAX_DISTANCE = 128
RMS_EPS = 1e-6
NEG_INF = -1e30
ATTN_SCALE = HEAD_DIM ** -0.5
LOG2E = math.log2(math.e)

LANES = 128
SUBLANES = 8
D_ATT = N_HEADS * HEAD_DIM

C_QD, C_KD, C_VD, C_QI, C_QS, C_KS, C_VS = (k * D_ATT for k in range(7))
C_GA = 7 * D_ATT
PROJ_TN = 1024


def _key_of(x):
    b = int(np.float32(x).view(np.int32))
    return b if b >= 0 else b ^ 0x7FFFFFFF


KEY_NEG_INF = _key_of(NEG_INF)
INT_MIN = -(2 ** 31)


def _cparams(n_axes, vmem_mb=48):
    return pltpu.CompilerParams(dimension_semantics=("arbitrary",) * n_axes,
                                vmem_limit_bytes=vmem_mb * 1024 * 1024)


def _dot(a, b):
    return jnp.dot(a, b, preferred_element_type=F32)


def _dot_nt(a, b):
    return lax.dot_general(a, b, (((1,), (1,)), ((), ())), preferred_element_type=F32)


def _split_dot(x, m01, passes):
    out = None
    rem = x
    for p in range(passes):
        piece = rem.astype(BF16)
        d = _dot(piece, m01)
        out = d if out is None else out + d
        if p + 1 < passes:
            rem = rem - piece.astype(F32)
    return out


def _softplus(z):
    return jnp.maximum(z, 0.0) + jnp.log1p(jnp.exp(-jnp.abs(z)))


def _float_key(s):
    s = jnp.where(s == 0.0, 0.0, s)
    bits = lax.bitcast_convert_type(s, I32)
    return jnp.where(bits >= 0, bits, bits ^ 0x7FFFFFFF)


def _rel_bucket_np(dist):
    max_exact = N_REL_BUCKETS // 2
    d = np.maximum(dist, 0)
    df = np.maximum(d, 1).astype(np.float32)
    log_part = np.log(df / np.float32(max_exact)) / np.float32(math.log(REL_MAX_DISTANCE / max_exact))
    large = max_exact + (log_part * np.float32(N_REL_BUCKETS - max_exact)).astype(np.int32)
    return np.where(d < max_exact, d, np.minimum(large, N_REL_BUCKETS - 1)).astype(np.int32)


def _rms_cast_kernel(x_ref, g_ref, o_ref):
    x = x_ref[...]
    ms = jnp.mean(x * x, axis=-1, keepdims=True)
    o_ref[...] = (x * lax.rsqrt(ms + RMS_EPS) * g_ref[...]).astype(o_ref.dtype)


def rms_cast(x, g, tm=256):
    m, d = x.shape
    tm = min(tm, m)
    return pl.pallas_call(
        _rms_cast_kernel,
        grid=(m // tm,),
        in_specs=[pl.BlockSpec((tm, d), lambda i: (i, 0)), pl.BlockSpec((1, d), lambda i: (0, 0))],
        out_specs=pl.BlockSpec((tm, d), lambda i: (i, 0)),
        out_shape=jax.ShapeDtypeStruct((m, d), BF16),
        compiler_params=_cparams(1),
        name="rms_cast",
    )(x, g.reshape(1, d))


def _in_proj_kernel(h_ref, wt_ref, o_ref, wb_ref):
    @pl.when(pl.program_id(1) == 0)
    def _():
        wb_ref[...] = wt_ref[...].astype(BF16)

    o_ref[...] = _dot_nt(h_ref[...], wb_ref[...])


def in_proj(h, w_t, tm):
    m, k = h.shape
    in_cols = w_t.shape[0]
    c_ki = 4 * D_ATT
    c_after = c_ki + IDX_DIM + N_IDX_HEADS
    assert c_ki % PROJ_TN == 0 and (in_cols - c_after) % PROJ_TN == 0 and c_ki + PROJ_TN <= in_cols
    n_a = c_ki // PROJ_TN
    n_b = (in_cols - c_after) // PROJ_TN

    assert c_after % SUBLANES == 0 and PROJ_TN % SUBLANES == 0
    tile8, after8, ki8 = PROJ_TN // SUBLANES, c_after // SUBLANES, c_ki // SUBLANES

    def w_row(j):
        return jnp.where(j < n_a, j * tile8, jnp.where(j < n_a + n_b, after8 + (j - n_a) * tile8, ki8)) * SUBLANES

    n_tiles = n_a + n_b + 1
    proj = pl.pallas_call(
        _in_proj_kernel,
        grid=(n_tiles, m // tm),
        in_specs=[pl.BlockSpec((tm, k), lambda j, i: (i, 0)),
                  pl.BlockSpec((pl.Element(PROJ_TN), pl.Element(k)), lambda j, i: (w_row(j), 0))],
        out_specs=pl.BlockSpec((tm, PROJ_TN), lambda j, i: (i, j)),
        out_shape=jax.ShapeDtypeStruct((m, n_tiles * PROJ_TN), F32),
        scratch_shapes=[pltpu.VMEM((PROJ_TN, k), BF16)],
        compiler_params=_cparams(2),
        name="in_proj",
    )(h, w_t)
    return proj, (n_a + n_b) * PROJ_TN


def _merge_kernel(oa_ref, ob_ref, w1_ref, w2_ref, ga_ref, gb_ref, y_ref):
    ya = _dot(oa_ref[...].astype(BF16), w1_ref[...])
    yb = _dot(ob_ref[...].astype(BF16), w2_ref[...])
    sa = 1.0 / (1.0 + jnp.exp(-ga_ref[...]))
    sb = 1.0 / (1.0 + jnp.exp(-gb_ref[...]))
    y_ref[...] = (sa * ya + sb * yb).astype(y_ref.dtype)


def merge_branches(o_a, o_b, w1, w2, proj, tm, tn=512):
    m = o_a.shape[0]
    d = w1.shape[1]
    ga0 = C_GA // tn
    gb0 = (C_GA + d) // tn
    return pl.pallas_call(
        _merge_kernel,
        grid=(m // tm, d // tn),
        in_specs=[pl.BlockSpec((tm, D_ATT), lambda i, j: (i, 0)),
                  pl.BlockSpec((tm, D_ATT), lambda i, j: (i, 0)),
                  pl.BlockSpec((D_ATT, tn), lambda i, j: (0, j)),
                  pl.BlockSpec((D_ATT, tn), lambda i, j: (0, j)),
                  pl.BlockSpec((tm, tn), lambda i, j: (i, ga0 + j)),
                  pl.BlockSpec((tm, tn), lambda i, j: (i, gb0 + j))],
        out_specs=pl.BlockSpec((tm, tn), lambda i, j: (i, j)),
        out_shape=jax.ShapeDtypeStruct((m, d), BF16),
        compiler_params=_cparams(2),
        name="merge",
    )(o_a, o_b, w1, w2, proj, proj)


def _proj_norm_kernel(y_ref, wo_ref, x_ref, gpost_ref, gpre_ref, x1_ref, h2_ref):
    a = _dot(y_ref[...], wo_ref[...])
    r = a * lax.rsqrt(jnp.mean(a * a, axis=-1, keepdims=True) + RMS_EPS) * gpost_ref[...]
    x1 = x_ref[...] + r
    x1_ref[...] = x1
    h2 = x1 * lax.rsqrt(jnp.mean(x1 * x1, axis=-1, keepdims=True) + RMS_EPS) * gpre_ref[...]
    h2_ref[...] = h2.astype(h2_ref.dtype)


def proj_norm(y, wo, x, g_post, g_pre, tm=256):
    m, d = x.shape
    tm = min(tm, m)
    row = lambda i: (i, 0)
    fix = lambda i: (0, 0)
    return pl.pallas_call(
        _proj_norm_kernel,
        grid=(m // tm,),
        in_specs=[pl.BlockSpec((tm, d), row), pl.BlockSpec((d, d), fix), pl.BlockSpec((tm, d), row),
                  pl.BlockSpec((1, d), fix), pl.BlockSpec((1, d), fix)],
        out_specs=[pl.BlockSpec((tm, d), row), pl.BlockSpec((tm, d), row)],
        out_shape=[jax.ShapeDtypeStruct((m, d), F32), jax.ShapeDtypeStruct((m, d), BF16)],
        compiler_params=_cparams(1),
        name="proj_norm",
    )(y, wo, x, g_post.reshape(1, d), g_pre.reshape(1, d))


def _mlp_kernel(h_ref, wu_ref, wd_ref, x_ref, g_ref, o_ref, acc_ref):
    k = pl.program_id(1)

    @pl.when(k == 0)
    def _():
        acc_ref[...] = jnp.zeros_like(acc_ref)

    u = jnp.maximum(_dot(h_ref[...], wu_ref[...]), 0.0)
    acc_ref[...] += _dot((u * u).astype(BF16), wd_ref[...])

    @pl.when(k == pl.num_programs(1) - 1)
    def _():
        a = acc_ref[...]
        o_ref[...] = x_ref[...] + a * lax.rsqrt(jnp.mean(a * a, axis=-1, keepdims=True) + RMS_EPS) * g_ref[...]


def mlp(h2, w_up, w_down, x1, g_post, tm, tk=1024):
    m, d = x1.shape
    ff = w_up.shape[1]
    return pl.pallas_call(
        _mlp_kernel,
        grid=(m // tm, ff // tk),
        in_specs=[pl.BlockSpec((tm, d), lambda i, k: (i, 0)),
                  pl.BlockSpec((d, tk), lambda i, k: (0, k)),
                  pl.BlockSpec((tk, d), lambda i, k: (k, 0)),
                  pl.BlockSpec((tm, d), lambda i, k: (i, 0)),
                  pl.BlockSpec((1, d), lambda i, k: (0, 0))],
        out_specs=pl.BlockSpec((tm, d), lambda i, k: (i, 0)),
        out_shape=jax.ShapeDtypeStruct((m, d), F32),
        scratch_shapes=[pltpu.VMEM((tm, d), F32)],
        compiler_params=_cparams(2),
        name="mlp",
    )(h2, w_up, w_down, x1, g_post.reshape(1, d))


QB = LANES
KC = 512
SCORE_KC = 512
AQ = 256


def _dsa_mask_prompt_kernel(qi_ref, sm_all_ref, sm_blk_ref, mask_ref, kia_ref, kib_ref, keys_ref, mstar_ref, qq_ref,
                            *, seq, top_k):
    i = pl.program_id(1)
    n_chunks = (i * QB) // KC + 1

    @pl.when(i == 0)
    def _():
        sm = sm_all_ref[...]
        lane = lax.broadcasted_iota(I32, sm.shape, 1)
        kia_ref[...] = jnp.where(lane < IDX_DIM, sm, 0.0).astype(BF16)
        kib_ref[...] = jnp.where(lane >= IDX_DIM, pltpu.roll(sm, IDX_DIM, axis=1), 0.0).astype(BF16)

    w_t = sm_blk_ref[...].T * IDX_W_SCALE
    qi = qi_ref[...].astype(BF16)
    for g in range(N_IDX_HEADS // 4):
        qq_ref[g] = jnp.concatenate([qi[:, (2 * g) * LANES:(2 * g + 1) * LANES],
                                     qi[:, (2 * g + 1) * LANES:(2 * g + 2) * LANES]], axis=0)
    q_pos_s = i * QB + lax.broadcasted_iota(I32, (SCORE_KC, QB), 1)
    row_s = lax.broadcasted_iota(I32, (SCORE_KC, QB), 0)

    def score_chunk(c, carry):
        r0 = pl.multiple_of(c * SCORE_KC, SCORE_KC)
        ka = kia_ref[pl.ds(r0, SCORE_KC), :]
        kb = kib_ref[pl.ds(r0, SCORE_KC), :]
        acc = jnp.zeros((SCORE_KC, QB), F32)
        for g in range(N_IDX_HEADS // 4):
            qq = qq_ref[g]
            for odd, k_half in ((0, ka), (1, kb)):
                s = jnp.maximum(_dot_nt(k_half, qq), 0.0)
                for j in range(2):
                    head = 4 * g + 2 * j + odd
                    acc = acc + s[:, j * LANES:(j + 1) * LANES] * w_t[IDX_DIM + head:IDX_DIM + head + 1, :]
        sc = jnp.where(c * SCORE_KC + row_s <= q_pos_s, acc, NEG_INF)
        keys_ref[pl.ds(r0, SCORE_KC), :] = _float_key(sc)
        return carry

    lax.fori_loop(0, n_chunks * (KC // SCORE_KC), score_chunk, 0)

    n_tail = seq - n_chunks * KC
    q_pos = i * QB + lax.broadcasted_iota(I32, (KC, QB), 1)
    row = lax.broadcasted_iota(I32, (KC, QB), 0)

    def count(pred_fn):
        def body(c, acc):
            k = keys_ref[pl.ds(pl.multiple_of(c * KC, KC), KC), :]
            ones = jnp.where(pred_fn(k, c), 1, 0)
            return acc + jnp.sum(ones.reshape(KC // SUBLANES, SUBLANES, QB), axis=0)

        acc = lax.fori_loop(0, n_chunks, body, jnp.zeros((SUBLANES, QB), I32))
        return jnp.sum(acc, axis=0, keepdims=True)

    def count_ge(cand):
        return count(lambda k, c: k >= cand) + jnp.where(KEY_NEG_INF >= cand, n_tail, 0)

    base = jnp.where(count_ge(jnp.zeros((1, QB), I32)) >= top_k, 0, INT_MIN).astype(I32)

    def bit_body(t, base):
        cand = base | lax.shift_left(jnp.int32(1), 30 - t)
        return jnp.where(count_ge(cand) >= top_k, cand, base)

    v = lax.fori_loop(0, 31, bit_body, base)

    n_gt = count(lambda k, c: k > v) + jnp.where(KEY_NEG_INF > v, n_tail, 0)
    need = top_k - n_gt
    n_eq = count(lambda k, c: k == v)
    mstar_ref[...] = jnp.full((SUBLANES, QB), seq, I32)

    @pl.when(jnp.max(jnp.where(n_eq > need, 1, 0)) > 0)
    def _():
        def idx_bit(t, m):
            cand = m | lax.shift_left(jnp.int32(1), (seq - 1).bit_length() - 1 - t)
            below = count(lambda k, c: (k == v) & (c * KC + row < cand))
            return jnp.where(below < need, cand, m)

        m = lax.fori_loop(0, (seq - 1).bit_length(), idx_bit, jnp.zeros((1, QB), I32))
        mstar_ref[...] = jnp.broadcast_to(m, (SUBLANES, QB))

    mstar = mstar_ref[0:1, :]

    def emit(c, carry):
        k = keys_ref[pl.ds(pl.multiple_of(c * KC, KC), KC), :]
        k_pos = c * KC + row
        sel = (k > v) | ((k == v) & (k_pos <= mstar))
        m_t = jnp.where(sel & (k_pos <= q_pos), 0.0, NEG_INF)
        mask_ref[c] = m_t.T.astype(mask_ref.dtype)
        return carry

    lax.fori_loop(0, n_chunks, emit, 0)

    def clear(c, carry):
        mask_ref[c] = jnp.full((QB, KC), NEG_INF, mask_ref.dtype)
        return carry

    lax.fori_loop(n_chunks, seq // KC, clear, 0)


def dsa_mask_prompt(proj, batch, seq, c_small):
    assert seq % KC == 0 and KC % AQ == 0 and AQ % QB == 0
    nq = seq // QB
    per = AQ // QB
    top_k = min(TOP_K_MAX, seq // 4)
    return pl.pallas_call(
        functools.partial(_dsa_mask_prompt_kernel, seq=seq, top_k=top_k),
        grid=(batch, nq),
        in_specs=[pl.BlockSpec((QB, N_IDX_HEADS * IDX_DIM), lambda b, i: (b * nq + i, C_QI // (N_IDX_HEADS * IDX_DIM))),
                  pl.BlockSpec((seq, LANES), lambda b, i: (b, c_small // LANES)),
                  pl.BlockSpec((QB, LANES), lambda b, i: (b * nq + i, c_small // LANES))],
        out_specs=pl.BlockSpec((None, seq // KC, None, QB, KC), lambda b, i: ((b * nq + i) // per, 0, i % per, 0, 0)),
        out_shape=jax.ShapeDtypeStruct((batch * seq // AQ, seq // KC, per, QB, KC), BF16),
        scratch_shapes=[pltpu.VMEM((seq, LANES), BF16), pltpu.VMEM((seq, LANES), BF16),
                        pltpu.VMEM((seq, QB), I32), pltpu.VMEM((SUBLANES, QB), I32),
                        pltpu.VMEM((N_IDX_HEADS // 4, 2 * QB, LANES), BF16)],
        compiler_params=_cparams(2),
        name="dsa_mask_prompt",
    )(proj, proj, proj)


def _bucket_thresholds():
    d = np.arange(4 * REL_MAX_DISTANCE)
    b = _rel_bucket_np(d)
    assert np.all(np.diff(b) >= 0) and b[-1] == N_REL_BUCKETS - 1
    return [int(np.argmax(b > j)) for j in range(N_REL_BUCKETS - 1)]


def _bias_of_distance(rb_ref, h, d, thresholds):
    acc = jnp.full(d.shape, rb_ref[N_REL_BUCKETS - 1, h], F32)
    for j in range(N_REL_BUCKETS - 2, -1, -1):
        acc = jnp.where(d < thresholds[j], rb_ref[j, h], acc)
    return acc


def _lane_fold(x, op):
    out = x[:, :LANES]
    for j in range(1, x.shape[1] // LANES):
        out = op(out, x[:, j * LANES:(j + 1) * LANES])
    return out


def _dsa_attn_prompt_kernel(rb_ref, q_ref, k_ref, v_ref, mask_ref, o_ref, tab_ref, lg_ref, kb_ref, vb_ref, acc_ref,
                            l_ref, *, thresholds, unroll):
    h = pl.program_id(1)
    i = pl.program_id(2)
    per = KC // AQ
    part = i % per
    c_diag = i // per

    @pl.when(i == 0)
    def _():
        kb_ref[...] = k_ref[...].astype(BF16)
        vb_ref[...] = v_ref[...].astype(BF16)
        row = lax.broadcasted_iota(I32, (AQ, KC), 0)
        col = lax.broadcasted_iota(I32, (AQ, KC), 1)
        for p in range(per):
            for back in range(2):
                tab_ref[3 * p + back] = LOG2E * _bias_of_distance(rb_ref, h, p * AQ + back * KC + row - col,
                                                                  thresholds)
            tab_ref[3 * p + 2] = jnp.full((AQ, KC), LOG2E * rb_ref[N_REL_BUCKETS - 1, h], F32)

    q = q_ref[...].astype(BF16)

    n_trips = c_diag // unroll + 1

    def logits_chunks(t, mx):
        for u in range(unroll):
            c = t * unroll + u
            kc = kb_ref[pl.ds(pl.multiple_of(c * KC, KC), KC), :]
            s = _dot_nt(q, kc) * (ATTN_SCALE * LOG2E) + tab_ref[3 * part + jnp.clip(c_diag - c, 0, 2)]
            s = s + mask_ref[c].reshape(AQ, KC).astype(F32)
            lg_ref[c] = s
            mx = jnp.maximum(mx, _lane_fold(s, jnp.maximum))
        return mx

    mx = lax.fori_loop(0, n_trips, logits_chunks, jnp.full((AQ, LANES), NEG_INF, F32))
    m = jnp.max(mx, axis=1, keepdims=True)
    acc_ref[...] = jnp.zeros_like(acc_ref)
    l_ref[...] = jnp.zeros_like(l_ref)

    def pv_chunks(t, carry):
        for u in range(unroll):
            c = t * unroll + u
            p = jnp.exp2(lg_ref[c] - m)
            l_ref[...] += _lane_fold(p, jnp.add)
            acc_ref[...] += _dot(p.astype(BF16), vb_ref[pl.ds(pl.multiple_of(c * KC, KC), KC), :])
        return carry

    lax.fori_loop(0, n_trips, pv_chunks, 0)
    o_ref[...] = (acc_ref[...] / jnp.sum(l_ref[...], axis=1, keepdims=True)).astype(o_ref.dtype)


def dsa_attn_prompt(proj, mask, rel_bias, batch, seq):
    nq = seq // AQ
    nk = seq // KC
    hd = HEAD_DIM
    thresholds = _bucket_thresholds()
    assert thresholds[-1] <= KC + 1
    return pl.pallas_call(
        functools.partial(_dsa_attn_prompt_kernel, thresholds=thresholds, unroll=2 if nk % 2 == 0 else 1),
        grid=(batch, N_HEADS, nq),
        in_specs=[pl.BlockSpec(memory_space=pltpu.SMEM),
                  pl.BlockSpec((AQ, hd), lambda b, h, i: (b * nq + i, C_QD // hd + h)),
                  pl.BlockSpec((seq, hd), lambda b, h, i: (b, C_KD // hd + h)),
                  pl.BlockSpec((seq, hd), lambda b, h, i: (b, C_VD // hd + h)),
                  pl.BlockSpec((None, nk, AQ // QB, QB, KC), lambda b, h, i: (b * nq + i, 0, 0, 0, 0))],
        out_specs=pl.BlockSpec((AQ, hd), lambda b, h, i: (b * nq + i, h)),
        out_shape=jax.ShapeDtypeStruct((batch * seq, D_ATT), BF16),
        scratch_shapes=[pltpu.VMEM((3 * (KC // AQ), AQ, KC), F32), pltpu.VMEM((nk, AQ, KC), F32),
                        pltpu.VMEM((seq, hd), BF16), pltpu.VMEM((seq, hd), BF16),
                        pltpu.VMEM((AQ, hd), F32), pltpu.VMEM((AQ, LANES), F32)],
        compiler_params=_cparams(3),
        name="dsa_attn_prompt",
    )(rel_bias, proj, proj, proj, mask)


SB_T = 256
SB_HEADS = 4
EXP_ZERO_BELOW = -104.0


def _suffix_sum_matrix(n):
    s = np.arange(n)
    tri = (s[:, None] > s[None, :]).astype(np.float32)
    return np.concatenate([tri, np.ones((n, n), np.float32)], axis=1)


def _sb_prompt_kernel(q_ref, k_ref, v_ref, r_ref, o_ref, kb_ref, vb_ref, acc_ref, run_ref):
    i = pl.program_id(2)

    @pl.when(i == 0)
    def _():
        kb_ref[...] = k_ref[...].astype(BF16)
        vb_ref[...] = v_ref[...].astype(BF16)

    q = q_ref[...].astype(BF16)
    acc_ref[...] = jnp.zeros_like(acc_ref)
    run_ref[...] = jnp.zeros_like(run_ref)
    row = lax.broadcasted_iota(I32, (SB_T, SB_T), 0)
    col = lax.broadcasted_iota(I32, (SB_T, SB_T), 1)

    def chunk(carry):
        step, _ = carry
        c = i - step
        r0 = pl.multiple_of(c * SB_T, SB_T)
        before = col < row + step * SB_T
        live = None
        for hh in range(SB_HEADS):
            cols = slice(hh * HEAD_DIM, (hh + 1) * HEAD_DIM)
            z = _dot_nt(q[:, cols], kb_ref[pl.ds(r0, SB_T), cols]) * ATTN_SCALE
            sp = _softplus(z)
            log_keep = jnp.where(before, -sp, 0.0)
            cs = _split_dot(log_keep, r_ref[...], 2)
            later = cs[:, :SB_T] + run_ref[hh]
            a = jnp.where(before, jnp.exp(z - sp + later), 0.0)
            acc_ref[:, cols] += _dot(a.astype(BF16), vb_ref[pl.ds(r0, SB_T), cols])
            run = run_ref[hh] + cs[:, SB_T:]
            run_ref[hh] = run
            top = jnp.max(run[:, :LANES])
            live = top if live is None else jnp.maximum(live, top)
        return step + 1, (live >= EXP_ZERO_BELOW).astype(I32)

    lax.while_loop(lambda carry: (carry[0] <= i) & (carry[1] > 0), chunk, (jnp.int32(0), jnp.int32(1)))
    o_ref[...] = acc_ref[...].astype(o_ref.dtype)


def sb_prompt(proj, batch, seq):
    nq = seq // SB_T
    w = SB_HEADS * HEAD_DIM
    r = jnp.asarray(_suffix_sum_matrix(SB_T), BF16)
    return pl.pallas_call(
        _sb_prompt_kernel,
        grid=(batch, N_HEADS // SB_HEADS, nq),
        in_specs=[pl.BlockSpec((SB_T, w), lambda b, h, i: (b * nq + i, C_QS // w + h)),
                  pl.BlockSpec((seq, w), lambda b, h, i: (b, C_KS // w + h)),
                  pl.BlockSpec((seq, w), lambda b, h, i: (b, C_VS // w + h)),
                  pl.BlockSpec((SB_T, 2 * SB_T), lambda b, h, i: (0, 0))],
        out_specs=pl.BlockSpec((SB_T, w), lambda b, h, i: (b * nq + i, h)),
        out_shape=jax.ShapeDtypeStruct((batch * seq, D_ATT), BF16),
        scratch_shapes=[pltpu.VMEM((seq, w), BF16), pltpu.VMEM((seq, w), BF16),
                        pltpu.VMEM((SB_T, w), F32), pltpu.VMEM((SB_HEADS, SB_T, SB_T), F32)],
        compiler_params=_cparams(3),
        name="sb_prompt",
    )(proj, proj, proj, r)


TOK_PAD = SUBLANES


def _pad_rows(x, rows):
    return jnp.concatenate([x, jnp.zeros((rows - x.shape[0], x.shape[1]), x.dtype)], axis=0)


def _head_rows(page_ref, h, page_size):
    return page_ref[pl.ds(h, page_size, stride=N_HEADS), :].astype(BF16)


def _page_specs(n_pages, rows, cols):
    return [pl.BlockSpec((None, rows, cols), functools.partial(lambda b, pt, p: (pt[b, p], 0, 0), p=p))
            for p in range(n_pages)]


def _sb_sample_kernel(pt_ref, q_ref, kn_ref, vn_ref, r_ref, k_hbm, v_hbm, o_ref, kbuf, vbuf, sems, run_ref, acc_ref,
                      *, n_pages, page_size, n_new, n_pre):
    b = pl.program_id(0)
    grp = N_HEADS * TOK_PAD
    spare = 2 * n_pre

    def page_copies(seq, page, slot):
        phys = pt_ref[seq, page]
        return (pltpu.make_async_copy(k_hbm.at[phys], kbuf.at[slot], sems.at[0, slot]),
                pltpu.make_async_copy(v_hbm.at[phys], vbuf.at[slot], sems.at[1, slot]))

    def prefetch(seq, action):
        for j in range(n_pre):
            for cp in page_copies(seq, n_pages - 1 - j, (seq % 2) * n_pre + j):
                action(cp)

    @pl.when(b == 0)
    def _():
        prefetch(b, lambda cp: cp.start())

    @pl.when(b + 1 < pl.num_programs(0))
    def _():
        prefetch(b + 1, lambda cp: cp.start())

    q = [q_ref[:, h * HEAD_DIM:(h + 1) * HEAD_DIM].astype(BF16) for h in range(N_HEADS)]
    row = lax.broadcasted_iota(I32, (grp, page_size), 0) % TOK_PAD
    lane = lax.broadcasted_iota(I32, (grp, page_size), 1)
    run_ref[...] = jnp.zeros_like(run_ref)
    acc_ref[...] = jnp.zeros_like(acc_ref)

    def process(k_h, v_h, before):
        z = jnp.concatenate([_dot_nt(q[h], k_h[h]) for h in range(N_HEADS)], axis=0) * ATTN_SCALE
        sp = _softplus(z)
        log_keep = -sp if before is None else jnp.where(before, -sp, 0.0)
        cs = _split_dot(log_keep, r_ref[...], 3)
        run = run_ref[...]
        e = jnp.exp(z - sp + cs[:, :page_size] + run)
        a = e if before is None else jnp.where(before, e, 0.0)
        run = run + cs[:, page_size:]
        run_ref[...] = run
        for h in range(N_HEADS):
            rows = slice(h * TOK_PAD, (h + 1) * TOK_PAD)
            acc_ref[rows, :] += _dot(a[rows, :].astype(BF16), v_h[h])
        return (jnp.max(jnp.where(row < n_new, run, NEG_INF)) >= EXP_ZERO_BELOW).astype(I32)

    def process_slot(slot):
        k_h = [kbuf[slot, pl.ds(h, page_size, stride=N_HEADS), :].astype(BF16) for h in range(N_HEADS)]
        v_h = [vbuf[slot, pl.ds(h, page_size, stride=N_HEADS), :].astype(BF16) for h in range(N_HEADS)]
        return process(k_h, v_h, None)

    process([_pad_rows(kn_ref[:, h * HEAD_DIM:(h + 1) * HEAD_DIM], page_size).astype(BF16) for h in range(N_HEADS)],
            [_pad_rows(vn_ref[:, h * HEAD_DIM:(h + 1) * HEAD_DIM], page_size).astype(BF16) for h in range(N_HEADS)],
            lane < row)
    prefetch(b, lambda cp: cp.wait())
    live = jnp.int32(1)
    for j in range(n_pre):
        live = lax.cond(live > 0, functools.partial(process_slot, (b % 2) * n_pre + j), lambda: jnp.int32(0))

    def fetch_and_process(carry):
        page, _ = carry
        for cp in page_copies(b, page, spare):
            cp.start()
        for cp in page_copies(b, page, spare):
            cp.wait()
        return page - 1, process_slot(spare)

    lax.while_loop(lambda carry: (carry[0] >= 0) & (carry[1] > 0), fetch_and_process,
                   (jnp.int32(n_pages - n_pre - 1), live))
    for h in range(N_HEADS):
        o_ref[:, h * HEAD_DIM:(h + 1) * HEAD_DIM] = acc_ref[h * TOK_PAD:(h + 1) * TOK_PAD, :]


def sb_sample(proj8, k_pool, v_pool, page_table, n_new):
    n_seq = proj8.shape[0]
    n_pages = page_table.shape[1]
    page_size = k_pool.shape[1] // N_HEADS
    assert page_size == LANES
    n_pre = min(2, n_pages)
    r = jnp.asarray(_suffix_sum_matrix(page_size), BF16)
    tok = lambda col: pl.BlockSpec((None, TOK_PAD, D_ATT), lambda b, pt: (b, 0, col // D_ATT))
    page_buf = pltpu.VMEM((2 * n_pre + 1, page_size * N_HEADS, HEAD_DIM), F32)
    grid_spec = pltpu.PrefetchScalarGridSpec(
        num_scalar_prefetch=1,
        grid=(n_seq,),
        in_specs=[tok(C_QS), tok(C_KS), tok(C_VS), pl.BlockSpec((page_size, 2 * page_size), lambda b, pt: (0, 0)),
                  pl.BlockSpec(memory_space=pl.ANY), pl.BlockSpec(memory_space=pl.ANY)],
        out_specs=pl.BlockSpec((None, TOK_PAD, D_ATT), lambda b, pt: (b, 0, 0)),
        scratch_shapes=[page_buf, page_buf, pltpu.SemaphoreType.DMA((2, 2 * n_pre + 1)),
                        pltpu.VMEM((N_HEADS * TOK_PAD, page_size), F32), pltpu.VMEM((N_HEADS * TOK_PAD, HEAD_DIM), F32)],
    )
    return pl.pallas_call(
        functools.partial(_sb_sample_kernel, n_pages=n_pages, page_size=page_size, n_new=n_new, n_pre=n_pre),
        grid_spec=grid_spec,
        out_shape=jax.ShapeDtypeStruct((n_seq, TOK_PAD, D_ATT), F32),
        compiler_params=_cparams(1),
        name="sb_sample",
    )(page_table, proj8, proj8, proj8, r, k_pool, v_pool)


def _dsa_scores_sample_kernel(pt_ref, qi_ref, w_ref, kin_ref, *rest, n_pages, page_size, n_new):
    ki_pages = rest[:n_pages]
    o_ref = rest[n_pages]
    qi = qi_ref[...].astype(BF16)
    w = w_ref[...] * IDX_W_SCALE
    o_ref[...] = jnp.zeros_like(o_ref)
    for g in range(n_pages + 1):
        ki_t = (kin_ref if g == n_pages else ki_pages[g])[...].astype(BF16)
        s = jnp.maximum(_dot(qi, ki_t), 0.0) * w
        s = jnp.sum(s.reshape(n_new, N_IDX_HEADS, page_size), axis=1)
        o_ref[0:n_new, g * page_size:(g + 1) * page_size] = s


def dsa_scores_sample(qi3, w_rep, ki_new_t, ki_pool_t, page_table):
    n_seq = qi3.shape[0]
    n_new = qi3.shape[1] // N_IDX_HEADS
    n_pages = page_table.shape[1]
    page_size = ki_pool_t.shape[2]
    assert page_size == LANES
    seq_blk = lambda r, c: pl.BlockSpec((None, r, c), lambda b, pt: (b, 0, 0))
    grid_spec = pltpu.PrefetchScalarGridSpec(
        num_scalar_prefetch=1,
        grid=(n_seq,),
        in_specs=[seq_blk(n_new * N_IDX_HEADS, IDX_DIM), seq_blk(n_new * N_IDX_HEADS, LANES),
                  seq_blk(IDX_DIM, page_size)] + _page_specs(n_pages, IDX_DIM, page_size),
        out_specs=seq_blk(TOK_PAD, (n_pages + 1) * page_size),
    )
    return pl.pallas_call(
        functools.partial(_dsa_scores_sample_kernel, n_pages=n_pages, page_size=page_size, n_new=n_new),
        grid_spec=grid_spec,
        out_shape=jax.ShapeDtypeStruct((n_seq, TOK_PAD, (n_pages + 1) * page_size), F32),
        compiler_params=_cparams(1),
        name="dsa_scores_sample",
    )(page_table, qi3, w_rep, ki_new_t, *([ki_pool_t] * n_pages))


MASK_ROWS = 128


def _dsa_mask_sample_kernel(s_ref, ones_ref, m_ref, *, n_past, n_new, top_k):
    n_chunks = s_ref.shape[1] // LANES
    shape = (MASK_ROWS, LANES)
    tok = lax.broadcasted_iota(I32, shape, 0) % TOK_PAD
    lane = lax.broadcasted_iota(I32, shape, 1)
    ones = ones_ref[...]
    keys, pos, exists = [], [], []
    for c in range(n_chunks):
        s = s_ref[:, c * LANES:(c + 1) * LANES]
        k_pos = c * LANES + lane
        s = jnp.where(k_pos <= n_past + tok, s, NEG_INF)
        keys.append(_float_key(s))
        pos.append(k_pos)
        exists.append(None if (c + 1) * LANES <= n_past + n_new else k_pos < n_past + n_new)

    def count(pred_fn):
        part = jnp.zeros(shape, F32)
        for c in range(n_chunks):
            p = pred_fn(keys[c], pos[c])
            if exists[c] is not None:
                p = p & exists[c]
            part = part + jnp.where(p, 1.0, 0.0)
        return _dot(part.astype(BF16), ones)

    base = jnp.where(count(lambda k, p: k >= 0) >= top_k, 0, INT_MIN).astype(I32)

    def bit_body(t, base):
        cand = base | lax.shift_left(jnp.int32(1), 30 - t)
        return jnp.where(count(lambda k, p: k >= cand) >= top_k, cand, base)

    v = lax.fori_loop(0, 31, bit_body, base)
    need = top_k - count(lambda k, p: k > v)

    n_bits = (n_chunks * LANES - 1).bit_length()

    def idx_bit(t, m):
        cand = m | lax.shift_left(jnp.int32(1), n_bits - 1 - t)
        below = count(lambda k, p: (k == v) & (p < cand))
        return jnp.where(below < need, cand, m)

    mstar = lax.fori_loop(0, n_bits, idx_bit, jnp.zeros(shape, I32))
    for c in range(n_chunks):
        sel = (keys[c] > v) | ((keys[c] == v) & (pos[c] <= mstar))
        sel = sel & (pos[c] <= n_past + tok) & (tok < n_new)
        if exists[c] is not None:
            sel = sel & exists[c]
        m_ref[:, c * LANES:(c + 1) * LANES] = jnp.where(sel, 1.0, 0.0)


def dsa_mask_sample(scores2, n_past, n_new):
    rows, width = scores2.shape
    top_k = min(TOP_K_MAX, (n_past + n_new) // 4)
    return pl.pallas_call(
        functools.partial(_dsa_mask_sample_kernel, n_past=n_past, n_new=n_new, top_k=top_k),
        grid=(rows // MASK_ROWS,),
        in_specs=[pl.BlockSpec((MASK_ROWS, width), lambda i: (i, 0)), pl.BlockSpec((LANES, LANES), lambda i: (0, 0))],
        out_specs=pl.BlockSpec((MASK_ROWS, width), lambda i: (i, 0)),
        out_shape=jax.ShapeDtypeStruct((rows, width), F32),
        compiler_params=_cparams(1),
        name="dsa_mask_sample",
    )(scores2, jnp.ones((LANES, LANES), BF16))


def _sample_bucket_tables(n_new, page_size):
    j = (np.arange(N_HEADS * TOK_PAD) % TOK_PAD)[:, None]
    u = np.arange(page_size)[None, :]
    assert _rel_bucket_np(np.array([page_size + 1]))[0] == N_REL_BUCKETS - 1
    return np.stack([_rel_bucket_np(page_size + j - u), _rel_bucket_np(j - u)]).astype(np.int32)


def _dsa_attn_sample_kernel(pt_ref, rb_ref, q_ref, kn_ref, vn_ref, m_ref, bkt_ref, *rest, n_pages, page_size, n_new):
    k_pages = rest[:n_pages]
    v_pages = rest[n_pages:2 * n_pages]
    o_ref = rest[2 * n_pages]
    tab_ref = rest[2 * n_pages + 1]
    lg_ref = rest[2 * n_pages + 2]
    grp = N_HEADS * TOK_PAD

    @pl.when(pl.program_id(0) == 0)
    def _():
        for h in range(N_HEADS):
            rows = slice(h * TOK_PAD, (h + 1) * TOK_PAD)
            for t in range(2):
                bk = bkt_ref[t, rows, :]
                acc = jnp.zeros(bk.shape, F32)
                for b in range(N_REL_BUCKETS):
                    acc = jnp.where(bk == b, rb_ref[b, h], acc)
                tab_ref[t, rows, :] = acc
            tab_ref[2, rows, :] = jnp.full((TOK_PAD, page_size), rb_ref[N_REL_BUCKETS - 1, h], F32)

    q = [q_ref[:, h * HEAD_DIM:(h + 1) * HEAD_DIM].astype(BF16) for h in range(N_HEADS)]
    mx = jnp.full((grp, page_size), NEG_INF, F32)
    for g in range(n_pages + 1):
        if g == n_pages:
            k_h = [_pad_rows(kn_ref[:, h * HEAD_DIM:(h + 1) * HEAD_DIM], page_size).astype(BF16) for h in range(N_HEADS)]
            bias = tab_ref[1]
        else:
            k_h = [_head_rows(k_pages[g], h, page_size) for h in range(N_HEADS)]
            bias = tab_ref[0] if g == n_pages - 1 else tab_ref[2]
        s = jnp.concatenate([_dot_nt(q[h], k_h[h]) for h in range(N_HEADS)], axis=0) * ATTN_SCALE + bias
        sel = jnp.tile(m_ref[:, g * page_size:(g + 1) * page_size], (N_HEADS, 1))
        s = jnp.where(sel > 0.5, s, NEG_INF)
        lg_ref[g] = s
        mx = jnp.maximum(mx, s)
    m = jnp.max(mx, axis=1, keepdims=True)
    l = jnp.zeros((grp, page_size), F32)
    acc = [jnp.zeros((TOK_PAD, HEAD_DIM), F32) for _ in range(N_HEADS)]
    for g in range(n_pages + 1):
        p = jnp.exp(lg_ref[g] - m)
        l = l + p
        for h in range(N_HEADS):
            if g == n_pages:
                v_h = _pad_rows(vn_ref[:, h * HEAD_DIM:(h + 1) * HEAD_DIM], page_size).astype(BF16)
            else:
                v_h = _head_rows(v_pages[g], h, page_size)
            acc[h] = acc[h] + _dot(p[h * TOK_PAD:(h + 1) * TOK_PAD, :].astype(BF16), v_h)
    inv = 1.0 / jnp.sum(l, axis=1, keepdims=True)
    for h in range(N_HEADS):
        o_ref[:, h * HEAD_DIM:(h + 1) * HEAD_DIM] = acc[h] * inv[h * TOK_PAD:(h + 1) * TOK_PAD, :]


def dsa_attn_sample(proj8, mask3, rel_bias, k_pool, v_pool, page_table, n_new):
    n_seq = proj8.shape[0]
    n_pages = page_table.shape[1]
    page_size = k_pool.shape[1] // N_HEADS
    assert page_size == LANES
    grp = N_HEADS * TOK_PAD
    width = (n_pages + 1) * page_size
    tok = lambda col: pl.BlockSpec((None, TOK_PAD, D_ATT), lambda b, pt: (b, 0, col // D_ATT))
    grid_spec = pltpu.PrefetchScalarGridSpec(
        num_scalar_prefetch=1,
        grid=(n_seq,),
        in_specs=[pl.BlockSpec(memory_space=pltpu.SMEM), tok(C_QD), tok(C_KD), tok(C_VD),
                  pl.BlockSpec((None, TOK_PAD, width), lambda b, pt: (b, 0, 0)),
                  pl.BlockSpec((2, grp, page_size), lambda b, pt: (0, 0, 0))]
                 + _page_specs(n_pages, page_size * N_HEADS, HEAD_DIM)
                 + _page_specs(n_pages, page_size * N_HEADS, HEAD_DIM),
        out_specs=pl.BlockSpec((None, TOK_PAD, D_ATT), lambda b, pt: (b, 0, 0)),
        scratch_shapes=[pltpu.VMEM((3, grp, page_size), F32), pltpu.VMEM((n_pages + 1, grp, page_size), F32)],
    )
    return pl.pallas_call(
        functools.partial(_dsa_attn_sample_kernel, n_pages=n_pages, page_size=page_size, n_new=n_new),
        grid_spec=grid_spec,
        out_shape=jax.ShapeDtypeStruct((n_seq, TOK_PAD, D_ATT), F32),
        compiler_params=_cparams(1, vmem_mb=56),
        name="dsa_attn_sample",
    )(page_table, rel_bias, proj8, proj8, proj8, mask3, jnp.asarray(_sample_bucket_tables(n_new, page_size)),
      *([k_pool] * n_pages), *([v_pool] * n_pages))


def _tail(x, proj, o_a, o_b, weights, gains):
    w1, w2, wo, wu, wd = weights
    g_attn_post, g_mlp_pre, g_mlp_post = gains
    tm = min(512, x.shape[0])
    y = merge_branches(o_a, o_b, w1, w2, proj, tm)
    x1, h2 = proj_norm(y, wo, x, g_attn_post, g_mlp_pre)
    return mlp(h2, wu, wd, x1, g_mlp_post, tm)


def kernel(x_prompt, x_sample, cache_k_dsa, cache_v_dsa, cache_k_idx, cache_k_sb, cache_v_sb, page_table, rel_bias,
           w_in, w_out_dsa, w_out_sb, w_o, w_up, w_down, g_attn_pre, g_attn_post, g_mlp_pre, g_mlp_post):
    batch, seq, d_model = x_prompt.shape
    n_seq, n_new, _ = x_sample.shape
    depth = w_in.shape[0]
    n_pool, page_size = cache_k_idx.shape[1], cache_k_idx.shape[2]
    n_past = page_table.shape[1] * page_size

    xp = x_prompt.reshape(batch * seq, d_model)
    xs = jnp.pad(x_sample, ((0, 0), (0, TOK_PAD - n_new), (0, 0))).reshape(n_seq * TOK_PAD, d_model)
    rows_p, rows_s = [], []
    for l in range(depth):
        w_in_t = jnp.swapaxes(w_in[l], 0, 1)
        weights = tuple(w[l].astype(BF16) for w in (w_out_dsa, w_out_sb, w_o, w_up, w_down))
        gains = (g_attn_post[l], g_mlp_pre[l], g_mlp_post[l])

        hp = rms_cast(xp, g_attn_pre[l])
        proj, c_small = in_proj(hp, w_in_t, tm=min(1024, hp.shape[0]))
        mask = dsa_mask_prompt(proj, batch, seq, c_small)
        o_a = dsa_attn_prompt(proj, mask, rel_bias, batch, seq)
        o_b = sb_prompt(proj, batch, seq)
        xp_new = _tail(xp, proj, o_a, o_b, weights, gains)
        heads = lambda c: proj[:, c:c + D_ATT].reshape(batch, seq, N_HEADS, HEAD_DIM)
        rows_p.append((heads(C_KD), heads(C_VD), proj[:, c_small:c_small + IDX_DIM].reshape(batch, seq, IDX_DIM),
                       heads(C_KS), heads(C_VS)))
        xp = xp_new

        hs = rms_cast(xs, g_attn_pre[l])
        proj_s, _ = in_proj(hs, w_in_t, tm=min(1024, hs.shape[0]))
        proj8 = proj_s.reshape(n_seq, TOK_PAD, proj_s.shape[1])
        real = lambda c, w: proj8[:, :n_new, c:c + w]
        qi3 = real(C_QI, N_IDX_HEADS * IDX_DIM).reshape(n_seq, n_new * N_IDX_HEADS, IDX_DIM)
        w_rep = jnp.broadcast_to(real(c_small + IDX_DIM, N_IDX_HEADS).reshape(n_seq, n_new * N_IDX_HEADS, 1),
                                 (n_seq, n_new * N_IDX_HEADS, LANES))
        ki_new = real(c_small, IDX_DIM)
        ki_new_t = jnp.pad(jnp.swapaxes(ki_new, 1, 2), ((0, 0), (0, 0), (0, page_size - n_new)))
        pool2 = lambda c: c[l].reshape(n_pool, page_size * N_HEADS, HEAD_DIM)
        scores = dsa_scores_sample(qi3, w_rep, ki_new_t, jnp.swapaxes(cache_k_idx[l], 1, 2), page_table)
        mask_s = dsa_mask_sample(scores.reshape(n_seq * TOK_PAD, scores.shape[2]), n_past, n_new)
        o_a = dsa_attn_sample(proj8, mask_s.reshape(scores.shape), rel_bias, pool2(cache_k_dsa), pool2(cache_v_dsa),
                              page_table, n_new)
        o_b = sb_sample(proj8, pool2(cache_k_sb), pool2(cache_v_sb), page_table, n_new)
        flat = lambda o: o.reshape(n_seq * TOK_PAD, D_ATT)
        xs_new = _tail(xs, proj_s, flat(o_a), flat(o_b), weights, gains)
        heads_s = lambda c: real(c, D_ATT).reshape(n_seq, n_new, N_HEADS, HEAD_DIM)
        rows_s.append((heads_s(C_KD), heads_s(C_VD), ki_new, heads_s(C_KS), heads_s(C_VS)))
        xs = xs_new

    outs_p = [jnp.stack(r, axis=0) for r in zip(*rows_p)]
    outs_s = [jnp.stack(r, axis=0) for r in zip(*rows_s)]
    return (xp.reshape(batch, seq, d_model), xs.reshape(n_seq, TOK_PAD, d_model)[:, :n_new], *outs_p, *outs_s)
```

```python
import functools
import math

import numpy as np
import jax
import jax.numpy as jnp
from jax import lax
from jax.experimental import pallas as pl
from jax.experimental.pallas import tpu as pltpu

F32 = jnp.float32
BF16 = jnp.bfloat16
I32 = jnp.int32

HEAD_DIM = 128
N_HEADS = 8
N_IDX_HEADS = 16
IDX_DIM = 64
IDX_W_SCALE = (N_IDX_HEADS * IDX_DIM) ** -0.5
TOP_K_MAX = 256
N_REL_BUCKETS = 32
REL_MAX_DISTANCE = 128
RMS_EPS = 1e-6
NEG_INF = -1e30
ATTN_SCALE = HEAD_DIM ** -0.5
LOG2E = math.log2(math.e)

LANES = 128
SUBLANES = 8
D_ATT = N_HEADS * HEAD_DIM

C_QD, C_KD, C_VD, C_QI, C_QS, C_KS, C_VS = (k * D_ATT for k in range(7))
C_GA = 7 * D_ATT
PROJ_TN = 1024


def _key_of(x):
    b = int(np.float32(x).view(np.int32))
    return b if b >= 0 else b ^ 0x7FFFFFFF


KEY_NEG_INF = _key_of(NEG_INF)
INT_MIN = -(2 ** 31)


def _cparams(n_axes, vmem_mb=48):
    return pltpu.CompilerParams(dimension_semantics=("arbitrary",) * n_axes,
                                vmem_limit_bytes=vmem_mb * 1024 * 1024)


def _dot(a, b):
    return jnp.dot(a, b, preferred_element_type=F32)


def _dot_nt(a, b):
    return lax.dot_general(a, b, (((1,), (1,)), ((), ())), preferred_element_type=F32)


def _split_dot(x, m01, passes):
    out = None
    rem = x
    for p in range(passes):
        piece = rem.astype(BF16)
        d = _dot(piece, m01)
        out = d if out is None else out + d
        if p + 1 < passes:
            rem = rem - piece.astype(F32)
    return out


def _softplus(z):
    return jnp.maximum(z, 0.0) + jnp.log1p(jnp.exp(-jnp.abs(z)))


def _float_key(s):
    s = jnp.where(s == 0.0, 0.0, s)
    bits = lax.bitcast_convert_type(s, I32)
    return jnp.where(bits >= 0, bits, bits ^ 0x7FFFFFFF)


def _rel_bucket_np(dist):
    max_exact = N_REL_BUCKETS // 2
    d = np.maximum(dist, 0)
    df = np.maximum(d, 1).astype(np.float32)
    log_part = np.log(df / np.float32(max_exact)) / np.float32(math.log(REL_MAX_DISTANCE / max_exact))
    large = max_exact + (log_part * np.float32(N_REL_BUCKETS - max_exact)).astype(np.int32)
    return np.where(d < max_exact, d, np.minimum(large, N_REL_BUCKETS - 1)).astype(np.int32)


def _rms_cast_kernel(x_ref, g_ref, o_ref):
    x = x_ref[...]
    ms = jnp.mean(x * x, axis=-1, keepdims=True)
    o_ref[...] = (x * lax.rsqrt(ms + RMS_EPS) * g_ref[...]).astype(o_ref.dtype)


def rms_cast(x, g, tm=256):
    m, d = x.shape
    tm = min(tm, m)
    return pl.pallas_call(
        _rms_cast_kernel,
        grid=(m // tm,),
        in_specs=[pl.BlockSpec((tm, d), lambda i: (i, 0)), pl.BlockSpec((1, d), lambda i: (0, 0))],
        out_specs=pl.BlockSpec((tm, d), lambda i: (i, 0)),
        out_shape=jax.ShapeDtypeStruct((m, d), BF16),
        compiler_params=_cparams(1),
        name="rms_cast",
    )(x, g.reshape(1, d))


def _in_proj_kernel(h_ref, wt_ref, o_ref, wb_ref):
    @pl.when(pl.program_id(1) == 0)
    def _():
        wb_ref[...] = wt_ref[...].astype(BF16)

    o_ref[...] = _dot_nt(h_ref[...], wb_ref[...])


def in_proj(h, w_t, tm):
    m, k = h.shape
    in_cols = w_t.shape[0]
    c_ki = 4 * D_ATT
    c_after = c_ki + IDX_DIM + N_IDX_HEADS
    assert c_ki % PROJ_TN == 0 and (in_cols - c_after) % PROJ_TN == 0 and c_ki + PROJ_TN <= in_cols
    n_a = c_ki // PROJ_TN
    n_b = (in_cols - c_after) // PROJ_TN

    assert c_after % SUBLANES == 0 and PROJ_TN % SUBLANES == 0
    tile8, after8, ki8 = PROJ_TN // SUBLANES, c_after // SUBLANES, c_ki // SUBLANES

    def w_row(j):
        return jnp.where(j < n_a, j * tile8, jnp.where(j < n_a + n_b, after8 + (j - n_a) * tile8, ki8)) * SUBLANES

    n_tiles = n_a + n_b + 1
    proj = pl.pallas_call(
        _in_proj_kernel,
        grid=(n_tiles, m // tm),
        in_specs=[pl.BlockSpec((tm, k), lambda j, i: (i, 0)),
                  pl.BlockSpec((pl.Element(PROJ_TN), pl.Element(k)), lambda j, i: (w_row(j), 0))],
        out_specs=pl.BlockSpec((tm, PROJ_TN), lambda j, i: (i, j)),
        out_shape=jax.ShapeDtypeStruct((m, n_tiles * PROJ_TN), F32),
        scratch_shapes=[pltpu.VMEM((PROJ_TN, k), BF16)],
        compiler_params=_cparams(2),
        name="in_proj",
    )(h, w_t)
    return proj, (n_a + n_b) * PROJ_TN


def _merge_kernel(oa_ref, ob_ref, w1_ref, w2_ref, ga_ref, gb_ref, y_ref):
    ya = _dot(oa_ref[...].astype(BF16), w1_ref[...])
    yb = _dot(ob_ref[...].astype(BF16), w2_ref[...])
    sa = 1.0 / (1.0 + jnp.exp(-ga_ref[...]))
    sb = 1.0 / (1.0 + jnp.exp(-gb_ref[...]))
    y_ref[...] = (sa * ya + sb * yb).astype(y_ref.dtype)


def merge_branches(o_a, o_b, w1, w2, proj, tm, tn=512):
    m = o_a.shape[0]
    d = w1.shape[1]
    ga0 = C_GA // tn
    gb0 = (C_GA + d) // tn
    return pl.pallas_call(
        _merge_kernel,
        grid=(m // tm, d // tn),
        in_specs=[pl.BlockSpec((tm, D_ATT), lambda i, j: (i, 0)),
                  pl.BlockSpec((tm, D_ATT), lambda i, j: (i, 0)),
                  pl.BlockSpec((D_ATT, tn), lambda i, j: (0, j)),
                  pl.BlockSpec((D_ATT, tn), lambda i, j: (0, j)),
                  pl.BlockSpec((tm, tn), lambda i, j: (i, ga0 + j)),
                  pl.BlockSpec((tm, tn), lambda i, j: (i, gb0 + j))],
        out_specs=pl.BlockSpec((tm, tn), lambda i, j: (i, j)),
        out_shape=jax.ShapeDtypeStruct((m, d), BF16),
        compiler_params=_cparams(2),
        name="merge",
    )(o_a, o_b, w1, w2, proj, proj)


def _proj_norm_kernel(y_ref, wo_ref, x_ref, gpost_ref, gpre_ref, x1_ref, h2_ref):
    a = _dot(y_ref[...], wo_ref[...])
    r = a * lax.rsqrt(jnp.mean(a * a, axis=-1, keepdims=True) + RMS_EPS) * gpost_ref[...]
    x1 = x_ref[...] + r
    x1_ref[...] = x1
    h2 = x1 * lax.rsqrt(jnp.mean(x1 * x1, axis=-1, keepdims=True) + RMS_EPS) * gpre_ref[...]
    h2_ref[...] = h2.astype(h2_ref.dtype)


def proj_norm(y, wo, x, g_post, g_pre, tm=256):
    m, d = x.shape
    tm = min(tm, m)
    row = lambda i: (i, 0)
    fix = lambda i: (0, 0)
    return pl.pallas_call(
        _proj_norm_kernel,
        grid=(m // tm,),
        in_specs=[pl.BlockSpec((tm, d), row), pl.BlockSpec((d, d), fix), pl.BlockSpec((tm, d), row),
                  pl.BlockSpec((1, d), fix), pl.BlockSpec((1, d), fix)],
        out_specs=[pl.BlockSpec((tm, d), row), pl.BlockSpec((tm, d), row)],
        out_shape=[jax.ShapeDtypeStruct((m, d), F32), jax.ShapeDtypeStruct((m, d), BF16)],
        compiler_params=_cparams(1),
        name="proj_norm",
    )(y, wo, x, g_post.reshape(1, d), g_pre.reshape(1, d))


def _mlp_kernel(h_ref, wu_ref, wd_ref, x_ref, g_ref, o_ref, acc_ref):
    k = pl.program_id(1)

    @pl.when(k == 0)
    def _():
        acc_ref[...] = jnp.zeros_like(acc_ref)

    u = jnp.maximum(_dot(h_ref[...], wu_ref[...]), 0.0)
    acc_ref[...] += _dot((u * u).astype(BF16), wd_ref[...])

    @pl.when(k == pl.num_programs(1) - 1)
    def _():
        a = acc_ref[...]
        o_ref[...] = x_ref[...] + a * lax.rsqrt(jnp.mean(a * a, axis=-1, keepdims=True) + RMS_EPS) * g_ref[...]


def mlp(h2, w_up, w_down, x1, g_post, tm, tk=1024):
    m, d = x1.shape
    ff = w_up.shape[1]
    return pl.pallas_call(
        _mlp_kernel,
        grid=(m // tm, ff // tk),
        in_specs=[pl.BlockSpec((tm, d), lambda i, k: (i, 0)),
                  pl.BlockSpec((d, tk), lambda i, k: (0, k)),
                  pl.BlockSpec((tk, d), lambda i, k: (k, 0)),
                  pl.BlockSpec((tm, d), lambda i, k: (i, 0)),
                  pl.BlockSpec((1, d), lambda i, k: (0, 0))],
        out_specs=pl.BlockSpec((tm, d), lambda i, k: (i, 0)),
        out_shape=jax.ShapeDtypeStruct((m, d), F32),
        scratch_shapes=[pltpu.VMEM((tm, d), F32)],
        compiler_params=_cparams(2),
        name="mlp",
    )(h2, w_up, w_down, x1, g_post.reshape(1, d))


QB = LANES
KC = 512
SCORE_KC = 512
AQ = 256
ATTN_UNROLL = 2


def _dsa_mask_prompt_kernel(qi_ref, sm_all_ref, sm_blk_ref, mask_ref, kia_ref, kib_ref, keys_ref, mstar_ref, qq_ref,
                            *, seq, top_k):
    i = pl.program_id(1)
    n_chunks = (i * QB) // KC + 1

    @pl.when(i == 0)
    def _():
        sm = sm_all_ref[...]
        lane = lax.broadcasted_iota(I32, sm.shape, 1)
        kia_ref[...] = jnp.where(lane < IDX_DIM, sm, 0.0).astype(BF16)
        kib_ref[...] = jnp.where(lane >= IDX_DIM, pltpu.roll(sm, IDX_DIM, axis=1), 0.0).astype(BF16)

    w_t = sm_blk_ref[...].T * IDX_W_SCALE
    qi = qi_ref[...].astype(BF16)
    for g in range(N_IDX_HEADS // 4):
        qq_ref[g] = jnp.concatenate([qi[:, (2 * g) * LANES:(2 * g + 1) * LANES],
                                     qi[:, (2 * g + 1) * LANES:(2 * g + 2) * LANES]], axis=0)
    q_pos_s = i * QB + lax.broadcasted_iota(I32, (SCORE_KC, QB), 1)
    row_s = lax.broadcasted_iota(I32, (SCORE_KC, QB), 0)

    def score_chunk(c, carry):
        r0 = pl.multiple_of(c * SCORE_KC, SCORE_KC)
        ka = kia_ref[pl.ds(r0, SCORE_KC), :]
        kb = kib_ref[pl.ds(r0, SCORE_KC), :]
        acc = jnp.zeros((SCORE_KC, QB), F32)
        for g in range(N_IDX_HEADS // 4):
            qq = qq_ref[g]
            for odd, k_half in ((0, ka), (1, kb)):
                s = jnp.maximum(_dot_nt(k_half, qq), 0.0)
                for j in range(2):
                    head = 4 * g + 2 * j + odd
                    acc = acc + s[:, j * LANES:(j + 1) * LANES] * w_t[IDX_DIM + head:IDX_DIM + head + 1, :]
        sc = jnp.where(c * SCORE_KC + row_s <= q_pos_s, acc, NEG_INF)
        keys_ref[pl.ds(r0, SCORE_KC), :] = _float_key(sc)
        return carry

    lax.fori_loop(0, n_chunks * (KC // SCORE_KC), score_chunk, 0)

    n_tail = seq - n_chunks * KC
    q_pos = i * QB + lax.broadcasted_iota(I32, (KC, QB), 1)
    row = lax.broadcasted_iota(I32, (KC, QB), 0)

    def count(pred_fn):
        def body(c, acc):
            k = keys_ref[pl.ds(pl.multiple_of(c * KC, KC), KC), :]
            ones = jnp.where(pred_fn(k, c), 1, 0)
            return acc + jnp.sum(ones.reshape(KC // SUBLANES, SUBLANES, QB), axis=0)

        acc = lax.fori_loop(0, n_chunks, body, jnp.zeros((SUBLANES, QB), I32))
        return jnp.sum(acc, axis=0, keepdims=True)

    def count_ge(cand):
        return count(lambda k, c: k >= cand) + jnp.where(KEY_NEG_INF >= cand, n_tail, 0)

    base = jnp.where(count_ge(jnp.zeros((1, QB), I32)) >= top_k, 0, INT_MIN).astype(I32)

    def bit_body(t, base):
        cand = base | lax.shift_left(jnp.int32(1), 30 - t)
        return jnp.where(count_ge(cand) >= top_k, cand, base)

    v = lax.fori_loop(0, 31, bit_body, base)

    n_gt = count(lambda k, c: k > v) + jnp.where(KEY_NEG_INF > v, n_tail, 0)
    need = top_k - n_gt
    n_eq = count(lambda k, c: k == v)
    mstar_ref[...] = jnp.full((SUBLANES, QB), seq, I32)

    @pl.when(jnp.max(jnp.where(n_eq > need, 1, 0)) > 0)
    def _():
        def idx_bit(t, m):
            cand = m | lax.shift_left(jnp.int32(1), (seq - 1).bit_length() - 1 - t)
            below = count(lambda k, c: (k == v) & (c * KC + row < cand))
            return jnp.where(below < need, cand, m)

        m = lax.fori_loop(0, (seq - 1).bit_length(), idx_bit, jnp.zeros((1, QB), I32))
        mstar_ref[...] = jnp.broadcast_to(m, (SUBLANES, QB))

    mstar = mstar_ref[0:1, :]

    def emit(c, carry):
        k = keys_ref[pl.ds(pl.multiple_of(c * KC, KC), KC), :]
        k_pos = c * KC + row
        sel = (k > v) | ((k == v) & (k_pos <= mstar))
        m_t = jnp.where(sel & (k_pos <= q_pos), 0.0, NEG_INF)
        mask_ref[c] = m_t.T.astype(mask_ref.dtype)
        return carry

    lax.fori_loop(0, n_chunks, emit, 0)

    def clear(c, carry):
        mask_ref[c] = jnp.full((QB, KC), NEG_INF, mask_ref.dtype)
        return carry

    lax.fori_loop(n_chunks, seq // KC, clear, 0)


def dsa_mask_prompt(proj, batch, seq, c_small):
    assert seq % KC == 0 and KC % AQ == 0 and AQ % QB == 0
    nq = seq // QB
    per = AQ // QB
    top_k = min(TOP_K_MAX, seq // 4)
    return pl.pallas_call(
        functools.partial(_dsa_mask_prompt_kernel, seq=seq, top_k=top_k),
        grid=(batch, nq),
        in_specs=[pl.BlockSpec((QB, N_IDX_HEADS * IDX_DIM), lambda b, i: (b * nq + i, C_QI // (N_IDX_HEADS * IDX_DIM))),
                  pl.BlockSpec((seq, LANES), lambda b, i: (b, c_small // LANES)),
                  pl.BlockSpec((QB, LANES), lambda b, i: (b * nq + i, c_small // LANES))],
        out_specs=pl.BlockSpec((None, seq // KC, None, QB, KC), lambda b, i: ((b * nq + i) // per, 0, i % per, 0, 0)),
        out_shape=jax.ShapeDtypeStruct((batch * seq // AQ, seq // KC, per, QB, KC), BF16),
        scratch_shapes=[pltpu.VMEM((seq, LANES), BF16), pltpu.VMEM((seq, LANES), BF16),
                        pltpu.VMEM((seq, QB), I32), pltpu.VMEM((SUBLANES, QB), I32),
                        pltpu.VMEM((N_IDX_HEADS // 4, 2 * QB, LANES), BF16)],
        compiler_params=_cparams(2),
        name="dsa_mask_prompt",
    )(proj, proj, proj)


def _bucket_thresholds():
    d = np.arange(4 * REL_MAX_DISTANCE)
    b = _rel_bucket_np(d)
    assert np.all(np.diff(b) >= 0) and b[-1] == N_REL_BUCKETS - 1
    return [int(np.argmax(b > j)) for j in range(N_REL_BUCKETS - 1)]


def _bias_of_distance(rb_ref, h, d, thresholds):
    acc = jnp.full(d.shape, rb_ref[N_REL_BUCKETS - 1, h], F32)
    for j in range(N_REL_BUCKETS - 2, -1, -1):
        acc = jnp.where(d < thresholds[j], rb_ref[j, h], acc)
    return acc


def _lane_fold(x, op):
    out = x[:, :LANES]
    for j in range(1, x.shape[1] // LANES):
        out = op(out, x[:, j * LANES:(j + 1) * LANES])
    return out


def _dsa_attn_prompt_kernel(rb_ref, q_ref, k_ref, v_ref, mask_ref, o_ref, tab_ref, lg_ref, kb_ref, vb_ref, acc_ref,
                            l_ref, mx_ref, *, thresholds, unroll):
    h = pl.program_id(1)
    i = pl.program_id(2)
    per = KC // AQ
    part = i % per
    c_diag = i // per

    @pl.when(i == 0)
    def _():
        kb_ref[...] = k_ref[...].astype(BF16)
        vb_ref[...] = v_ref[...].astype(BF16)
        row = lax.broadcasted_iota(I32, (AQ, KC), 0)
        col = lax.broadcasted_iota(I32, (AQ, KC), 1)
        for p in range(per):
            for back in range(2):
                tab_ref[3 * p + back] = LOG2E * _bias_of_distance(rb_ref, h, p * AQ + back * KC + row - col,
                                                                  thresholds)
            tab_ref[3 * p + 2] = jnp.full((AQ, KC), LOG2E * rb_ref[N_REL_BUCKETS - 1, h], F32)

    q = q_ref[...].astype(BF16)

    n_chunks = c_diag + 1

    def over_chunks(chunk_fn):
        def trip(t, carry):
            for u in range(unroll):
                chunk_fn(t * unroll + u)
            return carry

        lax.fori_loop(0, n_chunks // unroll, trip, 0)
        for r in range(unroll - 1):
            @pl.when(n_chunks % unroll > r)
            def _():
                chunk_fn((n_chunks // unroll) * unroll + r)

    def logits_chunk(c):
        kc = kb_ref[pl.ds(pl.multiple_of(c * KC, KC), KC), :]
        s = _dot_nt(q, kc) * (ATTN_SCALE * LOG2E) + tab_ref[3 * part + jnp.minimum(c_diag - c, 2)]
        s = s + mask_ref[c].reshape(AQ, KC).astype(F32)
        lg_ref[c] = s
        mx_ref[...] = jnp.maximum(mx_ref[...], _lane_fold(s, jnp.maximum))

    mx_ref[...] = jnp.full((AQ, LANES), NEG_INF, F32)
    over_chunks(logits_chunk)
    m = jnp.max(mx_ref[...], axis=1, keepdims=True)
    acc_ref[...] = jnp.zeros_like(acc_ref)
    l_ref[...] = jnp.zeros_like(l_ref)

    def pv_chunk(c):
        p = jnp.exp2(lg_ref[c] - m)
        l_ref[...] += _lane_fold(p, jnp.add)
        acc_ref[...] += _dot(p.astype(BF16), vb_ref[pl.ds(pl.multiple_of(c * KC, KC), KC), :])

    over_chunks(pv_chunk)
    o_ref[...] = (acc_ref[...] / jnp.sum(l_ref[...], axis=1, keepdims=True)).astype(o_ref.dtype)


def dsa_attn_prompt(proj, mask, rel_bias, batch, seq):
    nq = seq // AQ
    nk = seq // KC
    hd = HEAD_DIM
    thresholds = _bucket_thresholds()
    assert thresholds[-1] <= KC + 1
    return pl.pallas_call(
        functools.partial(_dsa_attn_prompt_kernel, thresholds=thresholds, unroll=ATTN_UNROLL),
        grid=(batch, N_HEADS, nq),
        in_specs=[pl.BlockSpec(memory_space=pltpu.SMEM),
                  pl.BlockSpec((AQ, hd), lambda b, h, i: (b * nq + i, C_QD // hd + h)),
                  pl.BlockSpec((seq, hd), lambda b, h, i: (b, C_KD // hd + h)),
                  pl.BlockSpec((seq, hd), lambda b, h, i: (b, C_VD // hd + h)),
                  pl.BlockSpec((None, nk, AQ // QB, QB, KC), lambda b, h, i: (b * nq + i, 0, 0, 0, 0))],
        out_specs=pl.BlockSpec((AQ, hd), lambda b, h, i: (b * nq + i, h)),
        out_shape=jax.ShapeDtypeStruct((batch * seq, D_ATT), BF16),
        scratch_shapes=[pltpu.VMEM((3 * (KC // AQ), AQ, KC), F32), pltpu.VMEM((nk, AQ, KC), F32),
                        pltpu.VMEM((seq, hd), BF16), pltpu.VMEM((seq, hd), BF16),
                        pltpu.VMEM((AQ, hd), F32), pltpu.VMEM((AQ, LANES), F32), pltpu.VMEM((AQ, LANES), F32)],
        compiler_params=_cparams(3),
        name="dsa_attn_prompt",
    )(rel_bias, proj, proj, proj, mask)


SB_T = 256
SB_HEADS = 4
EXP_ZERO_BELOW = -104.0


def _suffix_sum_matrix(n):
    s = np.arange(n)
    tri = (s[:, None] > s[None, :]).astype(np.float32)
    return np.concatenate([tri, np.ones((n, n), np.float32)], axis=1)


def _sb_prompt_kernel(q_ref, k_ref, v_ref, r_ref, o_ref, kb_ref, vb_ref, acc_ref, run_ref):
    i = pl.program_id(2)

    @pl.when(i == 0)
    def _():
        kb_ref[...] = k_ref[...].astype(BF16)
        vb_ref[...] = v_ref[...].astype(BF16)

    q = q_ref[...].astype(BF16)
    acc_ref[...] = jnp.zeros_like(acc_ref)
    run_ref[...] = jnp.zeros_like(run_ref)
    row = lax.broadcasted_iota(I32, (SB_T, SB_T), 0)
    col = lax.broadcasted_iota(I32, (SB_T, SB_T), 1)

    def chunk(carry):
        step, _ = carry
        c = i - step
        r0 = pl.multiple_of(c * SB_T, SB_T)
        before = col < row + step * SB_T
        live = None
        for hh in range(SB_HEADS):
            cols = slice(hh * HEAD_DIM, (hh + 1) * HEAD_DIM)
            z = _dot_nt(q[:, cols], kb_ref[pl.ds(r0, SB_T), cols]) * ATTN_SCALE
            sp = _softplus(z)
            log_keep = jnp.where(before, -sp, 0.0)
            cs = _split_dot(log_keep, r_ref[...], 2)
            later = cs[:, :SB_T] + run_ref[hh]
            a = jnp.where(before, jnp.exp(z - sp + later), 0.0)
            acc_ref[:, cols] += _dot(a.astype(BF16), vb_ref[pl.ds(r0, SB_T), cols])
            run = run_ref[hh] + cs[:, SB_T:]
            run_ref[hh] = run
            top = jnp.max(run[:, :LANES])
            live = top if live is None else jnp.maximum(live, top)
        return step + 1, (live >= EXP_ZERO_BELOW).astype(I32)

    lax.while_loop(lambda carry: (carry[0] <= i) & (carry[1] > 0), chunk, (jnp.int32(0), jnp.int32(1)))
    o_ref[...] = acc_ref[...].astype(o_ref.dtype)


def sb_prompt(proj, batch, seq):
    nq = seq // SB_T
    w = SB_HEADS * HEAD_DIM
    r = jnp.asarray(_suffix_sum_matrix(SB_T), BF16)
    return pl.pallas_call(
        _sb_prompt_kernel,
        grid=(batch, N_HEADS // SB_HEADS, nq),
        in_specs=[pl.BlockSpec((SB_T, w), lambda b, h, i: (b * nq + i, C_QS // w + h)),
                  pl.BlockSpec((seq, w), lambda b, h, i: (b, C_KS // w + h)),
                  pl.BlockSpec((seq, w), lambda b, h, i: (b, C_VS // w + h)),
                  pl.BlockSpec((SB_T, 2 * SB_T), lambda b, h, i: (0, 0))],
        out_specs=pl.BlockSpec((SB_T, w), lambda b, h, i: (b * nq + i, h)),
        out_shape=jax.ShapeDtypeStruct((batch * seq, D_ATT), BF16),
        scratch_shapes=[pltpu.VMEM((seq, w), BF16), pltpu.VMEM((seq, w), BF16),
                        pltpu.VMEM((SB_T, w), F32), pltpu.VMEM((SB_HEADS, SB_T, SB_T), F32)],
        compiler_params=_cparams(3),
        name="sb_prompt",
    )(proj, proj, proj, r)


TOK_PAD = SUBLANES


def _pad_rows(x, rows):
    return jnp.concatenate([x, jnp.zeros((rows - x.shape[0], x.shape[1]), x.dtype)], axis=0)


def _head_rows(page_ref, h, page_size):
    return page_ref[pl.ds(h, page_size, stride=N_HEADS), :].astype(BF16)


def _page_specs(n_pages, rows, cols):
    return [pl.BlockSpec((None, rows, cols), functools.partial(lambda b, pt, p: (pt[b, p], 0, 0), p=p))
            for p in range(n_pages)]


def _sb_sample_kernel(pt_ref, q_ref, kn_ref, vn_ref, r_ref, k_hbm, v_hbm, o_ref, kbuf, vbuf, sems, run_ref, acc_ref,
                      *, n_pages, page_size, n_new, n_pre):
    b = pl.program_id(0)
    grp = N_HEADS * TOK_PAD
    spare = 2 * n_pre

    def page_copies(seq, page, slot):
        phys = pt_ref[seq, page]
        return (pltpu.make_async_copy(k_hbm.at[phys], kbuf.at[slot], sems.at[0, slot]),
                pltpu.make_async_copy(v_hbm.at[phys], vbuf.at[slot], sems.at[1, slot]))

    def prefetch(seq, action):
        for j in range(n_pre):
            for cp in page_copies(seq, n_pages - 1 - j, (seq % 2) * n_pre + j):
                action(cp)

    @pl.when(b == 0)
    def _():
        prefetch(b, lambda cp: cp.start())

    @pl.when(b + 1 < pl.num_programs(0))
    def _():
        prefetch(b + 1, lambda cp: cp.start())

    q = [q_ref[:, h * HEAD_DIM:(h + 1) * HEAD_DIM].astype(BF16) for h in range(N_HEADS)]
    row = lax.broadcasted_iota(I32, (grp, page_size), 0) % TOK_PAD
    lane = lax.broadcasted_iota(I32, (grp, page_size), 1)
    prefetch(b, lambda cp: cp.wait())

    def process(k_h, v_h, before, run, acc):
        z = jnp.concatenate([_dot_nt(q[h], k_h[h]) for h in range(N_HEADS)], axis=0) * ATTN_SCALE
        sp = _softplus(z)
        log_keep = -sp if before is None else jnp.where(before, -sp, 0.0)
        cs = _split_dot(log_keep, r_ref[...], 3)
        e = jnp.exp(z - sp + cs[:, :page_size] + run)
        a = e if before is None else jnp.where(before, e, 0.0)
        run = run + cs[:, page_size:]
        acc = [acc[h] + _dot(a[h * TOK_PAD:(h + 1) * TOK_PAD, :].astype(BF16), v_h[h]) for h in range(N_HEADS)]
        live = (jnp.max(jnp.where(row < n_new, run, NEG_INF)) >= EXP_ZERO_BELOW).astype(I32)
        return run, acc, live

    def process_slot(slot, run, acc):
        k_h = [kbuf[slot, pl.ds(h, page_size, stride=N_HEADS), :].astype(BF16) for h in range(N_HEADS)]
        v_h = [vbuf[slot, pl.ds(h, page_size, stride=N_HEADS), :].astype(BF16) for h in range(N_HEADS)]
        return process(k_h, v_h, None, run, acc)

    run, acc, live = process(
        [_pad_rows(kn_ref[:, h * HEAD_DIM:(h + 1) * HEAD_DIM], page_size).astype(BF16) for h in range(N_HEADS)],
        [_pad_rows(vn_ref[:, h * HEAD_DIM:(h + 1) * HEAD_DIM], page_size).astype(BF16) for h in range(N_HEADS)],
        lane < row, jnp.zeros((grp, page_size), F32), [jnp.zeros((TOK_PAD, HEAD_DIM), F32)] * N_HEADS)
    for j in range(n_pre):
        run, acc, live = process_slot((b % 2) * n_pre + j, run, acc)
    run_ref[...] = run
    for h in range(N_HEADS):
        acc_ref[h * TOK_PAD:(h + 1) * TOK_PAD, :] = acc[h]

    def fetch_and_process(carry):
        page, _ = carry
        for cp in page_copies(b, page, spare):
            cp.start()
        for cp in page_copies(b, page, spare):
            cp.wait()
        run, acc, live = process_slot(spare, run_ref[...],
                                      [acc_ref[h * TOK_PAD:(h + 1) * TOK_PAD, :] for h in range(N_HEADS)])
        run_ref[...] = run
        for h in range(N_HEADS):
            acc_ref[h * TOK_PAD:(h + 1) * TOK_PAD, :] = acc[h]
        return page - 1, live

    lax.while_loop(lambda carry: (carry[0] >= 0) & (carry[1] > 0), fetch_and_process,
                   (jnp.int32(n_pages - n_pre - 1), live))
    for h in range(N_HEADS):
        o_ref[:, h * HEAD_DIM:(h + 1) * HEAD_DIM] = acc_ref[h * TOK_PAD:(h + 1) * TOK_PAD, :]


def sb_sample(proj8, k_pool, v_pool, page_table, n_new):
    n_seq = proj8.shape[0]
    n_pages = page_table.shape[1]
    page_size = k_pool.shape[1] // N_HEADS
    assert page_size == LANES
    n_pre = min(2, n_pages)
    r = jnp.asarray(_suffix_sum_matrix(page_size), BF16)
    tok = lambda col: pl.BlockSpec((None, TOK_PAD, D_ATT), lambda b, pt: (b, 0, col // D_ATT))
    page_buf = pltpu.VMEM((2 * n_pre + 1, page_size * N_HEADS, HEAD_DIM), F32)
    grid_spec = pltpu.PrefetchScalarGridSpec(
        num_scalar_prefetch=1,
        grid=(n_seq,),
        in_specs=[tok(C_QS), tok(C_KS), tok(C_VS), pl.BlockSpec((page_size, 2 * page_size), lambda b, pt: (0, 0)),
                  pl.BlockSpec(memory_space=pl.ANY), pl.BlockSpec(memory_space=pl.ANY)],
        out_specs=pl.BlockSpec((None, TOK_PAD, D_ATT), lambda b, pt: (b, 0, 0)),
        scratch_shapes=[page_buf, page_buf, pltpu.SemaphoreType.DMA((2, 2 * n_pre + 1)),
                        pltpu.VMEM((N_HEADS * TOK_PAD, page_size), F32), pltpu.VMEM((N_HEADS * TOK_PAD, HEAD_DIM), F32)],
    )
    return pl.pallas_call(
        functools.partial(_sb_sample_kernel, n_pages=n_pages, page_size=page_size, n_new=n_new, n_pre=n_pre),
        grid_spec=grid_spec,
        out_shape=jax.ShapeDtypeStruct((n_seq, TOK_PAD, D_ATT), F32),
        compiler_params=_cparams(1),
        name="sb_sample",
    )(page_table, proj8, proj8, proj8, r, k_pool, v_pool)


def _dsa_scores_sample_kernel(pt_ref, qi_ref, w_ref, kin_ref, *rest, n_pages, page_size, n_new):
    ki_pages = rest[:n_pages]
    o_ref = rest[n_pages]
    qi = qi_ref[...].astype(BF16)
    w = w_ref[...] * IDX_W_SCALE
    o_ref[...] = jnp.zeros_like(o_ref)
    for g in range(n_pages + 1):
        ki_t = (kin_ref if g == n_pages else ki_pages[g])[...].astype(BF16)
        s = jnp.maximum(_dot(qi, ki_t), 0.0) * w
        s = jnp.sum(s.reshape(n_new, N_IDX_HEADS, page_size), axis=1)
        o_ref[0:n_new, g * page_size:(g + 1) * page_size] = s


def dsa_scores_sample(qi3, w_rep, ki_new_t, ki_pool_t, page_table):
    n_seq = qi3.shape[0]
    n_new = qi3.shape[1] // N_IDX_HEADS
    n_pages = page_table.shape[1]
    page_size = ki_pool_t.shape[2]
    assert page_size == LANES
    seq_blk = lambda r, c: pl.BlockSpec((None, r, c), lambda b, pt: (b, 0, 0))
    grid_spec = pltpu.PrefetchScalarGridSpec(
        num_scalar_prefetch=1,
        grid=(n_seq,),
        in_specs=[seq_blk(n_new * N_IDX_HEADS, IDX_DIM), seq_blk(n_new * N_IDX_HEADS, LANES),
                  seq_blk(IDX_DIM, page_size)] + _page_specs(n_pages, IDX_DIM, page_size),
        out_specs=seq_blk(TOK_PAD, (n_pages + 1) * page_size),
    )
    return pl.pallas_call(
        functools.partial(_dsa_scores_sample_kernel, n_pages=n_pages, page_size=page_size, n_new=n_new),
        grid_spec=grid_spec,
        out_shape=jax.ShapeDtypeStruct((n_seq, TOK_PAD, (n_pages + 1) * page_size), F32),
        compiler_params=_cparams(1),
        name="dsa_scores_sample",
    )(page_table, qi3, w_rep, ki_new_t, *([ki_pool_t] * n_pages))


MASK_ROWS = 256


def _dsa_mask_sample_kernel(s_ref, ones_ref, m_ref, *, n_past, n_new, top_k):
    n_chunks = s_ref.shape[1] // LANES
    shape = (s_ref.shape[0], LANES)
    tok = lax.broadcasted_iota(I32, shape, 0) % TOK_PAD
    lane = lax.broadcasted_iota(I32, shape, 1)
    ones = ones_ref[...]
    keys, pos, exists = [], [], []
    for c in range(n_chunks):
        s = s_ref[:, c * LANES:(c + 1) * LANES]
        k_pos = c * LANES + lane
        s = jnp.where(k_pos <= n_past + tok, s, NEG_INF)
        keys.append(_float_key(s))
        pos.append(k_pos)
        exists.append(None if (c + 1) * LANES <= n_past + n_new else k_pos < n_past + n_new)

    def count(pred_fn):
        part = jnp.zeros(shape, F32)
        for c in range(n_chunks):
            p = pred_fn(keys[c], pos[c])
            if exists[c] is not None:
                p = p & exists[c]
            part = part + jnp.where(p, 1.0, 0.0)
        return _dot(part.astype(BF16), ones)

    base = jnp.where(count(lambda k, p: k >= 0) >= top_k, 0, INT_MIN).astype(I32)

    def bit_body(t, base):
        cand = base | lax.shift_left(jnp.int32(1), 30 - t)
        return jnp.where(count(lambda k, p: k >= cand) >= top_k, cand, base)

    v = lax.fori_loop(0, 31, bit_body, base)
    need = top_k - count(lambda k, p: k > v)

    n_bits = (n_chunks * LANES - 1).bit_length()

    def idx_bit(t, m):
        cand = m | lax.shift_left(jnp.int32(1), n_bits - 1 - t)
        below = count(lambda k, p: (k == v) & (p < cand))
        return jnp.where(below < need, cand, m)

    mstar = lax.fori_loop(0, n_bits, idx_bit, jnp.zeros(shape, I32))
    for c in range(n_chunks):
        sel = (keys[c] > v) | ((keys[c] == v) & (pos[c] <= mstar))
        sel = sel & (pos[c] <= n_past + tok) & (tok < n_new)
        if exists[c] is not None:
            sel = sel & exists[c]
        m_ref[:, c * LANES:(c + 1) * LANES] = jnp.where(sel, 1.0, 0.0)


def dsa_mask_sample(scores2, n_past, n_new):
    rows, width = scores2.shape
    top_k = min(TOP_K_MAX, (n_past + n_new) // 4)
    tr = min(MASK_ROWS, rows)
    return pl.pallas_call(
        functools.partial(_dsa_mask_sample_kernel, n_past=n_past, n_new=n_new, top_k=top_k),
        grid=(rows // tr,),
        in_specs=[pl.BlockSpec((tr, width), lambda i: (i, 0)), pl.BlockSpec((LANES, LANES), lambda i: (0, 0))],
        out_specs=pl.BlockSpec((tr, width), lambda i: (i, 0)),
        out_shape=jax.ShapeDtypeStruct((rows, width), F32),
        compiler_params=_cparams(1),
        name="dsa_mask_sample",
    )(scores2, jnp.ones((LANES, LANES), BF16))


def _sample_bucket_tables(n_new, page_size):
    j = (np.arange(N_HEADS * TOK_PAD) % TOK_PAD)[:, None]
    u = np.arange(page_size)[None, :]
    assert _rel_bucket_np(np.array([page_size + 1]))[0] == N_REL_BUCKETS - 1
    return np.stack([_rel_bucket_np(page_size + j - u), _rel_bucket_np(j - u)]).astype(np.int32)


def _dsa_attn_sample_kernel(pt_ref, rb_ref, q_ref, kn_ref, vn_ref, m_ref, bkt_ref, *rest, n_pages, page_size, n_new):
    k_pages = rest[:n_pages]
    v_pages = rest[n_pages:2 * n_pages]
    o_ref = rest[2 * n_pages]
    tab_ref = rest[2 * n_pages + 1]
    lg_ref = rest[2 * n_pages + 2]
    grp = N_HEADS * TOK_PAD

    @pl.when(pl.program_id(0) == 0)
    def _():
        for h in range(N_HEADS):
            rows = slice(h * TOK_PAD, (h + 1) * TOK_PAD)
            for t in range(2):
                bk = bkt_ref[t, rows, :]
                acc = jnp.zeros(bk.shape, F32)
                for b in range(N_REL_BUCKETS):
                    acc = jnp.where(bk == b, rb_ref[b, h], acc)
                tab_ref[t, rows, :] = acc
            tab_ref[2, rows, :] = jnp.full((TOK_PAD, page_size), rb_ref[N_REL_BUCKETS - 1, h], F32)

    q = [q_ref[:, h * HEAD_DIM:(h + 1) * HEAD_DIM].astype(BF16) for h in range(N_HEADS)]
    mx = jnp.full((grp, page_size), NEG_INF, F32)
    for g in range(n_pages + 1):
        if g == n_pages:
            k_h = [_pad_rows(kn_ref[:, h * HEAD_DIM:(h + 1) * HEAD_DIM], page_size).astype(BF16) for h in range(N_HEADS)]
            bias = tab_ref[1]
        else:
            k_h = [_head_rows(k_pages[g], h, page_size) for h in range(N_HEADS)]
            bias = tab_ref[0] if g == n_pages - 1 else tab_ref[2]
        s = jnp.concatenate([_dot_nt(q[h], k_h[h]) for h in range(N_HEADS)], axis=0) * ATTN_SCALE + bias
        sel = jnp.tile(m_ref[:, g * page_size:(g + 1) * page_size], (N_HEADS, 1))
        s = jnp.where(sel > 0.5, s, NEG_INF)
        lg_ref[g] = s
        mx = jnp.maximum(mx, s)
    m = jnp.max(mx, axis=1, keepdims=True)
    l = jnp.zeros((grp, page_size), F32)
    acc = [jnp.zeros((TOK_PAD, HEAD_DIM), F32) for _ in range(N_HEADS)]
    for g in range(n_pages + 1):
        p = jnp.exp(lg_ref[g] - m)
        l = l + p
        for h in range(N_HEADS):
            if g == n_pages:
                v_h = _pad_rows(vn_ref[:, h * HEAD_DIM:(h + 1) * HEAD_DIM], page_size).astype(BF16)
            else:
                v_h = _head_rows(v_pages[g], h, page_size)
            acc[h] = acc[h] + _dot(p[h * TOK_PAD:(h + 1) * TOK_PAD, :].astype(BF16), v_h)
    inv = 1.0 / jnp.sum(l, axis=1, keepdims=True)
    for h in range(N_HEADS):
        o_ref[:, h * HEAD_DIM:(h + 1) * HEAD_DIM] = acc[h] * inv[h * TOK_PAD:(h + 1) * TOK_PAD, :]


def dsa_attn_sample(proj8, mask3, rel_bias, k_pool, v_pool, page_table, n_new):
    n_seq = proj8.shape[0]
    n_pages = page_table.shape[1]
    page_size = k_pool.shape[1] // N_HEADS
    assert page_size == LANES
    grp = N_HEADS * TOK_PAD
    width = (n_pages + 1) * page_size
    tok = lambda col: pl.BlockSpec((None, TOK_PAD, D_ATT), lambda b, pt: (b, 0, col // D_ATT))
    grid_spec = pltpu.PrefetchScalarGridSpec(
        num_scalar_prefetch=1,
        grid=(n_seq,),
        in_specs=[pl.BlockSpec(memory_space=pltpu.SMEM), tok(C_QD), tok(C_KD), tok(C_VD),
                  pl.BlockSpec((None, TOK_PAD, width), lambda b, pt: (b, 0, 0)),
                  pl.BlockSpec((2, grp, page_size), lambda b, pt: (0, 0, 0))]
                 + _page_specs(n_pages, page_size * N_HEADS, HEAD_DIM)
                 + _page_specs(n_pages, page_size * N_HEADS, HEAD_DIM),
        out_specs=pl.BlockSpec((None, TOK_PAD, D_ATT), lambda b, pt: (b, 0, 0)),
        scratch_shapes=[pltpu.VMEM((3, grp, page_size), F32), pltpu.VMEM((n_pages + 1, grp, page_size), F32)],
    )
    return pl.pallas_call(
        functools.partial(_dsa_attn_sample_kernel, n_pages=n_pages, page_size=page_size, n_new=n_new),
        grid_spec=grid_spec,
        out_shape=jax.ShapeDtypeStruct((n_seq, TOK_PAD, D_ATT), F32),
        compiler_params=_cparams(1, vmem_mb=56),
        name="dsa_attn_sample",
    )(page_table, rel_bias, proj8, proj8, proj8, mask3, jnp.asarray(_sample_bucket_tables(n_new, page_size)),
      *([k_pool] * n_pages), *([v_pool] * n_pages))


def _tail(x, proj, o_a, o_b, weights, gains):
    w1, w2, wo, wu, wd = weights
    g_attn_post, g_mlp_pre, g_mlp_post = gains
    tm = min(512, x.shape[0])
    y = merge_branches(o_a, o_b, w1, w2, proj, tm)
    x1, h2 = proj_norm(y, wo, x, g_attn_post, g_mlp_pre)
    return mlp(h2, wu, wd, x1, g_mlp_post, tm)


def kernel(x_prompt, x_sample, cache_k_dsa, cache_v_dsa, cache_k_idx, cache_k_sb, cache_v_sb, page_table, rel_bias,
           w_in, w_out_dsa, w_out_sb, w_o, w_up, w_down, g_attn_pre, g_attn_post, g_mlp_pre, g_mlp_post):
    batch, seq, d_model = x_prompt.shape
    n_seq, n_new, _ = x_sample.shape
    depth = w_in.shape[0]
    n_pool, page_size = cache_k_idx.shape[1], cache_k_idx.shape[2]
    n_past = page_table.shape[1] * page_size

    xp = x_prompt.reshape(batch * seq, d_model)
    xs = jnp.pad(x_sample, ((0, 0), (0, TOK_PAD - n_new), (0, 0))).reshape(n_seq * TOK_PAD, d_model)
    rows_p, rows_s = [], []
    for l in range(depth):
        w_in_t = jnp.swapaxes(w_in[l], 0, 1)
        weights = tuple(w[l].astype(BF16) for w in (w_out_dsa, w_out_sb, w_o, w_up, w_down))
        gains = (g_attn_post[l], g_mlp_pre[l], g_mlp_post[l])

        hp = rms_cast(xp, g_attn_pre[l])
        proj, c_small = in_proj(hp, w_in_t, tm=min(1024, hp.shape[0]))
        mask = dsa_mask_prompt(proj, batch, seq, c_small)
        o_a = dsa_attn_prompt(proj, mask, rel_bias, batch, seq)
        o_b = sb_prompt(proj, batch, seq)
        xp_new = _tail(xp, proj, o_a, o_b, weights, gains)
        heads = lambda c: proj[:, c:c + D_ATT].reshape(batch, seq, N_HEADS, HEAD_DIM)
        rows_p.append((heads(C_KD), heads(C_VD), proj[:, c_small:c_small + IDX_DIM].reshape(batch, seq, IDX_DIM),
                       heads(C_KS), heads(C_VS)))
        xp = xp_new

        hs = rms_cast(xs, g_attn_pre[l])
        proj_s, _ = in_proj(hs, w_in_t, tm=min(1024, hs.shape[0]))
        proj8 = proj_s.reshape(n_seq, TOK_PAD, proj_s.shape[1])
        real = lambda c, w: proj8[:, :n_new, c:c + w]
        qi3 = real(C_QI, N_IDX_HEADS * IDX_DIM).reshape(n_seq, n_new * N_IDX_HEADS, IDX_DIM)
        w_rep = jnp.broadcast_to(real(c_small + IDX_DIM, N_IDX_HEADS).reshape(n_seq, n_new * N_IDX_HEADS, 1),
                                 (n_seq, n_new * N_IDX_HEADS, LANES))
        ki_new = real(c_small, IDX_DIM)
        ki_new_t = jnp.pad(jnp.swapaxes(ki_new, 1, 2), ((0, 0), (0, 0), (0, page_size - n_new)))
        pool2 = lambda c: c[l].reshape(n_pool, page_size * N_HEADS, HEAD_DIM)
        scores = dsa_scores_sample(qi3, w_rep, ki_new_t, jnp.swapaxes(cache_k_idx[l], 1, 2), page_table)
        mask_s = dsa_mask_sample(scores.reshape(n_seq * TOK_PAD, scores.shape[2]), n_past, n_new)
        o_a = dsa_attn_sample(proj8, mask_s.reshape(scores.shape), rel_bias, pool2(cache_k_dsa), pool2(cache_v_dsa),
                              page_table, n_new)
        o_b = sb_sample(proj8, pool2(cache_k_sb), pool2(cache_v_sb), page_table, n_new)
        flat = lambda o: o.reshape(n_seq * TOK_PAD, D_ATT)
        xs_new = _tail(xs, proj_s, flat(o_a), flat(o_b), weights, gains)
        heads_s = lambda c: real(c, D_ATT).reshape(n_seq, n_new, N_HEADS, HEAD_DIM)
        rows_s.append((heads_s(C_KD), heads_s(C_VD), ki_new, heads_s(C_KS), heads_s(C_VS)))
        xs = xs_new

    outs_p = [jnp.stack(r, axis=0) for r in zip(*rows_p)]
    outs_s = [jnp.stack(r, axis=0) for r in zip(*rows_s)]
    return (xp.reshape(batch, seq, d_model), xs.reshape(n_seq, TOK_PAD, d_model)[:, :n_new], *outs_p, *outs_s)
```

```python
import functools
import math

import numpy as np
import jax
import jax.numpy as jnp
from jax import lax
from jax.experimental import pallas as pl
from jax.experimental.pallas import tpu as pltpu

F32 = jnp.float32
BF16 = jnp.bfloat16
I32 = jnp.int32

HEAD_DIM = 128
N_HEADS = 8
N_IDX_HEADS = 16
IDX_DIM = 64
IDX_W_SCALE = (N_IDX_HEADS * IDX_DIM) ** -0.5
TOP_K_MAX = 256
N_REL_BUCKETS = 32
REL_MAX_DISTANCE = 128
RMS_EPS = 1e-6
NEG_INF = -1e30
ATTN_SCALE = HEAD_DIM ** -0.5
LOG2E = math.log2(math.e)

LANES = 128
SUBLANES = 8
D_ATT = N_HEADS * HEAD_DIM

C_QD, C_KD, C_VD, C_QI, C_QS, C_KS, C_VS = (k * D_ATT for k in range(7))
C_GA = 7 * D_ATT
PROJ_TN = 1024


def _key_of(x):
    b = int(np.float32(x).view(np.int32))
    return b if b >= 0 else b ^ 0x7FFFFFFF


KEY_NEG_INF = _key_of(NEG_INF)
INT_MIN = -(2 ** 31)


def _cparams(n_axes, vmem_mb=48):
    return pltpu.CompilerParams(dimension_semantics=("arbitrary",) * n_axes,
                                vmem_limit_bytes=vmem_mb * 1024 * 1024)


def _dot(a, b):
    return jnp.dot(a, b, preferred_element_type=F32)


def _dot_nt(a, b):
    return lax.dot_general(a, b, (((1,), (1,)), ((), ())), preferred_element_type=F32)


def _split_dot(x, m01, passes):
    out = None
    rem = x
    for p in range(passes):
        piece = rem.astype(BF16)
        d = _dot(piece, m01)
        out = d if out is None else out + d
        if p + 1 < passes:
            rem = rem - piece.astype(F32)
    return out


def _softplus(z):
    return jnp.maximum(z, 0.0) + jnp.log1p(jnp.exp(-jnp.abs(z)))


def _float_key(s):
    s = jnp.where(s == 0.0, 0.0, s)
    bits = lax.bitcast_convert_type(s, I32)
    return jnp.where(bits >= 0, bits, bits ^ 0x7FFFFFFF)


def _rel_bucket_np(dist):
    max_exact = N_REL_BUCKETS // 2
    d = np.maximum(dist, 0)
    df = np.maximum(d, 1).astype(np.float32)
    log_part = np.log(df / np.float32(max_exact)) / np.float32(math.log(REL_MAX_DISTANCE / max_exact))
    large = max_exact + (log_part * np.float32(N_REL_BUCKETS - max_exact)).astype(np.int32)
    return np.where(d < max_exact, d, np.minimum(large, N_REL_BUCKETS - 1)).astype(np.int32)


def _rms_cast_kernel(x_ref, g_ref, o_ref):
    x = x_ref[...]
    ms = jnp.mean(x * x, axis=-1, keepdims=True)
    o_ref[...] = (x * lax.rsqrt(ms + RMS_EPS) * g_ref[...]).astype(o_ref.dtype)


def rms_cast(x, g, tm=256):
    m, d = x.shape
    tm = min(tm, m)
    return pl.pallas_call(
        _rms_cast_kernel,
        grid=(m // tm,),
        in_specs=[pl.BlockSpec((tm, d), lambda i: (i, 0)), pl.BlockSpec((1, d), lambda i: (0, 0))],
        out_specs=pl.BlockSpec((tm, d), lambda i: (i, 0)),
        out_shape=jax.ShapeDtypeStruct((m, d), BF16),
        compiler_params=_cparams(1),
        name="rms_cast",
    )(x, g.reshape(1, d))


def _in_proj_kernel(h_ref, wt_ref, o_ref, wb_ref):
    @pl.when(pl.program_id(1) == 0)
    def _():
        wb_ref[...] = wt_ref[...].astype(BF16)

    o_ref[...] = _dot_nt(h_ref[...], wb_ref[...])


def in_proj(h, w_t, tm):
    m, k = h.shape
    in_cols = w_t.shape[0]
    c_ki = 4 * D_ATT
    c_after = c_ki + IDX_DIM + N_IDX_HEADS
    assert c_ki % PROJ_TN == 0 and (in_cols - c_after) % PROJ_TN == 0 and c_ki + PROJ_TN <= in_cols
    n_a = c_ki // PROJ_TN
    n_b = (in_cols - c_after) // PROJ_TN

    assert c_after % SUBLANES == 0 and PROJ_TN % SUBLANES == 0
    tile8, after8, ki8 = PROJ_TN // SUBLANES, c_after // SUBLANES, c_ki // SUBLANES

    def w_row(j):
        return jnp.where(j < n_a, j * tile8, jnp.where(j < n_a + n_b, after8 + (j - n_a) * tile8, ki8)) * SUBLANES

    n_tiles = n_a + n_b + 1
    proj = pl.pallas_call(
        _in_proj_kernel,
        grid=(n_tiles, m // tm),
        in_specs=[pl.BlockSpec((tm, k), lambda j, i: (i, 0)),
                  pl.BlockSpec((pl.Element(PROJ_TN), pl.Element(k)), lambda j, i: (w_row(j), 0))],
        out_specs=pl.BlockSpec((tm, PROJ_TN), lambda j, i: (i, j)),
        out_shape=jax.ShapeDtypeStruct((m, n_tiles * PROJ_TN), F32),
        scratch_shapes=[pltpu.VMEM((PROJ_TN, k), BF16)],
        compiler_params=_cparams(2),
        name="in_proj",
    )(h, w_t)
    return proj, (n_a + n_b) * PROJ_TN


def _merge_kernel(oa_ref, ob_ref, w1_ref, w2_ref, ga_ref, gb_ref, y_ref):
    ya = _dot(oa_ref[...].astype(BF16), w1_ref[...])
    yb = _dot(ob_ref[...].astype(BF16), w2_ref[...])
    sa = 1.0 / (1.0 + jnp.exp(-ga_ref[...]))
    sb = 1.0 / (1.0 + jnp.exp(-gb_ref[...]))
    y_ref[...] = (sa * ya + sb * yb).astype(y_ref.dtype)


def merge_branches(o_a, o_b, w1, w2, proj, tm, tn=1024):
    m = o_a.shape[0]
    d = w1.shape[1]
    tn = min(tn, d)
    ga0 = C_GA // tn
    gb0 = (C_GA + d) // tn
    return pl.pallas_call(
        _merge_kernel,
        grid=(d // tn, m // tm),
        in_specs=[pl.BlockSpec((tm, D_ATT), lambda j, i: (i, 0)),
                  pl.BlockSpec((tm, D_ATT), lambda j, i: (i, 0)),
                  pl.BlockSpec((D_ATT, tn), lambda j, i: (0, j)),
                  pl.BlockSpec((D_ATT, tn), lambda j, i: (0, j)),
                  pl.BlockSpec((tm, tn), lambda j, i: (i, ga0 + j)),
                  pl.BlockSpec((tm, tn), lambda j, i: (i, gb0 + j))],
        out_specs=pl.BlockSpec((tm, tn), lambda j, i: (i, j)),
        out_shape=jax.ShapeDtypeStruct((m, d), BF16),
        compiler_params=_cparams(2),
        name="merge",
    )(o_a, o_b, w1, w2, proj, proj)


def _proj_norm_kernel(y_ref, wo_ref, x_ref, gpost_ref, gpre_ref, x1_ref, h2_ref, *, parts):
    rows_per = y_ref.shape[0] // parts
    for p in range(parts):
        rows = slice(p * rows_per, (p + 1) * rows_per)
        a = _dot(y_ref[rows, :], wo_ref[...])
        r = a * lax.rsqrt(jnp.mean(a * a, axis=-1, keepdims=True) + RMS_EPS) * gpost_ref[...]
        x1 = x_ref[rows, :] + r
        x1_ref[rows, :] = x1
        h2 = x1 * lax.rsqrt(jnp.mean(x1 * x1, axis=-1, keepdims=True) + RMS_EPS) * gpre_ref[...]
        h2_ref[rows, :] = h2.astype(h2_ref.dtype)


def proj_norm(y, wo, x, g_post, g_pre, tm=512):
    m, d = x.shape
    tm = min(tm, m)
    parts = 2 if tm % (2 * 2 * SUBLANES) == 0 else 1
    row = lambda i: (i, 0)
    fix = lambda i: (0, 0)
    return pl.pallas_call(
        functools.partial(_proj_norm_kernel, parts=parts),
        grid=(m // tm,),
        in_specs=[pl.BlockSpec((tm, d), row), pl.BlockSpec((d, d), fix), pl.BlockSpec((tm, d), row),
                  pl.BlockSpec((1, d), fix), pl.BlockSpec((1, d), fix)],
        out_specs=[pl.BlockSpec((tm, d), row), pl.BlockSpec((tm, d), row)],
        out_shape=[jax.ShapeDtypeStruct((m, d), F32), jax.ShapeDtypeStruct((m, d), BF16)],
        compiler_params=_cparams(1),
        name="proj_norm",
    )(y, wo, x, g_post.reshape(1, d), g_pre.reshape(1, d))


def _mlp_kernel(h_ref, wu_ref, wd_ref, x_ref, g_ref, o_ref, acc_ref):
    k = pl.program_id(1)

    @pl.when(k == 0)
    def _():
        acc_ref[...] = jnp.zeros_like(acc_ref)

    u = jnp.maximum(_dot(h_ref[...], wu_ref[...]), 0.0)
    acc_ref[...] += _dot((u * u).astype(BF16), wd_ref[...])

    @pl.when(k == pl.num_programs(1) - 1)
    def _():
        a = acc_ref[...]
        o_ref[...] = x_ref[...] + a * lax.rsqrt(jnp.mean(a * a, axis=-1, keepdims=True) + RMS_EPS) * g_ref[...]


def mlp(h2, w_up, w_down, x1, g_post, tm, tk=1024):
    m, d = x1.shape
    ff = w_up.shape[1]
    return pl.pallas_call(
        _mlp_kernel,
        grid=(m // tm, ff // tk),
        in_specs=[pl.BlockSpec((tm, d), lambda i, k: (i, 0)),
                  pl.BlockSpec((d, tk), lambda i, k: (0, k)),
                  pl.BlockSpec((tk, d), lambda i, k: (k, 0)),
                  pl.BlockSpec((tm, d), lambda i, k: (i, 0)),
                  pl.BlockSpec((1, d), lambda i, k: (0, 0))],
        out_specs=pl.BlockSpec((tm, d), lambda i, k: (i, 0)),
        out_shape=jax.ShapeDtypeStruct((m, d), F32),
        scratch_shapes=[pltpu.VMEM((tm, d), F32)],
        compiler_params=_cparams(2),
        name="mlp",
    )(h2, w_up, w_down, x1, g_post.reshape(1, d))


QB = LANES
KC = 512
SCORE_KC = 512
AQ = 256
ATTN_UNROLL = 4


def _dsa_mask_prompt_kernel(qi_ref, sm_all_ref, sm_blk_ref, mask_ref, kia_ref, kib_ref, keys_ref, mstar_ref, qq_ref,
                            *, seq, top_k):
    i = pl.program_id(1)
    n_chunks = (i * QB) // KC + 1

    @pl.when(i == 0)
    def _():
        sm = sm_all_ref[...]
        lane = lax.broadcasted_iota(I32, sm.shape, 1)
        kia_ref[...] = jnp.where(lane < IDX_DIM, sm, 0.0).astype(BF16)
        kib_ref[...] = jnp.where(lane >= IDX_DIM, pltpu.roll(sm, IDX_DIM, axis=1), 0.0).astype(BF16)

    w_t = sm_blk_ref[...].T * IDX_W_SCALE
    qi = qi_ref[...].astype(BF16)
    for g in range(N_IDX_HEADS // 4):
        qq_ref[g] = jnp.concatenate([qi[:, (2 * g) * LANES:(2 * g + 1) * LANES],
                                     qi[:, (2 * g + 1) * LANES:(2 * g + 2) * LANES]], axis=0)
    q_pos_s = i * QB + lax.broadcasted_iota(I32, (SCORE_KC, QB), 1)
    row_s = lax.broadcasted_iota(I32, (SCORE_KC, QB), 0)

    def score_chunk(c, carry):
        r0 = pl.multiple_of(c * SCORE_KC, SCORE_KC)
        ka = kia_ref[pl.ds(r0, SCORE_KC), :]
        kb = kib_ref[pl.ds(r0, SCORE_KC), :]
        acc = jnp.zeros((SCORE_KC, QB), F32)
        for g in range(N_IDX_HEADS // 4):
            qq = qq_ref[g]
            for odd, k_half in ((0, ka), (1, kb)):
                s = jnp.maximum(_dot_nt(k_half, qq), 0.0)
                for j in range(2):
                    head = 4 * g + 2 * j + odd
                    acc = acc + s[:, j * LANES:(j + 1) * LANES] * w_t[IDX_DIM + head:IDX_DIM + head + 1, :]
        sc = jnp.where(c * SCORE_KC + row_s <= q_pos_s, acc, NEG_INF)
        keys_ref[pl.ds(r0, SCORE_KC), :] = _float_key(sc)
        return carry

    lax.fori_loop(0, n_chunks * (KC // SCORE_KC), score_chunk, 0)

    n_tail = seq - n_chunks * KC
    q_pos = i * QB + lax.broadcasted_iota(I32, (KC, QB), 1)
    row = lax.broadcasted_iota(I32, (KC, QB), 0)

    def count(pred_fn):
        def body(c, acc):
            k = keys_ref[pl.ds(pl.multiple_of(c * KC, KC), KC), :]
            ones = jnp.where(pred_fn(k, c), 1, 0)
            return acc + jnp.sum(ones.reshape(KC // SUBLANES, SUBLANES, QB), axis=0)

        acc = lax.fori_loop(0, n_chunks, body, jnp.zeros((SUBLANES, QB), I32))
        return jnp.sum(acc, axis=0, keepdims=True)

    def count_ge(cand):
        return count(lambda k, c: k >= cand) + jnp.where(KEY_NEG_INF >= cand, n_tail, 0)

    base = jnp.where(count_ge(jnp.zeros((1, QB), I32)) >= top_k, 0, INT_MIN).astype(I32)

    def bit_body(t, base):
        cand = base | lax.shift_left(jnp.int32(1), 30 - t)
        return jnp.where(count_ge(cand) >= top_k, cand, base)

    v = lax.fori_loop(0, 31, bit_body, base)

    n_gt = count(lambda k, c: k > v) + jnp.where(KEY_NEG_INF > v, n_tail, 0)
    need = top_k - n_gt
    n_eq = count(lambda k, c: k == v)
    mstar_ref[...] = jnp.full((SUBLANES, QB), seq, I32)

    @pl.when(jnp.max(jnp.where(n_eq > need, 1, 0)) > 0)
    def _():
        def idx_bit(t, m):
            cand = m | lax.shift_left(jnp.int32(1), (seq - 1).bit_length() - 1 - t)
            below = count(lambda k, c: (k == v) & (c * KC + row < cand))
            return jnp.where(below < need, cand, m)

        m = lax.fori_loop(0, (seq - 1).bit_length(), idx_bit, jnp.zeros((1, QB), I32))
        mstar_ref[...] = jnp.broadcast_to(m, (SUBLANES, QB))

    mstar = mstar_ref[0:1, :]

    def emit(c, carry):
        k = keys_ref[pl.ds(pl.multiple_of(c * KC, KC), KC), :]
        k_pos = c * KC + row
        sel = (k > v) | ((k == v) & (k_pos <= mstar))
        m_t = jnp.where(sel & (k_pos <= q_pos), 0.0, NEG_INF)
        mask_ref[c] = m_t.T.astype(mask_ref.dtype)
        return carry

    lax.fori_loop(0, n_chunks, emit, 0)

    def clear(c, carry):
        mask_ref[c] = jnp.full((QB, KC), NEG_INF, mask_ref.dtype)
        return carry

    lax.fori_loop(n_chunks, seq // KC, clear, 0)


def dsa_mask_prompt(proj, batch, seq, c_small):
    assert seq % KC == 0 and KC % AQ == 0 and AQ % QB == 0
    nq = seq // QB
    per = AQ // QB
    top_k = min(TOP_K_MAX, seq // 4)
    return pl.pallas_call(
        functools.partial(_dsa_mask_prompt_kernel, seq=seq, top_k=top_k),
        grid=(batch, nq),
        in_specs=[pl.BlockSpec((QB, N_IDX_HEADS * IDX_DIM), lambda b, i: (b * nq + i, C_QI // (N_IDX_HEADS * IDX_DIM))),
                  pl.BlockSpec((seq, LANES), lambda b, i: (b, c_small // LANES)),
                  pl.BlockSpec((QB, LANES), lambda b, i: (b * nq + i, c_small // LANES))],
        out_specs=pl.BlockSpec((None, seq // KC, None, QB, KC), lambda b, i: ((b * nq + i) // per, 0, i % per, 0, 0)),
        out_shape=jax.ShapeDtypeStruct((batch * seq // AQ, seq // KC, per, QB, KC), BF16),
        scratch_shapes=[pltpu.VMEM((seq, LANES), BF16), pltpu.VMEM((seq, LANES), BF16),
                        pltpu.VMEM((seq, QB), I32), pltpu.VMEM((SUBLANES, QB), I32),
                        pltpu.VMEM((N_IDX_HEADS // 4, 2 * QB, LANES), BF16)],
        compiler_params=_cparams(2),
        name="dsa_mask_prompt",
    )(proj, proj, proj)


def _bucket_thresholds():
    d = np.arange(4 * REL_MAX_DISTANCE)
    b = _rel_bucket_np(d)
    assert np.all(np.diff(b) >= 0) and b[-1] == N_REL_BUCKETS - 1
    return [int(np.argmax(b > j)) for j in range(N_REL_BUCKETS - 1)]


def _bias_of_distance(rb_ref, h, d, thresholds):
    acc = jnp.full(d.shape, rb_ref[N_REL_BUCKETS - 1, h], F32)
    for j in range(N_REL_BUCKETS - 2, -1, -1):
        acc = jnp.where(d < thresholds[j], rb_ref[j, h], acc)
    return acc


def _lane_fold(x, op):
    out = x[:, :LANES]
    for j in range(1, x.shape[1] // LANES):
        out = op(out, x[:, j * LANES:(j + 1) * LANES])
    return out


def _dsa_attn_prompt_kernel(rb_ref, q_ref, k_ref, v_ref, mask_ref, o_ref, tab_ref, lg_ref, kb_ref, vb_ref, acc_ref,
                            l_ref, mx_ref, *, thresholds, unroll):
    h = pl.program_id(1)
    i = pl.program_id(2)
    per = KC // AQ
    part = i % per
    c_diag = i // per

    @pl.when(i == 0)
    def _():
        kb_ref[...] = k_ref[...].astype(BF16)
        vb_ref[...] = v_ref[...].astype(BF16)
        row = lax.broadcasted_iota(I32, (AQ, KC), 0)
        col = lax.broadcasted_iota(I32, (AQ, KC), 1)
        for p in range(per):
            for back in range(2):
                tab_ref[3 * p + back] = LOG2E * _bias_of_distance(rb_ref, h, p * AQ + back * KC + row - col,
                                                                  thresholds)
            tab_ref[3 * p + 2] = jnp.full((AQ, KC), LOG2E * rb_ref[N_REL_BUCKETS - 1, h], F32)

    q = q_ref[...].astype(BF16)

    n_chunks = c_diag + 1

    def over_chunks(chunk_fn):
        def trip(t, carry):
            for u in range(unroll):
                chunk_fn(t * unroll + u)
            return carry

        lax.fori_loop(0, n_chunks // unroll, trip, 0)
        base = (n_chunks // unroll) * unroll
        part_size = unroll // 2
        while part_size >= 1:
            has_part = ((n_chunks - base) & part_size) != 0

            @pl.when(has_part)
            def _(base=base, part_size=part_size):
                for u in range(part_size):
                    chunk_fn(base + u)

            base = base + jnp.where(has_part, part_size, 0)
            part_size //= 2

    def logits_chunk(c):
        kc = kb_ref[pl.ds(pl.multiple_of(c * KC, KC), KC), :]
        s = _dot_nt(q, kc) * (ATTN_SCALE * LOG2E) + tab_ref[3 * part + jnp.minimum(c_diag - c, 2)]
        s = s + mask_ref[c].reshape(AQ, KC).astype(F32)
        lg_ref[c] = s
        mx_ref[...] = jnp.maximum(mx_ref[...], _lane_fold(s, jnp.maximum))

    mx_ref[...] = jnp.full((AQ, LANES), NEG_INF, F32)
    over_chunks(logits_chunk)
    m = jnp.max(mx_ref[...], axis=1, keepdims=True)
    acc_ref[...] = jnp.zeros_like(acc_ref)
    l_ref[...] = jnp.zeros_like(l_ref)

    def pv_chunk(c):
        p = jnp.exp2(lg_ref[c] - m)
        l_ref[...] += _lane_fold(p, jnp.add)
        acc_ref[...] += _dot(p.astype(BF16), vb_ref[pl.ds(pl.multiple_of(c * KC, KC), KC), :])

    over_chunks(pv_chunk)
    o_ref[...] = (acc_ref[...] / jnp.sum(l_ref[...], axis=1, keepdims=True)).astype(o_ref.dtype)


def dsa_attn_prompt(proj, mask, rel_bias, batch, seq):
    nq = seq // AQ
    nk = seq // KC
    hd = HEAD_DIM
    thresholds = _bucket_thresholds()
    assert thresholds[-1] <= KC + 1
    return pl.pallas_call(
        functools.partial(_dsa_attn_prompt_kernel, thresholds=thresholds, unroll=ATTN_UNROLL),
        grid=(batch, N_HEADS, nq),
        in_specs=[pl.BlockSpec(memory_space=pltpu.SMEM),
                  pl.BlockSpec((AQ, hd), lambda b, h, i: (b * nq + i, C_QD // hd + h)),
                  pl.BlockSpec((seq, hd), lambda b, h, i: (b, C_KD // hd + h)),
                  pl.BlockSpec((seq, hd), lambda b, h, i: (b, C_VD // hd + h)),
                  pl.BlockSpec((None, nk, AQ // QB, QB, KC), lambda b, h, i: (b * nq + i, 0, 0, 0, 0))],
        out_specs=pl.BlockSpec((AQ, hd), lambda b, h, i: (b * nq + i, h)),
        out_shape=jax.ShapeDtypeStruct((batch * seq, D_ATT), BF16),
        scratch_shapes=[pltpu.VMEM((3 * (KC // AQ), AQ, KC), F32), pltpu.VMEM((nk, AQ, KC), F32),
                        pltpu.VMEM((seq, hd), BF16), pltpu.VMEM((seq, hd), BF16),
                        pltpu.VMEM((AQ, hd), F32), pltpu.VMEM((AQ, LANES), F32), pltpu.VMEM((AQ, LANES), F32)],
        compiler_params=_cparams(3),
        name="dsa_attn_prompt",
    )(rel_bias, proj, proj, proj, mask)


SB_T = 256
SB_HEADS = 4
EXP_ZERO_BELOW = -104.0


def _suffix_sum_matrix(n):
    s = np.arange(n)
    tri = (s[:, None] > s[None, :]).astype(np.float32)
    return np.concatenate([tri, np.ones((n, n), np.float32)], axis=1)


def _sb_prompt_kernel(q_ref, k_ref, v_ref, r_ref, o_ref, kb_ref, vb_ref, acc_ref, run_ref):
    i = pl.program_id(2)

    @pl.when(i == 0)
    def _():
        kb_ref[...] = k_ref[...].astype(BF16)
        vb_ref[...] = v_ref[...].astype(BF16)

    q = q_ref[...].astype(BF16)
    acc_ref[...] = jnp.zeros_like(acc_ref)
    run_ref[...] = jnp.zeros_like(run_ref)
    row = lax.broadcasted_iota(I32, (SB_T, SB_T), 0)
    col = lax.broadcasted_iota(I32, (SB_T, SB_T), 1)

    def chunk(carry):
        step, _ = carry
        c = i - step
        r0 = pl.multiple_of(c * SB_T, SB_T)
        before = col < row + step * SB_T
        live = None
        for hh in range(SB_HEADS):
            cols = slice(hh * HEAD_DIM, (hh + 1) * HEAD_DIM)
            z = _dot_nt(q[:, cols], kb_ref[pl.ds(r0, SB_T), cols]) * ATTN_SCALE
            sp = _softplus(z)
            log_keep = jnp.where(before, -sp, 0.0)
            cs = _split_dot(log_keep, r_ref[...], 2)
            later = cs[:, :SB_T] + run_ref[hh]
            a = jnp.where(before, jnp.exp(z - sp + later), 0.0)
            acc_ref[:, cols] += _dot(a.astype(BF16), vb_ref[pl.ds(r0, SB_T), cols])
            run = run_ref[hh] + cs[:, SB_T:]
            run_ref[hh] = run
            top = jnp.max(run[:, :LANES])
            live = top if live is None else jnp.maximum(live, top)
        return step + 1, (live >= EXP_ZERO_BELOW).astype(I32)

    lax.while_loop(lambda carry: (carry[0] <= i) & (carry[1] > 0), chunk, (jnp.int32(0), jnp.int32(1)))
    o_ref[...] = acc_ref[...].astype(o_ref.dtype)


def sb_prompt(proj, batch, seq):
    nq = seq // SB_T
    w = SB_HEADS * HEAD_DIM
    r = jnp.asarray(_suffix_sum_matrix(SB_T), BF16)
    return pl.pallas_call(
        _sb_prompt_kernel,
        grid=(batch, N_HEADS // SB_HEADS, nq),
        in_specs=[pl.BlockSpec((SB_T, w), lambda b, h, i: (b * nq + i, C_QS // w + h)),
                  pl.BlockSpec((seq, w), lambda b, h, i: (b, C_KS // w + h)),
                  pl.BlockSpec((seq, w), lambda b, h, i: (b, C_VS // w + h)),
                  pl.BlockSpec((SB_T, 2 * SB_T), lambda b, h, i: (0, 0))],
        out_specs=pl.BlockSpec((SB_T, w), lambda b, h, i: (b * nq + i, h)),
        out_shape=jax.ShapeDtypeStruct((batch * seq, D_ATT), BF16),
        scratch_shapes=[pltpu.VMEM((seq, w), BF16), pltpu.VMEM((seq, w), BF16),
                        pltpu.VMEM((SB_T, w), F32), pltpu.VMEM((SB_HEADS, SB_T, SB_T), F32)],
        compiler_params=_cparams(3),
        name="sb_prompt",
    )(proj, proj, proj, r)


TOK_PAD = SUBLANES


def _pad_rows(x, rows):
    return jnp.concatenate([x, jnp.zeros((rows - x.shape[0], x.shape[1]), x.dtype)], axis=0)


def _head_rows(page_ref, h, page_size):
    return page_ref[pl.ds(h, page_size, stride=N_HEADS), :].astype(BF16)


def _page_specs(n_pages, rows, cols):
    return [pl.BlockSpec((None, rows, cols), functools.partial(lambda b, pt, p: (pt[b, p], 0, 0), p=p))
            for p in range(n_pages)]


def _sb_sample_kernel(pt_ref, q_ref, kn_ref, vn_ref, r_ref, k_hbm, v_hbm, o_ref, kbuf, vbuf, sems, run_ref, acc_ref,
                      *, n_pages, page_size, n_new, n_pre):
    b = pl.program_id(0)
    grp = N_HEADS * TOK_PAD
    spare = 2 * n_pre

    def page_copies(seq, page, slot):
        phys = pt_ref[seq, page]
        return (pltpu.make_async_copy(k_hbm.at[phys], kbuf.at[slot], sems.at[0, slot]),
                pltpu.make_async_copy(v_hbm.at[phys], vbuf.at[slot], sems.at[1, slot]))

    def prefetch(seq, action):
        for j in range(n_pre):
            for cp in page_copies(seq, n_pages - 1 - j, (seq % 2) * n_pre + j):
                action(cp)

    @pl.when(b == 0)
    def _():
        prefetch(b, lambda cp: cp.start())

    @pl.when(b + 1 < pl.num_programs(0))
    def _():
        prefetch(b + 1, lambda cp: cp.start())

    q = [q_ref[:, h * HEAD_DIM:(h + 1) * HEAD_DIM].astype(BF16) for h in range(N_HEADS)]
    row = lax.broadcasted_iota(I32, (grp, page_size), 0) % TOK_PAD
    lane = lax.broadcasted_iota(I32, (grp, page_size), 1)
    prefetch(b, lambda cp: cp.wait())

    def process(k_h, v_h, before, run, acc):
        z = jnp.concatenate([_dot_nt(q[h], k_h[h]) for h in range(N_HEADS)], axis=0) * ATTN_SCALE
        sp = _softplus(z)
        log_keep = -sp if before is None else jnp.where(before, -sp, 0.0)
        cs = _split_dot(log_keep, r_ref[...], 3)
        e = jnp.exp(z - sp + cs[:, :page_size] + run)
        a = e if before is None else jnp.where(before, e, 0.0)
        run = run + cs[:, page_size:]
        acc = [acc[h] + _dot(a[h * TOK_PAD:(h + 1) * TOK_PAD, :].astype(BF16), v_h[h]) for h in range(N_HEADS)]
        live = (jnp.max(jnp.where(row < n_new, run, NEG_INF)) >= EXP_ZERO_BELOW).astype(I32)
        return run, acc, live

    def process_slot(slot, run, acc):
        k_h = [kbuf[slot, pl.ds(h, page_size, stride=N_HEADS), :].astype(BF16) for h in range(N_HEADS)]
        v_h = [vbuf[slot, pl.ds(h, page_size, stride=N_HEADS), :].astype(BF16) for h in range(N_HEADS)]
        return process(k_h, v_h, None, run, acc)

    run, acc, live = process(
        [_pad_rows(kn_ref[:, h * HEAD_DIM:(h + 1) * HEAD_DIM], page_size).astype(BF16) for h in range(N_HEADS)],
        [_pad_rows(vn_ref[:, h * HEAD_DIM:(h + 1) * HEAD_DIM], page_size).astype(BF16) for h in range(N_HEADS)],
        lane < row, jnp.zeros((grp, page_size), F32), [jnp.zeros((TOK_PAD, HEAD_DIM), F32)] * N_HEADS)
    for j in range(n_pre):
        run, acc, live = process_slot((b % 2) * n_pre + j, run, acc)
    run_ref[...] = run
    for h in range(N_HEADS):
        acc_ref[h * TOK_PAD:(h + 1) * TOK_PAD, :] = acc[h]

    def fetch_and_process(carry):
        page, _ = carry
        for cp in page_copies(b, page, spare):
            cp.start()
        for cp in page_copies(b, page, spare):
            cp.wait()
        run, acc, live = process_slot(spare, run_ref[...],
                                      [acc_ref[h * TOK_PAD:(h + 1) * TOK_PAD, :] for h in range(N_HEADS)])
        run_ref[...] = run
        for h in range(N_HEADS):
            acc_ref[h * TOK_PAD:(h + 1) * TOK_PAD, :] = acc[h]
        return page - 1, live

    lax.while_loop(lambda carry: (carry[0] >= 0) & (carry[1] > 0), fetch_and_process,
                   (jnp.int32(n_pages - n_pre - 1), live))
    for h in range(N_HEADS):
        o_ref[:, h * HEAD_DIM:(h + 1) * HEAD_DIM] = acc_ref[h * TOK_PAD:(h + 1) * TOK_PAD, :]


def sb_sample(proj8, k_pool, v_pool, page_table, n_new):
    n_seq = proj8.shape[0]
    n_pages = page_table.shape[1]
    page_size = k_pool.shape[1] // N_HEADS
    assert page_size == LANES
    n_pre = min(2, n_pages)
    r = jnp.asarray(_suffix_sum_matrix(page_size), BF16)
    tok = lambda col: pl.BlockSpec((None, TOK_PAD, D_ATT), lambda b, pt: (b, 0, col // D_ATT))
    page_buf = pltpu.VMEM((2 * n_pre + 1, page_size * N_HEADS, HEAD_DIM), F32)
    grid_spec = pltpu.PrefetchScalarGridSpec(
        num_scalar_prefetch=1,
        grid=(n_seq,),
        in_specs=[tok(C_QS), tok(C_KS), tok(C_VS), pl.BlockSpec((page_size, 2 * page_size), lambda b, pt: (0, 0)),
                  pl.BlockSpec(memory_space=pl.ANY), pl.BlockSpec(memory_space=pl.ANY)],
        out_specs=pl.BlockSpec((None, TOK_PAD, D_ATT), lambda b, pt: (b, 0, 0)),
        scratch_shapes=[page_buf, page_buf, pltpu.SemaphoreType.DMA((2, 2 * n_pre + 1)),
                        pltpu.VMEM((N_HEADS * TOK_PAD, page_size), F32), pltpu.VMEM((N_HEADS * TOK_PAD, HEAD_DIM), F32)],
    )
    return pl.pallas_call(
        functools.partial(_sb_sample_kernel, n_pages=n_pages, page_size=page_size, n_new=n_new, n_pre=n_pre),
        grid_spec=grid_spec,
        out_shape=jax.ShapeDtypeStruct((n_seq, TOK_PAD, D_ATT), F32),
        compiler_params=_cparams(1),
        name="sb_sample",
    )(page_table, proj8, proj8, proj8, r, k_pool, v_pool)


def _dsa_scores_sample_kernel(pt_ref, qi_ref, w_ref, kin_ref, *rest, n_pages, page_size, n_new):
    ki_pages = rest[:n_pages]
    o_ref = rest[n_pages]
    qi = qi_ref[...].astype(BF16)
    w = w_ref[...] * IDX_W_SCALE
    o_ref[...] = jnp.zeros_like(o_ref)
    for g in range(n_pages + 1):
        ki_t = (kin_ref if g == n_pages else ki_pages[g])[...].astype(BF16)
        s = jnp.maximum(_dot(qi, ki_t), 0.0) * w
        s = jnp.sum(s.reshape(n_new, N_IDX_HEADS, page_size), axis=1)
        o_ref[0:n_new, g * page_size:(g + 1) * page_size] = s


def dsa_scores_sample(qi3, w_rep, ki_new_t, ki_pool_t, page_table):
    n_seq = qi3.shape[0]
    n_new = qi3.shape[1] // N_IDX_HEADS
    n_pages = page_table.shape[1]
    page_size = ki_pool_t.shape[2]
    assert page_size == LANES
    seq_blk = lambda r, c: pl.BlockSpec((None, r, c), lambda b, pt: (b, 0, 0))
    grid_spec = pltpu.PrefetchScalarGridSpec(
        num_scalar_prefetch=1,
        grid=(n_seq,),
        in_specs=[seq_blk(n_new * N_IDX_HEADS, IDX_DIM), seq_blk(n_new * N_IDX_HEADS, LANES),
                  seq_blk(IDX_DIM, page_size)] + _page_specs(n_pages, IDX_DIM, page_size),
        out_specs=seq_blk(TOK_PAD, (n_pages + 1) * page_size),
    )
    return pl.pallas_call(
        functools.partial(_dsa_scores_sample_kernel, n_pages=n_pages, page_size=page_size, n_new=n_new),
        grid_spec=grid_spec,
        out_shape=jax.ShapeDtypeStruct((n_seq, TOK_PAD, (n_pages + 1) * page_size), F32),
        compiler_params=_cparams(1),
        name="dsa_scores_sample",
    )(page_table, qi3, w_rep, ki_new_t, *([ki_pool_t] * n_pages))


MASK_ROWS = 256


def _dsa_mask_sample_kernel(s_ref, ones_ref, m_ref, *, n_past, n_new, top_k):
    n_chunks = s_ref.shape[1] // LANES
    shape = (s_ref.shape[0], LANES)
    tok = lax.broadcasted_iota(I32, shape, 0) % TOK_PAD
    lane = lax.broadcasted_iota(I32, shape, 1)
    ones = ones_ref[...]
    keys, pos, exists = [], [], []
    for c in range(n_chunks):
        s = s_ref[:, c * LANES:(c + 1) * LANES]
        k_pos = c * LANES + lane
        s = jnp.where(k_pos <= n_past + tok, s, NEG_INF)
        keys.append(_float_key(s))
        pos.append(k_pos)
        exists.append(None if (c + 1) * LANES <= n_past + n_new else k_pos < n_past + n_new)

    def count(pred_fn):
        part = jnp.zeros(shape, F32)
        for c in range(n_chunks):
            p = pred_fn(keys[c], pos[c])
            if exists[c] is not None:
                p = p & exists[c]
            part = part + jnp.where(p, 1.0, 0.0)
        return _dot(part.astype(BF16), ones)

    base = jnp.where(count(lambda k, p: k >= 0) >= top_k, 0, INT_MIN).astype(I32)

    def bit_body(t, base):
        cand = base | lax.shift_left(jnp.int32(1), 30 - t)
        return jnp.where(count(lambda k, p: k >= cand) >= top_k, cand, base)

    v = lax.fori_loop(0, 31, bit_body, base)
    need = top_k - count(lambda k, p: k > v)

    n_bits = (n_chunks * LANES - 1).bit_length()

    def idx_bit(t, m):
        cand = m | lax.shift_left(jnp.int32(1), n_bits - 1 - t)
        below = count(lambda k, p: (k == v) & (p < cand))
        return jnp.where(below < need, cand, m)

    mstar = lax.fori_loop(0, n_bits, idx_bit, jnp.zeros(shape, I32))
    for c in range(n_chunks):
        sel = (keys[c] > v) | ((keys[c] == v) & (pos[c] <= mstar))
        sel = sel & (pos[c] <= n_past + tok) & (tok < n_new)
        if exists[c] is not None:
            sel = sel & exists[c]
        m_ref[:, c * LANES:(c + 1) * LANES] = jnp.where(sel, 1.0, 0.0)


def dsa_mask_sample(scores2, n_past, n_new):
    rows, width = scores2.shape
    top_k = min(TOP_K_MAX, (n_past + n_new) // 4)
    tr = min(MASK_ROWS, rows)
    return pl.pallas_call(
        functools.partial(_dsa_mask_sample_kernel, n_past=n_past, n_new=n_new, top_k=top_k),
        grid=(rows // tr,),
        in_specs=[pl.BlockSpec((tr, width), lambda i: (i, 0)), pl.BlockSpec((LANES, LANES), lambda i: (0, 0))],
        out_specs=pl.BlockSpec((tr, width), lambda i: (i, 0)),
        out_shape=jax.ShapeDtypeStruct((rows, width), F32),
        compiler_params=_cparams(1),
        name="dsa_mask_sample",
    )(scores2, jnp.ones((LANES, LANES), BF16))


def _sample_bucket_tables(n_new, page_size):
    j = (np.arange(N_HEADS * TOK_PAD) % TOK_PAD)[:, None]
    u = np.arange(page_size)[None, :]
    assert _rel_bucket_np(np.array([page_size + 1]))[0] == N_REL_BUCKETS - 1
    return np.stack([_rel_bucket_np(page_size + j - u), _rel_bucket_np(j - u)]).astype(np.int32)


def _dsa_attn_sample_kernel(pt_ref, rb_ref, q_ref, kn_ref, vn_ref, m_ref, bkt_ref, *rest, n_pages, page_size, n_new):
    k_pages = rest[:n_pages]
    v_pages = rest[n_pages:2 * n_pages]
    o_ref = rest[2 * n_pages]
    tab_ref = rest[2 * n_pages + 1]
    lg_ref = rest[2 * n_pages + 2]
    grp = N_HEADS * TOK_PAD

    @pl.when(pl.program_id(0) == 0)
    def _():
        for h in range(N_HEADS):
            rows = slice(h * TOK_PAD, (h + 1) * TOK_PAD)
            for t in range(2):
                bk = bkt_ref[t, rows, :]
                acc = jnp.zeros(bk.shape, F32)
                for b in range(N_REL_BUCKETS):
                    acc = jnp.where(bk == b, rb_ref[b, h], acc)
                tab_ref[t, rows, :] = acc
            tab_ref[2, rows, :] = jnp.full((TOK_PAD, page_size), rb_ref[N_REL_BUCKETS - 1, h], F32)

    q = [q_ref[:, h * HEAD_DIM:(h + 1) * HEAD_DIM].astype(BF16) for h in range(N_HEADS)]
    mx = jnp.full((grp, page_size), NEG_INF, F32)
    for g in range(n_pages + 1):
        if g == n_pages:
            k_h = [_pad_rows(kn_ref[:, h * HEAD_DIM:(h + 1) * HEAD_DIM], page_size).astype(BF16) for h in range(N_HEADS)]
            bias = tab_ref[1]
        else:
            k_h = [_head_rows(k_pages[g], h, page_size) for h in range(N_HEADS)]
            bias = tab_ref[0] if g == n_pages - 1 else tab_ref[2]
        s = jnp.concatenate([_dot_nt(q[h], k_h[h]) for h in range(N_HEADS)], axis=0) * ATTN_SCALE + bias
        sel = jnp.tile(m_ref[:, g * page_size:(g + 1) * page_size], (N_HEADS, 1))
        s = jnp.where(sel > 0.5, s, NEG_INF)
        lg_ref[g] = s
        mx = jnp.maximum(mx, s)
    m = jnp.max(mx, axis=1, keepdims=True)
    l = jnp.zeros((grp, page_size), F32)
    acc = [jnp.zeros((TOK_PAD, HEAD_DIM), F32) for _ in range(N_HEADS)]
    for g in range(n_pages + 1):
        p = jnp.exp(lg_ref[g] - m)
        l = l + p
        for h in range(N_HEADS):
            if g == n_pages:
                v_h = _pad_rows(vn_ref[:, h * HEAD_DIM:(h + 1) * HEAD_DIM], page_size).astype(BF16)
            else:
                v_h = _head_rows(v_pages[g], h, page_size)
            acc[h] = acc[h] + _dot(p[h * TOK_PAD:(h + 1) * TOK_PAD, :].astype(BF16), v_h)
    inv = 1.0 / jnp.sum(l, axis=1, keepdims=True)
    for h in range(N_HEADS):
        o_ref[:, h * HEAD_DIM:(h + 1) * HEAD_DIM] = acc[h] * inv[h * TOK_PAD:(h + 1) * TOK_PAD, :]


def dsa_attn_sample(proj8, mask3, rel_bias, k_pool, v_pool, page_table, n_new):
    n_seq = proj8.shape[0]
    n_pages = page_table.shape[1]
    page_size = k_pool.shape[1] // N_HEADS
    assert page_size == LANES
    grp = N_HEADS * TOK_PAD
    width = (n_pages + 1) * page_size
    tok = lambda col: pl.BlockSpec((None, TOK_PAD, D_ATT), lambda b, pt: (b, 0, col // D_ATT))
    grid_spec = pltpu.PrefetchScalarGridSpec(
        num_scalar_prefetch=1,
        grid=(n_seq,),
        in_specs=[pl.BlockSpec(memory_space=pltpu.SMEM), tok(C_QD), tok(C_KD), tok(C_VD),
                  pl.BlockSpec((None, TOK_PAD, width), lambda b, pt: (b, 0, 0)),
                  pl.BlockSpec((2, grp, page_size), lambda b, pt: (0, 0, 0))]
                 + _page_specs(n_pages, page_size * N_HEADS, HEAD_DIM)
                 + _page_specs(n_pages, page_size * N_HEADS, HEAD_DIM),
        out_specs=pl.BlockSpec((None, TOK_PAD, D_ATT), lambda b, pt: (b, 0, 0)),
        scratch_shapes=[pltpu.VMEM((3, grp, page_size), F32), pltpu.VMEM((n_pages + 1, grp, page_size), F32)],
    )
    return pl.pallas_call(
        functools.partial(_dsa_attn_sample_kernel, n_pages=n_pages, page_size=page_size, n_new=n_new),
        grid_spec=grid_spec,
        out_shape=jax.ShapeDtypeStruct((n_seq, TOK_PAD, D_ATT), F32),
        compiler_params=_cparams(1, vmem_mb=56),
        name="dsa_attn_sample",
    )(page_table, rel_bias, proj8, proj8, proj8, mask3, jnp.asarray(_sample_bucket_tables(n_new, page_size)),
      *([k_pool] * n_pages), *([v_pool] * n_pages))


def _tail(x, proj, o_a, o_b, weights, gains):
    w1, w2, wo, wu, wd = weights
    g_attn_post, g_mlp_pre, g_mlp_post = gains
    tm = min(512, x.shape[0])
    y = merge_branches(o_a, o_b, w1, w2, proj, min(1024, x.shape[0]))
    x1, h2 = proj_norm(y, wo, x, g_attn_post, g_mlp_pre)
    return mlp(h2, wu, wd, x1, g_mlp_post, tm)


def kernel(x_prompt, x_sample, cache_k_dsa, cache_v_dsa, cache_k_idx, cache_k_sb, cache_v_sb, page_table, rel_bias,
           w_in, w_out_dsa, w_out_sb, w_o, w_up, w_down, g_attn_pre, g_attn_post, g_mlp_pre, g_mlp_post):
    batch, seq, d_model = x_prompt.shape
    n_seq, n_new, _ = x_sample.shape
    depth = w_in.shape[0]
    n_pool, page_size = cache_k_idx.shape[1], cache_k_idx.shape[2]
    n_past = page_table.shape[1] * page_size

    xp = x_prompt.reshape(batch * seq, d_model)
    xs = jnp.pad(x_sample, ((0, 0), (0, TOK_PAD - n_new), (0, 0))).reshape(n_seq * TOK_PAD, d_model)
    rows_p, rows_s = [], []
    for l in range(depth):
        w_in_t = jnp.swapaxes(w_in[l], 0, 1)
        weights = tuple(w[l].astype(BF16) for w in (w_out_dsa, w_out_sb, w_o, w_up, w_down))
        gains = (g_attn_post[l], g_mlp_pre[l], g_mlp_post[l])

        hp = rms_cast(xp, g_attn_pre[l])
        proj, c_small = in_proj(hp, w_in_t, tm=min(1024, hp.shape[0]))
        mask = dsa_mask_prompt(proj, batch, seq, c_small)
        o_a = dsa_attn_prompt(proj, mask, rel_bias, batch, seq)
        o_b = sb_prompt(proj, batch, seq)
        xp_new = _tail(xp, proj, o_a, o_b, weights, gains)
        heads = lambda c: proj[:, c:c + D_ATT].reshape(batch, seq, N_HEADS, HEAD_DIM)
        rows_p.append((heads(C_KD), heads(C_VD), proj[:, c_small:c_small + IDX_DIM].reshape(batch, seq, IDX_DIM),
                       heads(C_KS), heads(C_VS)))
        xp = xp_new

        hs = rms_cast(xs, g_attn_pre[l])
        proj_s, _ = in_proj(hs, w_in_t, tm=min(1024, hs.shape[0]))
        proj8 = proj_s.reshape(n_seq, TOK_PAD, proj_s.shape[1])
        real = lambda c, w: proj8[:, :n_new, c:c + w]
        qi3 = real(C_QI, N_IDX_HEADS * IDX_DIM).reshape(n_seq, n_new * N_IDX_HEADS, IDX_DIM)
        w_rep = jnp.broadcast_to(real(c_small + IDX_DIM, N_IDX_HEADS).reshape(n_seq, n_new * N_IDX_HEADS, 1),
                                 (n_seq, n_new * N_IDX_HEADS, LANES))
        ki_new = real(c_small, IDX_DIM)
        ki_new_t = jnp.pad(jnp.swapaxes(ki_new, 1, 2), ((0, 0), (0, 0), (0, page_size - n_new)))
        pool2 = lambda c: c[l].reshape(n_pool, page_size * N_HEADS, HEAD_DIM)
        scores = dsa_scores_sample(qi3, w_rep, ki_new_t, jnp.swapaxes(cache_k_idx[l], 1, 2), page_table)
        mask_s = dsa_mask_sample(scores.reshape(n_seq * TOK_PAD, scores.shape[2]), n_past, n_new)
        o_a = dsa_attn_sample(proj8, mask_s.reshape(scores.shape), rel_bias, pool2(cache_k_dsa), pool2(cache_v_dsa),
                              page_table, n_new)
        o_b = sb_sample(proj8, pool2(cache_k_sb), pool2(cache_v_sb), page_table, n_new)
        flat = lambda o: o.reshape(n_seq * TOK_PAD, D_ATT)
        xs_new = _tail(xs, proj_s, flat(o_a), flat(o_b), weights, gains)
        heads_s = lambda c: real(c, D_ATT).reshape(n_seq, n_new, N_HEADS, HEAD_DIM)
        rows_s.append((heads_s(C_KD), heads_s(C_VD), ki_new, heads_s(C_KS), heads_s(C_VS)))
        xs = xs_new

    outs_p = [jnp.stack(r, axis=0) for r in zip(*rows_p)]
    outs_s = [jnp.stack(r, axis=0) for r in zip(*rows_s)]
    return (xp.reshape(batch, seq, d_model), xs.reshape(n_seq, TOK_PAD, d_model)[:, :n_new], *outs_p, *outs_s)
```

```python
import functools
import math

import numpy as np
import jax
import jax.numpy as jnp
from jax import lax
from jax.experimental import pallas as pl
from jax.experimental.pallas import tpu as pltpu

F32 = jnp.float32
BF16 = jnp.bfloat16
I32 = jnp.int32

HEAD_DIM = 128
N_HEADS = 8
N_IDX_HEADS = 16
IDX_DIM = 64
IDX_W_SCALE = (N_IDX_HEADS * IDX_DIM) ** -0.5
TOP_K_MAX = 256
N_REL_BUCKETS = 32
REL_MAX_DISTANCE = 128
RMS_EPS = 1e-6
NEG_INF = -1e30
ATTN_SCALE = HEAD_DIM ** -0.5
LOG2E = math.log2(math.e)

LANES = 128
SUBLANES = 8
D_ATT = N_HEADS * HEAD_DIM

C_QD, C_KD, C_VD, C_QI, C_QS, C_KS, C_VS = (k * D_ATT for k in range(7))
C_GA = 7 * D_ATT
PROJ_TN = 1024


def _key_of(x):
    b = int(np.float32(x).view(np.int32))
    return b if b >= 0 else b ^ 0x7FFFFFFF


KEY_NEG_INF = _key_of(NEG_INF)
INT_MIN = -(2 ** 31)


def _cparams(n_axes, vmem_mb=48):
    return pltpu.CompilerParams(dimension_semantics=("arbitrary",) * n_axes,
                                vmem_limit_bytes=vmem_mb * 1024 * 1024)


def _dot(a, b):
    return jnp.dot(a, b, preferred_element_type=F32)


def _dot_nt(a, b):
    return lax.dot_general(a, b, (((1,), (1,)), ((), ())), preferred_element_type=F32)


def _split_dot(x, m01, passes):
    out = None
    rem = x
    for p in range(passes):
        piece = rem.astype(BF16)
        d = _dot(piece, m01)
        out = d if out is None else out + d
        if p + 1 < passes:
            rem = rem - piece.astype(F32)
    return out


def _softplus(z):
    return jnp.maximum(z, 0.0) + jnp.log1p(jnp.exp(-jnp.abs(z)))


def _float_key(s):
    s = jnp.where(s == 0.0, 0.0, s)
    bits = lax.bitcast_convert_type(s, I32)
    return jnp.where(bits >= 0, bits, bits ^ 0x7FFFFFFF)


def _rel_bucket_np(dist):
    max_exact = N_REL_BUCKETS // 2
    d = np.maximum(dist, 0)
    df = np.maximum(d, 1).astype(np.float32)
    log_part = np.log(df / np.float32(max_exact)) / np.float32(math.log(REL_MAX_DISTANCE / max_exact))
    large = max_exact + (log_part * np.float32(N_REL_BUCKETS - max_exact)).astype(np.int32)
    return np.where(d < max_exact, d, np.minimum(large, N_REL_BUCKETS - 1)).astype(np.int32)


def _rms_cast_kernel(x_ref, g_ref, o_ref):
    x = x_ref[...]
    ms = jnp.mean(x * x, axis=-1, keepdims=True)
    o_ref[...] = (x * lax.rsqrt(ms + RMS_EPS) * g_ref[...]).astype(o_ref.dtype)


def rms_cast(x, g, tm=256):
    m, d = x.shape
    tm = min(tm, m)
    return pl.pallas_call(
        _rms_cast_kernel,
        grid=(m // tm,),
        in_specs=[pl.BlockSpec((tm, d), lambda i: (i, 0)), pl.BlockSpec((1, d), lambda i: (0, 0))],
        out_specs=pl.BlockSpec((tm, d), lambda i: (i, 0)),
        out_shape=jax.ShapeDtypeStruct((m, d), BF16),
        compiler_params=_cparams(1),
        name="rms_cast",
    )(x, g.reshape(1, d))


def _in_proj_kernel(h_ref, wt_ref, o_ref, wb_ref):
    @pl.when(pl.program_id(1) == 0)
    def _():
        wb_ref[...] = wt_ref[...].astype(BF16)

    o_ref[...] = _dot_nt(h_ref[...], wb_ref[...])


def in_proj(h, w_t, tm):
    m, k = h.shape
    in_cols = w_t.shape[0]
    c_ki = 4 * D_ATT
    c_after = c_ki + IDX_DIM + N_IDX_HEADS
    assert c_ki % PROJ_TN == 0 and (in_cols - c_after) % PROJ_TN == 0 and c_ki + PROJ_TN <= in_cols
    n_a = c_ki // PROJ_TN
    n_b = (in_cols - c_after) // PROJ_TN

    assert c_after % SUBLANES == 0 and PROJ_TN % SUBLANES == 0
    tile8, after8, ki8 = PROJ_TN // SUBLANES, c_after // SUBLANES, c_ki // SUBLANES

    def w_row(j):
        return jnp.where(j < n_a, j * tile8, jnp.where(j < n_a + n_b, after8 + (j - n_a) * tile8, ki8)) * SUBLANES

    n_tiles = n_a + n_b + 1
    proj = pl.pallas_call(
        _in_proj_kernel,
        grid=(n_tiles, m // tm),
        in_specs=[pl.BlockSpec((tm, k), lambda j, i: (i, 0)),
                  pl.BlockSpec((pl.Element(PROJ_TN), pl.Element(k)), lambda j, i: (w_row(j), 0))],
        out_specs=pl.BlockSpec((tm, PROJ_TN), lambda j, i: (i, j)),
        out_shape=jax.ShapeDtypeStruct((m, n_tiles * PROJ_TN), F32),
        scratch_shapes=[pltpu.VMEM((PROJ_TN, k), BF16)],
        compiler_params=_cparams(2),
        name="in_proj",
    )(h, w_t)
    return proj, (n_a + n_b) * PROJ_TN


def _merge_kernel(oa_ref, ob_ref, w1_ref, w2_ref, ga_ref, gb_ref, y_ref):
    ya = _dot(oa_ref[...].astype(BF16), w1_ref[...])
    yb = _dot(ob_ref[...].astype(BF16), w2_ref[...])
    sa = 1.0 / (1.0 + jnp.exp(-ga_ref[...]))
    sb = 1.0 / (1.0 + jnp.exp(-gb_ref[...]))
    y_ref[...] = (sa * ya + sb * yb).astype(y_ref.dtype)


def merge_branches(o_a, o_b, w1, w2, proj, tm, tn=1024):
    m = o_a.shape[0]
    d = w1.shape[1]
    tn = min(tn, d)
    ga0 = C_GA // tn
    gb0 = (C_GA + d) // tn
    return pl.pallas_call(
        _merge_kernel,
        grid=(d // tn, m // tm),
        in_specs=[pl.BlockSpec((tm, D_ATT), lambda j, i: (i, 0)),
                  pl.BlockSpec((tm, D_ATT), lambda j, i: (i, 0)),
                  pl.BlockSpec((D_ATT, tn), lambda j, i: (0, j)),
                  pl.BlockSpec((D_ATT, tn), lambda j, i: (0, j)),
                  pl.BlockSpec((tm, tn), lambda j, i: (i, ga0 + j)),
                  pl.BlockSpec((tm, tn), lambda j, i: (i, gb0 + j))],
        out_specs=pl.BlockSpec((tm, tn), lambda j, i: (i, j)),
        out_shape=jax.ShapeDtypeStruct((m, d), BF16),
        compiler_params=_cparams(2),
        name="merge",
    )(o_a, o_b, w1, w2, proj, proj)


def _proj_norm_kernel(y_ref, wo_ref, x_ref, gpost_ref, gpre_ref, x1_ref, h2_ref, *, parts):
    rows_per = y_ref.shape[0] // parts
    for p in range(parts):
        rows = slice(p * rows_per, (p + 1) * rows_per)
        a = _dot(y_ref[rows, :], wo_ref[...])
        r = a * lax.rsqrt(jnp.mean(a * a, axis=-1, keepdims=True) + RMS_EPS) * gpost_ref[...]
        x1 = x_ref[rows, :] + r
        x1_ref[rows, :] = x1
        h2 = x1 * lax.rsqrt(jnp.mean(x1 * x1, axis=-1, keepdims=True) + RMS_EPS) * gpre_ref[...]
        h2_ref[rows, :] = h2.astype(h2_ref.dtype)


def proj_norm(y, wo, x, g_post, g_pre, tm=512):
    m, d = x.shape
    tm = min(tm, m)
    parts = 2 if tm % (2 * 2 * SUBLANES) == 0 else 1
    row = lambda i: (i, 0)
    fix = lambda i: (0, 0)
    return pl.pallas_call(
        functools.partial(_proj_norm_kernel, parts=parts),
        grid=(m // tm,),
        in_specs=[pl.BlockSpec((tm, d), row), pl.BlockSpec((d, d), fix), pl.BlockSpec((tm, d), row),
                  pl.BlockSpec((1, d), fix), pl.BlockSpec((1, d), fix)],
        out_specs=[pl.BlockSpec((tm, d), row), pl.BlockSpec((tm, d), row)],
        out_shape=[jax.ShapeDtypeStruct((m, d), F32), jax.ShapeDtypeStruct((m, d), BF16)],
        compiler_params=_cparams(1),
        name="proj_norm",
    )(y, wo, x, g_post.reshape(1, d), g_pre.reshape(1, d))


def _mlp_kernel(h_ref, wu_ref, wd_ref, x_ref, g_ref, o_ref, acc_ref):
    k = pl.program_id(1)

    @pl.when(k == 0)
    def _():
        acc_ref[...] = jnp.zeros_like(acc_ref)

    u = jnp.maximum(_dot(h_ref[...], wu_ref[...]), 0.0)
    acc_ref[...] += _dot((u * u).astype(BF16), wd_ref[...])

    @pl.when(k == pl.num_programs(1) - 1)
    def _():
        a = acc_ref[...]
        o_ref[...] = x_ref[...] + a * lax.rsqrt(jnp.mean(a * a, axis=-1, keepdims=True) + RMS_EPS) * g_ref[...]


MLP_TM = 512
MLP_ONE_TILE_ROWS = 1024


def mlp(h2, w_up, w_down, x1, g_post):
    m, d = x1.shape
    ff = w_up.shape[1]
    if m <= MLP_ONE_TILE_ROWS:
        tm, tk, vmem_mb = m, min(512, ff), 58
    else:
        tm, tk, vmem_mb = MLP_TM, min(1024, ff), 48
    return pl.pallas_call(
        _mlp_kernel,
        grid=(m // tm, ff // tk),
        in_specs=[pl.BlockSpec((tm, d), lambda i, k: (i, 0)),
                  pl.BlockSpec((d, tk), lambda i, k: (0, k)),
                  pl.BlockSpec((tk, d), lambda i, k: (k, 0)),
                  pl.BlockSpec((tm, d), lambda i, k: (i, 0)),
                  pl.BlockSpec((1, d), lambda i, k: (0, 0))],
        out_specs=pl.BlockSpec((tm, d), lambda i, k: (i, 0)),
        out_shape=jax.ShapeDtypeStruct((m, d), F32),
        scratch_shapes=[pltpu.VMEM((tm, d), F32)],
        compiler_params=_cparams(2, vmem_mb=vmem_mb),
        name="mlp",
    )(h2, w_up, w_down, x1, g_post.reshape(1, d))


QB = LANES
KC = 512
SCORE_KC = 512
AQ = 256
ATTN_UNROLL = 4


def _dsa_mask_prompt_kernel(qi_ref, sm_all_ref, sm_blk_ref, mask_ref, kia_ref, kib_ref, keys_ref, mstar_ref, qq_ref,
                            *, seq, top_k):
    i = pl.program_id(1)
    n_chunks = (i * QB) // KC + 1

    @pl.when(i == 0)
    def _():
        sm = sm_all_ref[...]
        lane = lax.broadcasted_iota(I32, sm.shape, 1)
        kia_ref[...] = jnp.where(lane < IDX_DIM, sm, 0.0).astype(BF16)
        kib_ref[...] = jnp.where(lane >= IDX_DIM, pltpu.roll(sm, IDX_DIM, axis=1), 0.0).astype(BF16)

    w_t = sm_blk_ref[...].T * IDX_W_SCALE
    qi = qi_ref[...].astype(BF16)
    for g in range(N_IDX_HEADS // 4):
        qq_ref[g] = jnp.concatenate([qi[:, (2 * g) * LANES:(2 * g + 1) * LANES],
                                     qi[:, (2 * g + 1) * LANES:(2 * g + 2) * LANES]], axis=0)
    q_pos_s = i * QB + lax.broadcasted_iota(I32, (SCORE_KC, QB), 1)
    row_s = lax.broadcasted_iota(I32, (SCORE_KC, QB), 0)

    def score_chunk(c, carry):
        r0 = pl.multiple_of(c * SCORE_KC, SCORE_KC)
        ka = kia_ref[pl.ds(r0, SCORE_KC), :]
        kb = kib_ref[pl.ds(r0, SCORE_KC), :]
        acc = jnp.zeros((SCORE_KC, QB), F32)
        for g in range(N_IDX_HEADS // 4):
            qq = qq_ref[g]
            for odd, k_half in ((0, ka), (1, kb)):
                s = jnp.maximum(_dot_nt(k_half, qq), 0.0)
                for j in range(2):
                    head = 4 * g + 2 * j + odd
                    acc = acc + s[:, j * LANES:(j + 1) * LANES] * w_t[IDX_DIM + head:IDX_DIM + head + 1, :]
        sc = jnp.where(c * SCORE_KC + row_s <= q_pos_s, acc, NEG_INF)
        keys_ref[pl.ds(r0, SCORE_KC), :] = _float_key(sc)
        return carry

    lax.fori_loop(0, n_chunks * (KC // SCORE_KC), score_chunk, 0)

    n_tail = seq - n_chunks * KC
    q_pos = i * QB + lax.broadcasted_iota(I32, (KC, QB), 1)
    row = lax.broadcasted_iota(I32, (KC, QB), 0)

    def count(pred_fn):
        def body(c, acc):
            k = keys_ref[pl.ds(pl.multiple_of(c * KC, KC), KC), :]
            ones = jnp.where(pred_fn(k, c), 1, 0)
            return acc + jnp.sum(ones.reshape(KC // SUBLANES, SUBLANES, QB), axis=0)

        acc = lax.fori_loop(0, n_chunks, body, jnp.zeros((SUBLANES, QB), I32))
        return jnp.sum(acc, axis=0, keepdims=True)

    def count_ge(cand):
        return count(lambda k, c: k >= cand) + jnp.where(KEY_NEG_INF >= cand, n_tail, 0)

    base = jnp.where(count_ge(jnp.zeros((1, QB), I32)) >= top_k, 0, INT_MIN).astype(I32)

    def bit_body(t, base):
        cand = base | lax.shift_left(jnp.int32(1), 30 - t)
        return jnp.where(count_ge(cand) >= top_k, cand, base)

    v = lax.fori_loop(0, 31, bit_body, base)

    n_gt = count(lambda k, c: k > v) + jnp.where(KEY_NEG_INF > v, n_tail, 0)
    need = top_k - n_gt
    n_eq = count(lambda k, c: k == v)
    mstar_ref[...] = jnp.full((SUBLANES, QB), seq, I32)

    @pl.when(jnp.max(jnp.where(n_eq > need, 1, 0)) > 0)
    def _():
        def idx_bit(t, m):
            cand = m | lax.shift_left(jnp.int32(1), (seq - 1).bit_length() - 1 - t)
            below = count(lambda k, c: (k == v) & (c * KC + row < cand))
            return jnp.where(below < need, cand, m)

        m = lax.fori_loop(0, (seq - 1).bit_length(), idx_bit, jnp.zeros((1, QB), I32))
        mstar_ref[...] = jnp.broadcast_to(m, (SUBLANES, QB))

    mstar = mstar_ref[0:1, :]

    def emit(c, carry):
        k = keys_ref[pl.ds(pl.multiple_of(c * KC, KC), KC), :]
        k_pos = c * KC + row
        sel = (k > v) | ((k == v) & (k_pos <= mstar))
        m_t = jnp.where(sel & (k_pos <= q_pos), 0.0, NEG_INF)
        mask_ref[c] = m_t.T.astype(mask_ref.dtype)
        return carry

    lax.fori_loop(0, n_chunks, emit, 0)

    def clear(c, carry):
        mask_ref[c] = jnp.full((QB, KC), NEG_INF, mask_ref.dtype)
        return carry

    lax.fori_loop(n_chunks, seq // KC, clear, 0)


def dsa_mask_prompt(proj, batch, seq, c_small):
    assert seq % KC == 0 and KC % AQ == 0 and AQ % QB == 0
    nq = seq // QB
    per = AQ // QB
    top_k = min(TOP_K_MAX, seq // 4)
    return pl.pallas_call(
        functools.partial(_dsa_mask_prompt_kernel, seq=seq, top_k=top_k),
        grid=(batch, nq),
        in_specs=[pl.BlockSpec((QB, N_IDX_HEADS * IDX_DIM), lambda b, i: (b * nq + i, C_QI // (N_IDX_HEADS * IDX_DIM))),
                  pl.BlockSpec((seq, LANES), lambda b, i: (b, c_small // LANES)),
                  pl.BlockSpec((QB, LANES), lambda b, i: (b * nq + i, c_small // LANES))],
        out_specs=pl.BlockSpec((None, seq // KC, None, QB, KC), lambda b, i: ((b * nq + i) // per, 0, i % per, 0, 0)),
        out_shape=jax.ShapeDtypeStruct((batch * seq // AQ, seq // KC, per, QB, KC), BF16),
        scratch_shapes=[pltpu.VMEM((seq, LANES), BF16), pltpu.VMEM((seq, LANES), BF16),
                        pltpu.VMEM((seq, QB), I32), pltpu.VMEM((SUBLANES, QB), I32),
                        pltpu.VMEM((N_IDX_HEADS // 4, 2 * QB, LANES), BF16)],
        compiler_params=_cparams(2),
        name="dsa_mask_prompt",
    )(proj, proj, proj)


def _bucket_thresholds():
    d = np.arange(4 * REL_MAX_DISTANCE)
    b = _rel_bucket_np(d)
    assert np.all(np.diff(b) >= 0) and b[-1] == N_REL_BUCKETS - 1
    return [int(np.argmax(b > j)) for j in range(N_REL_BUCKETS - 1)]


def _bias_of_distance(rb_ref, h, d, thresholds):
    acc = jnp.full(d.shape, rb_ref[N_REL_BUCKETS - 1, h], F32)
    for j in range(N_REL_BUCKETS - 2, -1, -1):
        acc = jnp.where(d < thresholds[j], rb_ref[j, h], acc)
    return acc


def _lane_fold(x, op):
    out = x[:, :LANES]
    for j in range(1, x.shape[1] // LANES):
        out = op(out, x[:, j * LANES:(j + 1) * LANES])
    return out


def _dsa_attn_prompt_kernel(rb_ref, q_ref, k_ref, v_ref, mask_ref, o_ref, tab_ref, lg_ref, kb_ref, vb_ref, acc_ref,
                            l_ref, mx_ref, *, thresholds, unroll):
    h = pl.program_id(1)
    i = pl.program_id(2)
    per = KC // AQ
    part = i % per
    c_diag = i // per

    @pl.when(i == 0)
    def _():
        kb_ref[...] = k_ref[...].astype(BF16)
        vb_ref[...] = v_ref[...].astype(BF16)
        row = lax.broadcasted_iota(I32, (AQ, KC), 0)
        col = lax.broadcasted_iota(I32, (AQ, KC), 1)
        for p in range(per):
            for back in range(2):
                tab_ref[3 * p + back] = LOG2E * _bias_of_distance(rb_ref, h, p * AQ + back * KC + row - col,
                                                                  thresholds)
            tab_ref[3 * p + 2] = jnp.full((AQ, KC), LOG2E * rb_ref[N_REL_BUCKETS - 1, h], F32)

    q = q_ref[...].astype(BF16)

    n_chunks = c_diag + 1

    def over_chunks(chunk_fn):
        def trip(t, carry):
            for u in range(unroll):
                chunk_fn(t * unroll + u)
            return carry

        lax.fori_loop(0, n_chunks // unroll, trip, 0)
        base = (n_chunks // unroll) * unroll
        part_size = unroll // 2
        while part_size >= 1:
            has_part = ((n_chunks - base) & part_size) != 0

            @pl.when(has_part)
            def _(base=base, part_size=part_size):
                for u in range(part_size):
                    chunk_fn(base + u)

            base = base + jnp.where(has_part, part_size, 0)
            part_size //= 2

    def logits_chunk(c):
        kc = kb_ref[pl.ds(pl.multiple_of(c * KC, KC), KC), :]
        s = _dot_nt(q, kc) * (ATTN_SCALE * LOG2E) + tab_ref[3 * part + jnp.minimum(c_diag - c, 2)]
        s = s + mask_ref[c].reshape(AQ, KC).astype(F32)
        lg_ref[c] = s
        mx_ref[...] = jnp.maximum(mx_ref[...], _lane_fold(s, jnp.maximum))

    mx_ref[...] = jnp.full((AQ, LANES), NEG_INF, F32)
    over_chunks(logits_chunk)
    m = jnp.max(mx_ref[...], axis=1, keepdims=True)
    acc_ref[...] = jnp.zeros_like(acc_ref)
    l_ref[...] = jnp.zeros_like(l_ref)

    def pv_chunk(c):
        p = jnp.exp2(lg_ref[c] - m)
        l_ref[...] += _lane_fold(p, jnp.add)
        acc_ref[...] += _dot(p.astype(BF16), vb_ref[pl.ds(pl.multiple_of(c * KC, KC), KC), :])

    over_chunks(pv_chunk)
    o_ref[...] = (acc_ref[...] / jnp.sum(l_ref[...], axis=1, keepdims=True)).astype(o_ref.dtype)


def dsa_attn_prompt(proj, mask, rel_bias, batch, seq):
    nq = seq // AQ
    nk = seq // KC
    hd = HEAD_DIM
    thresholds = _bucket_thresholds()
    assert thresholds[-1] <= KC + 1
    return pl.pallas_call(
        functools.partial(_dsa_attn_prompt_kernel, thresholds=thresholds, unroll=ATTN_UNROLL),
        grid=(batch, N_HEADS, nq),
        in_specs=[pl.BlockSpec(memory_space=pltpu.SMEM),
                  pl.BlockSpec((AQ, hd), lambda b, h, i: (b * nq + i, C_QD // hd + h)),
                  pl.BlockSpec((seq, hd), lambda b, h, i: (b, C_KD // hd + h)),
                  pl.BlockSpec((seq, hd), lambda b, h, i: (b, C_VD // hd + h)),
                  pl.BlockSpec((None, nk, AQ // QB, QB, KC), lambda b, h, i: (b * nq + i, 0, 0, 0, 0))],
        out_specs=pl.BlockSpec((AQ, hd), lambda b, h, i: (b * nq + i, h)),
        out_shape=jax.ShapeDtypeStruct((batch * seq, D_ATT), BF16),
        scratch_shapes=[pltpu.VMEM((3 * (KC // AQ), AQ, KC), F32), pltpu.VMEM((nk, AQ, KC), F32),
                        pltpu.VMEM((seq, hd), BF16), pltpu.VMEM((seq, hd), BF16),
                        pltpu.VMEM((AQ, hd), F32), pltpu.VMEM((AQ, LANES), F32), pltpu.VMEM((AQ, LANES), F32)],
        compiler_params=_cparams(3),
        name="dsa_attn_prompt",
    )(rel_bias, proj, proj, proj, mask)


SB_T = 256
SB_HEADS = 4
EXP_ZERO_BELOW = -104.0


def _suffix_sum_matrix(n):
    s = np.arange(n)
    tri = (s[:, None] > s[None, :]).astype(np.float32)
    return np.concatenate([tri, np.ones((n, n), np.float32)], axis=1)


def _sb_prompt_kernel(q_ref, k_ref, v_ref, r_ref, o_ref, kb_ref, vb_ref, acc_ref, run_ref):
    i = pl.program_id(2)

    @pl.when(i == 0)
    def _():
        kb_ref[...] = k_ref[...].astype(BF16)
        vb_ref[...] = v_ref[...].astype(BF16)

    q = q_ref[...].astype(BF16)
    acc_ref[...] = jnp.zeros_like(acc_ref)
    run_ref[...] = jnp.zeros_like(run_ref)
    row = lax.broadcasted_iota(I32, (SB_T, SB_T), 0)
    col = lax.broadcasted_iota(I32, (SB_T, SB_T), 1)

    def chunk(carry):
        step, _ = carry
        c = i - step
        r0 = pl.multiple_of(c * SB_T, SB_T)
        before = col < row + step * SB_T
        live = None
        for hh in range(SB_HEADS):
            cols = slice(hh * HEAD_DIM, (hh + 1) * HEAD_DIM)
            z = _dot_nt(q[:, cols], kb_ref[pl.ds(r0, SB_T), cols]) * ATTN_SCALE
            sp = _softplus(z)
            log_keep = jnp.where(before, -sp, 0.0)
            cs = _split_dot(log_keep, r_ref[...], 2)
            later = cs[:, :SB_T] + run_ref[hh]
            a = jnp.where(before, jnp.exp(z - sp + later), 0.0)
            acc_ref[:, cols] += _dot(a.astype(BF16), vb_ref[pl.ds(r0, SB_T), cols])
            run = run_ref[hh] + cs[:, SB_T:]
            run_ref[hh] = run
            top = jnp.max(run[:, :LANES])
            live = top if live is None else jnp.maximum(live, top)
        return step + 1, (live >= EXP_ZERO_BELOW).astype(I32)

    lax.while_loop(lambda carry: (carry[0] <= i) & (carry[1] > 0), chunk, (jnp.int32(0), jnp.int32(1)))
    o_ref[...] = acc_ref[...].astype(o_ref.dtype)


def sb_prompt(proj, batch, seq):
    nq = seq // SB_T
    w = SB_HEADS * HEAD_DIM
    r = jnp.asarray(_suffix_sum_matrix(SB_T), BF16)
    return pl.pallas_call(
        _sb_prompt_kernel,
        grid=(batch, N_HEADS // SB_HEADS, nq),
        in_specs=[pl.BlockSpec((SB_T, w), lambda b, h, i: (b * nq + i, C_QS // w + h)),
                  pl.BlockSpec((seq, w), lambda b, h, i: (b, C_KS // w + h)),
                  pl.BlockSpec((seq, w), lambda b, h, i: (b, C_VS // w + h)),
                  pl.BlockSpec((SB_T, 2 * SB_T), lambda b, h, i: (0, 0))],
        out_specs=pl.BlockSpec((SB_T, w), lambda b, h, i: (b * nq + i, h)),
        out_shape=jax.ShapeDtypeStruct((batch * seq, D_ATT), BF16),
        scratch_shapes=[pltpu.VMEM((seq, w), BF16), pltpu.VMEM((seq, w), BF16),
                        pltpu.VMEM((SB_T, w), F32), pltpu.VMEM((SB_HEADS, SB_T, SB_T), F32)],
        compiler_params=_cparams(3),
        name="sb_prompt",
    )(proj, proj, proj, r)


TOK_PAD = SUBLANES


def _pad_rows(x, rows):
    return jnp.concatenate([x, jnp.zeros((rows - x.shape[0], x.shape[1]), x.dtype)], axis=0)


def _head_rows(page_ref, h, page_size):
    return page_ref[pl.ds(h, page_size, stride=N_HEADS), :].astype(BF16)


def _page_specs(n_pages, rows, cols):
    return [pl.BlockSpec((None, rows, cols), functools.partial(lambda b, pt, p: (pt[b, p], 0, 0), p=p))
            for p in range(n_pages)]


def _sb_sample_kernel(pt_ref, q_ref, kn_ref, vn_ref, r_ref, k_hbm, v_hbm, o_ref, kbuf, vbuf, sems, run_ref, acc_ref,
                      *, n_pages, page_size, n_new, n_pre):
    b = pl.program_id(0)
    grp = N_HEADS * TOK_PAD
    spare = 2 * n_pre

    def page_copies(seq, page, slot):
        phys = pt_ref[seq, page]
        return (pltpu.make_async_copy(k_hbm.at[phys], kbuf.at[slot], sems.at[0, slot]),
                pltpu.make_async_copy(v_hbm.at[phys], vbuf.at[slot], sems.at[1, slot]))

    def prefetch(seq, action):
        for j in range(n_pre):
            for cp in page_copies(seq, n_pages - 1 - j, (seq % 2) * n_pre + j):
                action(cp)

    @pl.when(b == 0)
    def _():
        prefetch(b, lambda cp: cp.start())

    @pl.when(b + 1 < pl.num_programs(0))
    def _():
        prefetch(b + 1, lambda cp: cp.start())

    q = [q_ref[:, h * HEAD_DIM:(h + 1) * HEAD_DIM].astype(BF16) for h in range(N_HEADS)]
    row = lax.broadcasted_iota(I32, (grp, page_size), 0) % TOK_PAD
    lane = lax.broadcasted_iota(I32, (grp, page_size), 1)
    prefetch(b, lambda cp: cp.wait())

    def process(k_h, v_h, before, run, acc):
        z = jnp.concatenate([_dot_nt(q[h], k_h[h]) for h in range(N_HEADS)], axis=0) * ATTN_SCALE
        sp = _softplus(z)
        log_keep = -sp if before is None else jnp.where(before, -sp, 0.0)
        cs = _split_dot(log_keep, r_ref[...], 3)
        e = jnp.exp(z - sp + cs[:, :page_size] + run)
        a = e if before is None else jnp.where(before, e, 0.0)
        run = run + cs[:, page_size:]
        acc = [acc[h] + _dot(a[h * TOK_PAD:(h + 1) * TOK_PAD, :].astype(BF16), v_h[h]) for h in range(N_HEADS)]
        live = (jnp.max(jnp.where(row < n_new, run, NEG_INF)) >= EXP_ZERO_BELOW).astype(I32)
        return run, acc, live

    def process_slot(slot, run, acc):
        k_h = [kbuf[slot, pl.ds(h, page_size, stride=N_HEADS), :].astype(BF16) for h in range(N_HEADS)]
        v_h = [vbuf[slot, pl.ds(h, page_size, stride=N_HEADS), :].astype(BF16) for h in range(N_HEADS)]
        return process(k_h, v_h, None, run, acc)

    run, acc, live = process(
        [_pad_rows(kn_ref[:, h * HEAD_DIM:(h + 1) * HEAD_DIM], page_size).astype(BF16) for h in range(N_HEADS)],
        [_pad_rows(vn_ref[:, h * HEAD_DIM:(h + 1) * HEAD_DIM], page_size).astype(BF16) for h in range(N_HEADS)],
        lane < row, jnp.zeros((grp, page_size), F32), [jnp.zeros((TOK_PAD, HEAD_DIM), F32)] * N_HEADS)
    for j in range(n_pre):
        run, acc, live = process_slot((b % 2) * n_pre + j, run, acc)
    run_ref[...] = run
    for h in range(N_HEADS):
        acc_ref[h * TOK_PAD:(h + 1) * TOK_PAD, :] = acc[h]

    def fetch_and_process(carry):
        page, _ = carry
        for cp in page_copies(b, page, spare):
            cp.start()
        for cp in page_copies(b, page, spare):
            cp.wait()
        run, acc, live = process_slot(spare, run_ref[...],
                                      [acc_ref[h * TOK_PAD:(h + 1) * TOK_PAD, :] for h in range(N_HEADS)])
        run_ref[...] = run
        for h in range(N_HEADS):
            acc_ref[h * TOK_PAD:(h + 1) * TOK_PAD, :] = acc[h]
        return page - 1, live

    lax.while_loop(lambda carry: (carry[0] >= 0) & (carry[1] > 0), fetch_and_process,
                   (jnp.int32(n_pages - n_pre - 1), live))
    for h in range(N_HEADS):
        o_ref[:, h * HEAD_DIM:(h + 1) * HEAD_DIM] = acc_ref[h * TOK_PAD:(h + 1) * TOK_PAD, :]


def sb_sample(proj8, k_pool, v_pool, page_table, n_new):
    n_seq = proj8.shape[0]
    n_pages = page_table.shape[1]
    page_size = k_pool.shape[1] // N_HEADS
    assert page_size == LANES
    n_pre = min(2, n_pages)
    r = jnp.asarray(_suffix_sum_matrix(page_size), BF16)
    tok = lambda col: pl.BlockSpec((None, TOK_PAD, D_ATT), lambda b, pt: (b, 0, col // D_ATT))
    page_buf = pltpu.VMEM((2 * n_pre + 1, page_size * N_HEADS, HEAD_DIM), F32)
    grid_spec = pltpu.PrefetchScalarGridSpec(
        num_scalar_prefetch=1,
        grid=(n_seq,),
        in_specs=[tok(C_QS), tok(C_KS), tok(C_VS), pl.BlockSpec((page_size, 2 * page_size), lambda b, pt: (0, 0)),
                  pl.BlockSpec(memory_space=pl.ANY), pl.BlockSpec(memory_space=pl.ANY)],
        out_specs=pl.BlockSpec((None, TOK_PAD, D_ATT), lambda b, pt: (b, 0, 0)),
        scratch_shapes=[page_buf, page_buf, pltpu.SemaphoreType.DMA((2, 2 * n_pre + 1)),
                        pltpu.VMEM((N_HEADS * TOK_PAD, page_size), F32), pltpu.VMEM((N_HEADS * TOK_PAD, HEAD_DIM), F32)],
    )
    return pl.pallas_call(
        functools.partial(_sb_sample_kernel, n_pages=n_pages, page_size=page_size, n_new=n_new, n_pre=n_pre),
        grid_spec=grid_spec,
        out_shape=jax.ShapeDtypeStruct((n_seq, TOK_PAD, D_ATT), F32),
        compiler_params=_cparams(1),
        name="sb_sample",
    )(page_table, proj8, proj8, proj8, r, k_pool, v_pool)


SCORE_SEQS = 2


def _dsa_scores_sample_kernel(pt_ref, qi_ref, w_ref, kin_ref, *rest, n_pages, page_size, n_new, n_sub):
    o_ref = rest[n_sub * n_pages]
    o_ref[...] = jnp.zeros_like(o_ref)
    for s_i in range(n_sub):
        ki_pages = rest[s_i * n_pages:(s_i + 1) * n_pages]
        qi = qi_ref[s_i].astype(BF16)
        w = w_ref[s_i] * IDX_W_SCALE
        for g in range(n_pages + 1):
            ki_t = (kin_ref[s_i] if g == n_pages else ki_pages[g][...]).astype(BF16)
            s = jnp.maximum(_dot(qi, ki_t), 0.0) * w
            s = jnp.sum(s.reshape(n_new, N_IDX_HEADS, page_size), axis=1)
            o_ref[s_i, 0:n_new, g * page_size:(g + 1) * page_size] = s


def dsa_scores_sample(qi3, w_rep, ki_new_t, ki_pool_t, page_table):
    n_seq = qi3.shape[0]
    n_new = qi3.shape[1] // N_IDX_HEADS
    n_pages = page_table.shape[1]
    page_size = ki_pool_t.shape[2]
    assert page_size == LANES
    n_sub = SCORE_SEQS if n_seq % SCORE_SEQS == 0 else 1
    seq_blk = lambda r, c: pl.BlockSpec((n_sub, r, c), lambda b, pt: (b, 0, 0))
    pages = [pl.BlockSpec((None, IDX_DIM, page_size),
                          functools.partial(lambda b, pt, s_i, p: (pt[b * n_sub + s_i, p], 0, 0), s_i=s_i, p=p))
             for s_i in range(n_sub) for p in range(n_pages)]
    grid_spec = pltpu.PrefetchScalarGridSpec(
        num_scalar_prefetch=1,
        grid=(n_seq // n_sub,),
        in_specs=[seq_blk(n_new * N_IDX_HEADS, IDX_DIM), seq_blk(n_new * N_IDX_HEADS, LANES),
                  seq_blk(IDX_DIM, page_size)] + pages,
        out_specs=seq_blk(TOK_PAD, (n_pages + 1) * page_size),
    )
    return pl.pallas_call(
        functools.partial(_dsa_scores_sample_kernel, n_pages=n_pages, page_size=page_size, n_new=n_new, n_sub=n_sub),
        grid_spec=grid_spec,
        out_shape=jax.ShapeDtypeStruct((n_seq, TOK_PAD, (n_pages + 1) * page_size), F32),
        compiler_params=_cparams(1),
        name="dsa_scores_sample",
    )(page_table, qi3, w_rep, ki_new_t, *([ki_pool_t] * (n_sub * n_pages)))


MASK_ROWS = 256


def _dsa_mask_sample_kernel(s_ref, ones_ref, m_ref, *, n_past, n_new, top_k):
    n_chunks = s_ref.shape[1] // LANES
    shape = (s_ref.shape[0], LANES)
    tok = lax.broadcasted_iota(I32, shape, 0) % TOK_PAD
    lane = lax.broadcasted_iota(I32, shape, 1)
    ones = ones_ref[...]
    keys, pos, exists = [], [], []
    for c in range(n_chunks):
        s = s_ref[:, c * LANES:(c + 1) * LANES]
        k_pos = c * LANES + lane
        s = jnp.where(k_pos <= n_past + tok, s, NEG_INF)
        keys.append(_float_key(s))
        pos.append(k_pos)
        exists.append(None if (c + 1) * LANES <= n_past + n_new else k_pos < n_past + n_new)

    def count(pred_fn):
        part = jnp.zeros(shape, F32)
        for c in range(n_chunks):
            p = pred_fn(keys[c], pos[c])
            if exists[c] is not None:
                p = p & exists[c]
            part = part + jnp.where(p, 1.0, 0.0)
        return _dot(part.astype(BF16), ones)

    base = jnp.where(count(lambda k, p: k >= 0) >= top_k, 0, INT_MIN).astype(I32)

    def bit_body(t, base):
        cand = base | lax.shift_left(jnp.int32(1), 30 - t)
        return jnp.where(count(lambda k, p: k >= cand) >= top_k, cand, base)

    v = lax.fori_loop(0, 31, bit_body, base)
    need = top_k - count(lambda k, p: k > v)

    n_bits = (n_chunks * LANES - 1).bit_length()

    def idx_bit(t, m):
        cand = m | lax.shift_left(jnp.int32(1), n_bits - 1 - t)
        below = count(lambda k, p: (k == v) & (p < cand))
        return jnp.where(below < need, cand, m)

    mstar = lax.fori_loop(0, n_bits, idx_bit, jnp.zeros(shape, I32))
    for c in range(n_chunks):
        sel = (keys[c] > v) | ((keys[c] == v) & (pos[c] <= mstar))
        sel = sel & (pos[c] <= n_past + tok) & (tok < n_new)
        if exists[c] is not None:
            sel = sel & exists[c]
        m_ref[:, c * LANES:(c + 1) * LANES] = jnp.where(sel, 1.0, 0.0)


def dsa_mask_sample(scores2, n_past, n_new):
    rows, width = scores2.shape
    top_k = min(TOP_K_MAX, (n_past + n_new) // 4)
    tr = min(MASK_ROWS, rows)
    return pl.pallas_call(
        functools.partial(_dsa_mask_sample_kernel, n_past=n_past, n_new=n_new, top_k=top_k),
        grid=(rows // tr,),
        in_specs=[pl.BlockSpec((tr, width), lambda i: (i, 0)), pl.BlockSpec((LANES, LANES), lambda i: (0, 0))],
        out_specs=pl.BlockSpec((tr, width), lambda i: (i, 0)),
        out_shape=jax.ShapeDtypeStruct((rows, width), F32),
        compiler_params=_cparams(1),
        name="dsa_mask_sample",
    )(scores2, jnp.ones((LANES, LANES), BF16))


def _sample_bucket_tables(n_new, page_size):
    j = (np.arange(N_HEADS * TOK_PAD) % TOK_PAD)[:, None]
    u = np.arange(page_size)[None, :]
    assert _rel_bucket_np(np.array([page_size + 1]))[0] == N_REL_BUCKETS - 1
    return np.stack([_rel_bucket_np(page_size + j - u), _rel_bucket_np(j - u)]).astype(np.int32)


def _dsa_attn_sample_kernel(pt_ref, rb_ref, q_ref, kn_ref, vn_ref, m_ref, bkt_ref, *rest, n_pages, page_size, n_new):
    k_pages = rest[:n_pages]
    v_pages = rest[n_pages:2 * n_pages]
    o_ref = rest[2 * n_pages]
    tab_ref = rest[2 * n_pages + 1]
    lg_ref = rest[2 * n_pages + 2]
    grp = N_HEADS * TOK_PAD

    @pl.when(pl.program_id(0) == 0)
    def _():
        for h in range(N_HEADS):
            rows = slice(h * TOK_PAD, (h + 1) * TOK_PAD)
            for t in range(2):
                bk = bkt_ref[t, rows, :]
                acc = jnp.zeros(bk.shape, F32)
                for b in range(N_REL_BUCKETS):
                    acc = jnp.where(bk == b, rb_ref[b, h], acc)
                tab_ref[t, rows, :] = acc
            tab_ref[2, rows, :] = jnp.full((TOK_PAD, page_size), rb_ref[N_REL_BUCKETS - 1, h], F32)

    q = [q_ref[:, h * HEAD_DIM:(h + 1) * HEAD_DIM].astype(BF16) for h in range(N_HEADS)]
    mx = jnp.full((grp, page_size), NEG_INF, F32)
    for g in range(n_pages + 1):
        if g == n_pages:
            k_h = [_pad_rows(kn_ref[:, h * HEAD_DIM:(h + 1) * HEAD_DIM], page_size).astype(BF16) for h in range(N_HEADS)]
            bias = tab_ref[1]
        else:
            k_h = [_head_rows(k_pages[g], h, page_size) for h in range(N_HEADS)]
            bias = tab_ref[0] if g == n_pages - 1 else tab_ref[2]
        s = jnp.concatenate([_dot_nt(q[h], k_h[h]) for h in range(N_HEADS)], axis=0) * ATTN_SCALE + bias
        sel = jnp.tile(m_ref[:, g * page_size:(g + 1) * page_size], (N_HEADS, 1))
        s = jnp.where(sel > 0.5, s, NEG_INF)
        lg_ref[g] = s
        mx = jnp.maximum(mx, s)
    m = jnp.max(mx, axis=1, keepdims=True)
    l = jnp.zeros((grp, page_size), F32)
    acc = [jnp.zeros((TOK_PAD, HEAD_DIM), F32) for _ in range(N_HEADS)]
    for g in range(n_pages + 1):
        p = jnp.exp(lg_ref[g] - m)
        l = l + p
        for h in range(N_HEADS):
            if g == n_pages:
                v_h = _pad_rows(vn_ref[:, h * HEAD_DIM:(h + 1) * HEAD_DIM], page_size).astype(BF16)
            else:
                v_h = _head_rows(v_pages[g], h, page_size)
            acc[h] = acc[h] + _dot(p[h * TOK_PAD:(h + 1) * TOK_PAD, :].astype(BF16), v_h)
    inv = 1.0 / jnp.sum(l, axis=1, keepdims=True)
    for h in range(N_HEADS):
        o_ref[:, h * HEAD_DIM:(h + 1) * HEAD_DIM] = acc[h] * inv[h * TOK_PAD:(h + 1) * TOK_PAD, :]


def dsa_attn_sample(proj8, mask3, rel_bias, k_pool, v_pool, page_table, n_new):
    n_seq = proj8.shape[0]
    n_pages = page_table.shape[1]
    page_size = k_pool.shape[1] // N_HEADS
    assert page_size == LANES
    grp = N_HEADS * TOK_PAD
    width = (n_pages + 1) * page_size
    tok = lambda col: pl.BlockSpec((None, TOK_PAD, D_ATT), lambda b, pt: (b, 0, col // D_ATT))
    grid_spec = pltpu.PrefetchScalarGridSpec(
        num_scalar_prefetch=1,
        grid=(n_seq,),
        in_specs=[pl.BlockSpec(memory_space=pltpu.SMEM), tok(C_QD), tok(C_KD), tok(C_VD),
                  pl.BlockSpec((None, TOK_PAD, width), lambda b, pt: (b, 0, 0)),
                  pl.BlockSpec((2, grp, page_size), lambda b, pt: (0, 0, 0))]
                 + _page_specs(n_pages, page_size * N_HEADS, HEAD_DIM)
                 + _page_specs(n_pages, page_size * N_HEADS, HEAD_DIM),
        out_specs=pl.BlockSpec((None, TOK_PAD, D_ATT), lambda b, pt: (b, 0, 0)),
        scratch_shapes=[pltpu.VMEM((3, grp, page_size), F32), pltpu.VMEM((n_pages + 1, grp, page_size), F32)],
    )
    return pl.pallas_call(
        functools.partial(_dsa_attn_sample_kernel, n_pages=n_pages, page_size=page_size, n_new=n_new),
        grid_spec=grid_spec,
        out_shape=jax.ShapeDtypeStruct((n_seq, TOK_PAD, D_ATT), F32),
        compiler_params=_cparams(1, vmem_mb=56),
        name="dsa_attn_sample",
    )(page_table, rel_bias, proj8, proj8, proj8, mask3, jnp.asarray(_sample_bucket_tables(n_new, page_size)),
      *([k_pool] * n_pages), *([v_pool] * n_pages))


def _tail(x, proj, o_a, o_b, weights, gains):
    w1, w2, wo, wu, wd = weights
    g_attn_post, g_mlp_pre, g_mlp_post = gains
    y = merge_branches(o_a, o_b, w1, w2, proj, min(1024, x.shape[0]))
    x1, h2 = proj_norm(y, wo, x, g_attn_post, g_mlp_pre)
    return mlp(h2, wu, wd, x1, g_mlp_post)


def kernel(x_prompt, x_sample, cache_k_dsa, cache_v_dsa, cache_k_idx, cache_k_sb, cache_v_sb, page_table, rel_bias,
           w_in, w_out_dsa, w_out_sb, w_o, w_up, w_down, g_attn_pre, g_attn_post, g_mlp_pre, g_mlp_post):
    batch, seq, d_model = x_prompt.shape
    n_seq, n_new, _ = x_sample.shape
    depth = w_in.shape[0]
    n_pool, page_size = cache_k_idx.shape[1], cache_k_idx.shape[2]
    n_past = page_table.shape[1] * page_size

    xp = x_prompt.reshape(batch * seq, d_model)
    xs = jnp.pad(x_sample, ((0, 0), (0, TOK_PAD - n_new), (0, 0))).reshape(n_seq * TOK_PAD, d_model)
    rows_p, rows_s = [], []
    for l in range(depth):
        w_in_t = jnp.swapaxes(w_in[l], 0, 1)
        weights = tuple(w[l].astype(BF16) for w in (w_out_dsa, w_out_sb, w_o, w_up, w_down))
        gains = (g_attn_post[l], g_mlp_pre[l], g_mlp_post[l])

        hp = rms_cast(xp, g_attn_pre[l])
        proj, c_small = in_proj(hp, w_in_t, tm=min(1024, hp.shape[0]))
        mask = dsa_mask_prompt(proj, batch, seq, c_small)
        o_a = dsa_attn_prompt(proj, mask, rel_bias, batch, seq)
        o_b = sb_prompt(proj, batch, seq)
        xp_new = _tail(xp, proj, o_a, o_b, weights, gains)
        heads = lambda c: proj[:, c:c + D_ATT].reshape(batch, seq, N_HEADS, HEAD_DIM)
        rows_p.append((heads(C_KD), heads(C_VD), proj[:, c_small:c_small + IDX_DIM].reshape(batch, seq, IDX_DIM),
                       heads(C_KS), heads(C_VS)))
        xp = xp_new

        hs = rms_cast(xs, g_attn_pre[l])
        proj_s, _ = in_proj(hs, w_in_t, tm=min(1024, hs.shape[0]))
        proj8 = proj_s.reshape(n_seq, TOK_PAD, proj_s.shape[1])
        real = lambda c, w: proj8[:, :n_new, c:c + w]
        qi3 = real(C_QI, N_IDX_HEADS * IDX_DIM).reshape(n_seq, n_new * N_IDX_HEADS, IDX_DIM)
        w_rep = jnp.broadcast_to(real(c_small + IDX_DIM, N_IDX_HEADS).reshape(n_seq, n_new * N_IDX_HEADS, 1),
                                 (n_seq, n_new * N_IDX_HEADS, LANES))
        ki_new = real(c_small, IDX_DIM)
        ki_new_t = jnp.pad(jnp.swapaxes(ki_new, 1, 2), ((0, 0), (0, 0), (0, page_size - n_new)))
        pool2 = lambda c: c[l].reshape(n_pool, page_size * N_HEADS, HEAD_DIM)
        scores = dsa_scores_sample(qi3, w_rep, ki_new_t, jnp.swapaxes(cache_k_idx[l], 1, 2), page_table)
        mask_s = dsa_mask_sample(scores.reshape(n_seq * TOK_PAD, scores.shape[2]), n_past, n_new)
        o_a = dsa_attn_sample(proj8, mask_s.reshape(scores.shape), rel_bias, pool2(cache_k_dsa), pool2(cache_v_dsa),
                              page_table, n_new)
        o_b = sb_sample(proj8, pool2(cache_k_sb), pool2(cache_v_sb), page_table, n_new)
        flat = lambda o: o.reshape(n_seq * TOK_PAD, D_ATT)
        xs_new = _tail(xs, proj_s, flat(o_a), flat(o_b), weights, gains)
        heads_s = lambda c: real(c, D_ATT).reshape(n_seq, n_new, N_HEADS, HEAD_DIM)
        rows_s.append((heads_s(C_KD), heads_s(C_VD), ki_new, heads_s(C_KS), heads_s(C_VS)))
        xs = xs_new

    outs_p = [jnp.stack(r, axis=0) for r in zip(*rows_p)]
    outs_s = [jnp.stack(r, axis=0) for r in zip(*rows_s)]
    return (xp.reshape(batch, seq, d_model), xs.reshape(n_seq, TOK_PAD, d_model)[:, :n_new], *outs_p, *outs_s)
```

```python
import functools
import math

import numpy as np
import jax
import jax.numpy as jnp
from jax import lax
from jax.experimental import pallas as pl
from jax.experimental.pallas import tpu as pltpu

F32 = jnp.float32
BF16 = jnp.bfloat16
I32 = jnp.int32

HEAD_DIM = 128
N_HEADS = 8
N_IDX_HEADS = 16
IDX_DIM = 64
IDX_W_SCALE = (N_IDX_HEADS * IDX_DIM) ** -0.5
TOP_K_MAX = 256
N_REL_BUCKETS = 32
REL_MAX_DISTANCE = 128
RMS_EPS = 1e-6
NEG_INF = -1e30
ATTN_SCALE = HEAD_DIM ** -0.5
LOG2E = math.log2(math.e)

LANES = 128
SUBLANES = 8
D_ATT = N_HEADS * HEAD_DIM

C_QD, C_KD, C_VD, C_QI, C_QS, C_KS, C_VS = (k * D_ATT for k in range(7))
C_GA = 7 * D_ATT
PROJ_TN = 1024


def _key_of(x):
    b = int(np.float32(x).view(np.int32))
    return b if b >= 0 else b ^ 0x7FFFFFFF


KEY_NEG_INF = _key_of(NEG_INF)
INT_MIN = -(2 ** 31)


def _cparams(n_axes, vmem_mb=48):
    return pltpu.CompilerParams(dimension_semantics=("arbitrary",) * n_axes,
                                vmem_limit_bytes=vmem_mb * 1024 * 1024)


def _dot(a, b):
    return jnp.dot(a, b, preferred_element_type=F32)


def _dot_nt(a, b):
    return lax.dot_general(a, b, (((1,), (1,)), ((), ())), preferred_element_type=F32)


def _split_dot(x, m01, passes):
    out = None
    rem = x
    for p in range(passes):
        piece = rem.astype(BF16)
        d = _dot(piece, m01)
        out = d if out is None else out + d
        if p + 1 < passes:
            rem = rem - piece.astype(F32)
    return out


def _softplus(z):
    return jnp.maximum(z, 0.0) + jnp.log1p(jnp.exp(-jnp.abs(z)))


def _float_key(s):
    s = jnp.where(s == 0.0, 0.0, s)
    bits = lax.bitcast_convert_type(s, I32)
    return jnp.where(bits >= 0, bits, bits ^ 0x7FFFFFFF)


def _rel_bucket_np(dist):
    max_exact = N_REL_BUCKETS // 2
    d = np.maximum(dist, 0)
    df = np.maximum(d, 1).astype(np.float32)
    log_part = np.log(df / np.float32(max_exact)) / np.float32(math.log(REL_MAX_DISTANCE / max_exact))
    large = max_exact + (log_part * np.float32(N_REL_BUCKETS - max_exact)).astype(np.int32)
    return np.where(d < max_exact, d, np.minimum(large, N_REL_BUCKETS - 1)).astype(np.int32)


def _rms_cast_kernel(x_ref, g_ref, o_ref):
    x = x_ref[...]
    ms = jnp.mean(x * x, axis=-1, keepdims=True)
    o_ref[...] = (x * lax.rsqrt(ms + RMS_EPS) * g_ref[...]).astype(o_ref.dtype)


def rms_cast(x, g, tm=256):
    m, d = x.shape
    tm = min(tm, m)
    return pl.pallas_call(
        _rms_cast_kernel,
        grid=(m // tm,),
        in_specs=[pl.BlockSpec((tm, d), lambda i: (i, 0)), pl.BlockSpec((1, d), lambda i: (0, 0))],
        out_specs=pl.BlockSpec((tm, d), lambda i: (i, 0)),
        out_shape=jax.ShapeDtypeStruct((m, d), BF16),
        compiler_params=_cparams(1),
        name="rms_cast",
    )(x, g.reshape(1, d))


def _in_proj_kernel(h_ref, wt_ref, o_ref, wb_ref):
    @pl.when(pl.program_id(1) == 0)
    def _():
        wb_ref[...] = wt_ref[...].astype(BF16)

    o_ref[...] = _dot_nt(h_ref[...], wb_ref[...])


def in_proj(h, w_t, tm):
    m, k = h.shape
    in_cols = w_t.shape[0]
    c_ki = 4 * D_ATT
    c_after = c_ki + IDX_DIM + N_IDX_HEADS
    assert c_ki % PROJ_TN == 0 and (in_cols - c_after) % PROJ_TN == 0 and c_ki + PROJ_TN <= in_cols
    n_a = c_ki // PROJ_TN
    n_b = (in_cols - c_after) // PROJ_TN

    assert c_after % SUBLANES == 0 and PROJ_TN % SUBLANES == 0
    tile8, after8, ki8 = PROJ_TN // SUBLANES, c_after // SUBLANES, c_ki // SUBLANES

    def w_row(j):
        return jnp.where(j < n_a, j * tile8, jnp.where(j < n_a + n_b, after8 + (j - n_a) * tile8, ki8)) * SUBLANES

    n_tiles = n_a + n_b + 1
    proj = pl.pallas_call(
        _in_proj_kernel,
        grid=(n_tiles, m // tm),
        in_specs=[pl.BlockSpec((tm, k), lambda j, i: (i, 0)),
                  pl.BlockSpec((pl.Element(PROJ_TN), pl.Element(k)), lambda j, i: (w_row(j), 0))],
        out_specs=pl.BlockSpec((tm, PROJ_TN), lambda j, i: (i, j)),
        out_shape=jax.ShapeDtypeStruct((m, n_tiles * PROJ_TN), F32),
        scratch_shapes=[pltpu.VMEM((PROJ_TN, k), BF16)],
        compiler_params=_cparams(2),
        name="in_proj",
    )(h, w_t)
    return proj, (n_a + n_b) * PROJ_TN


def _merge_kernel(oa_ref, ob_ref, w1_ref, w2_ref, ga_ref, gb_ref, y_ref):
    ya = _dot(oa_ref[...].astype(BF16), w1_ref[...])
    yb = _dot(ob_ref[...].astype(BF16), w2_ref[...])
    sa = 1.0 / (1.0 + jnp.exp(-ga_ref[...]))
    sb = 1.0 / (1.0 + jnp.exp(-gb_ref[...]))
    y_ref[...] = (sa * ya + sb * yb).astype(y_ref.dtype)


def merge_branches(o_a, o_b, w1, w2, proj, tm, tn=1024):
    m = o_a.shape[0]
    d = w1.shape[1]
    tn = min(tn, d)
    ga0 = C_GA // tn
    gb0 = (C_GA + d) // tn
    return pl.pallas_call(
        _merge_kernel,
        grid=(d // tn, m // tm),
        in_specs=[pl.BlockSpec((tm, D_ATT), lambda j, i: (i, 0)),
                  pl.BlockSpec((tm, D_ATT), lambda j, i: (i, 0)),
                  pl.BlockSpec((D_ATT, tn), lambda j, i: (0, j)),
                  pl.BlockSpec((D_ATT, tn), lambda j, i: (0, j)),
                  pl.BlockSpec((tm, tn), lambda j, i: (i, ga0 + j)),
                  pl.BlockSpec((tm, tn), lambda j, i: (i, gb0 + j))],
        out_specs=pl.BlockSpec((tm, tn), lambda j, i: (i, j)),
        out_shape=jax.ShapeDtypeStruct((m, d), BF16),
        compiler_params=_cparams(2),
        name="merge",
    )(o_a, o_b, w1, w2, proj, proj)


def _proj_norm_kernel(y_ref, wo_ref, x_ref, gpost_ref, gpre_ref, x1_ref, h2_ref, *, parts):
    rows_per = y_ref.shape[0] // parts
    for p in range(parts):
        rows = slice(p * rows_per, (p + 1) * rows_per)
        a = _dot(y_ref[rows, :], wo_ref[...])
        r = a * lax.rsqrt(jnp.mean(a * a, axis=-1, keepdims=True) + RMS_EPS) * gpost_ref[...]
        x1 = x_ref[rows, :] + r
        x1_ref[rows, :] = x1
        h2 = x1 * lax.rsqrt(jnp.mean(x1 * x1, axis=-1, keepdims=True) + RMS_EPS) * gpre_ref[...]
        h2_ref[rows, :] = h2.astype(h2_ref.dtype)


def proj_norm(y, wo, x, g_post, g_pre, tm=512):
    m, d = x.shape
    tm = min(tm, m)
    parts = 2 if tm % (2 * 2 * SUBLANES) == 0 else 1
    row = lambda i: (i, 0)
    fix = lambda i: (0, 0)
    return pl.pallas_call(
        functools.partial(_proj_norm_kernel, parts=parts),
        grid=(m // tm,),
        in_specs=[pl.BlockSpec((tm, d), row), pl.BlockSpec((d, d), fix), pl.BlockSpec((tm, d), row),
                  pl.BlockSpec((1, d), fix), pl.BlockSpec((1, d), fix)],
        out_specs=[pl.BlockSpec((tm, d), row), pl.BlockSpec((tm, d), row)],
        out_shape=[jax.ShapeDtypeStruct((m, d), F32), jax.ShapeDtypeStruct((m, d), BF16)],
        compiler_params=_cparams(1),
        name="proj_norm",
    )(y, wo, x, g_post.reshape(1, d), g_pre.reshape(1, d))


def _mlp_kernel(h_ref, wu_ref, wd_ref, x_ref, g_ref, o_ref, acc_ref):
    k = pl.program_id(1)

    @pl.when(k == 0)
    def _():
        acc_ref[...] = jnp.zeros_like(acc_ref)

    u = jnp.maximum(_dot(h_ref[...], wu_ref[...]), 0.0)
    acc_ref[...] += _dot((u * u).astype(BF16), wd_ref[...])

    @pl.when(k == pl.num_programs(1) - 1)
    def _():
        a = acc_ref[...]
        o_ref[...] = x_ref[...] + a * lax.rsqrt(jnp.mean(a * a, axis=-1, keepdims=True) + RMS_EPS) * g_ref[...]


MLP_TM = 512
MLP_ONE_TILE_ROWS = 1024


def mlp(h2, w_up, w_down, x1, g_post):
    m, d = x1.shape
    ff = w_up.shape[1]
    if m <= MLP_ONE_TILE_ROWS:
        tm, tk, vmem_mb = m, min(512, ff), 58
    else:
        tm, tk, vmem_mb = MLP_TM, min(1024, ff), 48
    return pl.pallas_call(
        _mlp_kernel,
        grid=(m // tm, ff // tk),
        in_specs=[pl.BlockSpec((tm, d), lambda i, k: (i, 0)),
                  pl.BlockSpec((d, tk), lambda i, k: (0, k)),
                  pl.BlockSpec((tk, d), lambda i, k: (k, 0)),
                  pl.BlockSpec((tm, d), lambda i, k: (i, 0)),
                  pl.BlockSpec((1, d), lambda i, k: (0, 0))],
        out_specs=pl.BlockSpec((tm, d), lambda i, k: (i, 0)),
        out_shape=jax.ShapeDtypeStruct((m, d), F32),
        scratch_shapes=[pltpu.VMEM((tm, d), F32)],
        compiler_params=_cparams(2, vmem_mb=vmem_mb),
        name="mlp",
    )(h2, w_up, w_down, x1, g_post.reshape(1, d))


QB = LANES
KC = 512
SCORE_KC = 512
AQ = 512
ATTN_UNROLL = 4


def _dsa_mask_prompt_kernel(qi_ref, sm_all_ref, sm_blk_ref, mask_ref, kia_ref, kib_ref, keys_ref, mstar_ref, qq_ref,
                            *, seq, top_k):
    i = pl.program_id(1)
    n_chunks = (i * QB) // KC + 1

    @pl.when(i == 0)
    def _():
        sm = sm_all_ref[...]
        lane = lax.broadcasted_iota(I32, sm.shape, 1)
        kia_ref[...] = jnp.where(lane < IDX_DIM, sm, 0.0).astype(BF16)
        kib_ref[...] = jnp.where(lane >= IDX_DIM, pltpu.roll(sm, IDX_DIM, axis=1), 0.0).astype(BF16)

    w_t = sm_blk_ref[...].T * IDX_W_SCALE
    qi = qi_ref[...].astype(BF16)
    for g in range(N_IDX_HEADS // 4):
        qq_ref[g] = jnp.concatenate([qi[:, (2 * g) * LANES:(2 * g + 1) * LANES],
                                     qi[:, (2 * g + 1) * LANES:(2 * g + 2) * LANES]], axis=0)
    q_pos_s = i * QB + lax.broadcasted_iota(I32, (SCORE_KC, QB), 1)
    row_s = lax.broadcasted_iota(I32, (SCORE_KC, QB), 0)

    def score_chunk(c, carry):
        r0 = pl.multiple_of(c * SCORE_KC, SCORE_KC)
        ka = kia_ref[pl.ds(r0, SCORE_KC), :]
        kb = kib_ref[pl.ds(r0, SCORE_KC), :]
        acc = jnp.zeros((SCORE_KC, QB), F32)
        for g in range(N_IDX_HEADS // 4):
            qq = qq_ref[g]
            for odd, k_half in ((0, ka), (1, kb)):
                s = jnp.maximum(_dot_nt(k_half, qq), 0.0)
                for j in range(2):
                    head = 4 * g + 2 * j + odd
                    acc = acc + s[:, j * LANES:(j + 1) * LANES] * w_t[IDX_DIM + head:IDX_DIM + head + 1, :]
        sc = jnp.where(c * SCORE_KC + row_s <= q_pos_s, acc, NEG_INF)
        keys_ref[pl.ds(r0, SCORE_KC), :] = _float_key(sc)
        return carry

    lax.fori_loop(0, n_chunks * (KC // SCORE_KC), score_chunk, 0)

    n_tail = seq - n_chunks * KC
    q_pos = i * QB + lax.broadcasted_iota(I32, (KC, QB), 1)
    row = lax.broadcasted_iota(I32, (KC, QB), 0)

    def count(pred_fn):
        def body(c, acc):
            k = keys_ref[pl.ds(pl.multiple_of(c * KC, KC), KC), :]
            ones = jnp.where(pred_fn(k, c), 1, 0)
            return acc + jnp.sum(ones.reshape(KC // SUBLANES, SUBLANES, QB), axis=0)

        acc = lax.fori_loop(0, n_chunks, body, jnp.zeros((SUBLANES, QB), I32))
        return jnp.sum(acc, axis=0, keepdims=True)

    def count_ge(cand):
        return count(lambda k, c: k >= cand) + jnp.where(KEY_NEG_INF >= cand, n_tail, 0)

    base = jnp.where(count_ge(jnp.zeros((1, QB), I32)) >= top_k, 0, INT_MIN).astype(I32)

    def bit_body(t, base):
        cand = base | lax.shift_left(jnp.int32(1), 30 - t)
        return jnp.where(count_ge(cand) >= top_k, cand, base)

    v = lax.fori_loop(0, 31, bit_body, base)

    n_gt = count(lambda k, c: k > v) + jnp.where(KEY_NEG_INF > v, n_tail, 0)
    need = top_k - n_gt
    n_eq = count(lambda k, c: k == v)
    mstar_ref[...] = jnp.full((SUBLANES, QB), seq, I32)

    @pl.when(jnp.max(jnp.where(n_eq > need, 1, 0)) > 0)
    def _():
        def idx_bit(t, m):
            cand = m | lax.shift_left(jnp.int32(1), (seq - 1).bit_length() - 1 - t)
            below = count(lambda k, c: (k == v) & (c * KC + row < cand))
            return jnp.where(below < need, cand, m)

        m = lax.fori_loop(0, (seq - 1).bit_length(), idx_bit, jnp.zeros((1, QB), I32))
        mstar_ref[...] = jnp.broadcast_to(m, (SUBLANES, QB))

    mstar = mstar_ref[0:1, :]

    def emit(c, carry):
        k = keys_ref[pl.ds(pl.multiple_of(c * KC, KC), KC), :]
        k_pos = c * KC + row
        sel = (k > v) | ((k == v) & (k_pos <= mstar))
        m_t = jnp.where(sel & (k_pos <= q_pos), 0.0, NEG_INF)
        mask_ref[c] = m_t.T.astype(mask_ref.dtype)
        return carry

    lax.fori_loop(0, n_chunks, emit, 0)

    def clear(c, carry):
        mask_ref[c] = jnp.full((QB, KC), NEG_INF, mask_ref.dtype)
        return carry

    lax.fori_loop(n_chunks, seq // KC, clear, 0)


def dsa_mask_prompt(proj, batch, seq, c_small):
    assert seq % KC == 0 and KC % AQ == 0 and AQ % QB == 0
    nq = seq // QB
    per = AQ // QB
    top_k = min(TOP_K_MAX, seq // 4)
    return pl.pallas_call(
        functools.partial(_dsa_mask_prompt_kernel, seq=seq, top_k=top_k),
        grid=(batch, nq),
        in_specs=[pl.BlockSpec((QB, N_IDX_HEADS * IDX_DIM), lambda b, i: (b * nq + i, C_QI // (N_IDX_HEADS * IDX_DIM))),
                  pl.BlockSpec((seq, LANES), lambda b, i: (b, c_small // LANES)),
                  pl.BlockSpec((QB, LANES), lambda b, i: (b * nq + i, c_small // LANES))],
        out_specs=pl.BlockSpec((None, seq // KC, None, QB, KC), lambda b, i: ((b * nq + i) // per, 0, i % per, 0, 0)),
        out_shape=jax.ShapeDtypeStruct((batch * seq // AQ, seq // KC, per, QB, KC), BF16),
        scratch_shapes=[pltpu.VMEM((seq, LANES), BF16), pltpu.VMEM((seq, LANES), BF16),
                        pltpu.VMEM((seq, QB), I32), pltpu.VMEM((SUBLANES, QB), I32),
                        pltpu.VMEM((N_IDX_HEADS // 4, 2 * QB, LANES), BF16)],
        compiler_params=_cparams(2),
        name="dsa_mask_prompt",
    )(proj, proj, proj)


def _bucket_thresholds():
    d = np.arange(4 * REL_MAX_DISTANCE)
    b = _rel_bucket_np(d)
    assert np.all(np.diff(b) >= 0) and b[-1] == N_REL_BUCKETS - 1
    return [int(np.argmax(b > j)) for j in range(N_REL_BUCKETS - 1)]


def _bias_of_distance(rb_ref, h, d, thresholds):
    acc = jnp.full(d.shape, rb_ref[N_REL_BUCKETS - 1, h], F32)
    for j in range(N_REL_BUCKETS - 2, -1, -1):
        acc = jnp.where(d < thresholds[j], rb_ref[j, h], acc)
    return acc


def _lane_fold(x, op):
    out = x[:, :LANES]
    for j in range(1, x.shape[1] // LANES):
        out = op(out, x[:, j * LANES:(j + 1) * LANES])
    return out


def _dsa_attn_prompt_kernel(rb_ref, q_ref, k_ref, v_ref, mask_ref, o_ref, tab_ref, lg_ref, kb_ref, vb_ref, acc_ref,
                            l_ref, mx_ref, *, thresholds, unroll):
    h = pl.program_id(1)
    i = pl.program_id(2)
    per = KC // AQ
    part = i % per
    c_diag = i // per

    @pl.when(i == 0)
    def _():
        kb_ref[...] = k_ref[...].astype(BF16)
        vb_ref[...] = v_ref[...].astype(BF16)
        row = lax.broadcasted_iota(I32, (AQ, KC), 0)
        col = lax.broadcasted_iota(I32, (AQ, KC), 1)
        for p in range(per):
            for back in range(2):
                tab_ref[3 * p + back] = LOG2E * _bias_of_distance(rb_ref, h, p * AQ + back * KC + row - col,
                                                                  thresholds)
            tab_ref[3 * p + 2] = jnp.full((AQ, KC), LOG2E * rb_ref[N_REL_BUCKETS - 1, h], F32)

    q = q_ref[...].astype(BF16)

    n_chunks = c_diag + 1

    def over_chunks(chunk_fn):
        def trip(t, carry):
            for u in range(unroll):
                chunk_fn(t * unroll + u)
            return carry

        lax.fori_loop(0, n_chunks // unroll, trip, 0)
        base = (n_chunks // unroll) * unroll
        part_size = unroll // 2
        while part_size >= 1:
            has_part = ((n_chunks - base) & part_size) != 0

            @pl.when(has_part)
            def _(base=base, part_size=part_size):
                for u in range(part_size):
                    chunk_fn(base + u)

            base = base + jnp.where(has_part, part_size, 0)
            part_size //= 2

    def logits_chunk(c):
        kc = kb_ref[pl.ds(pl.multiple_of(c * KC, KC), KC), :]
        s = _dot_nt(q, kc) * (ATTN_SCALE * LOG2E) + tab_ref[3 * part + jnp.minimum(c_diag - c, 2)]
        s = s + mask_ref[c].reshape(AQ, KC).astype(F32)
        lg_ref[c] = s
        mx_ref[...] = jnp.maximum(mx_ref[...], _lane_fold(s, jnp.maximum))

    mx_ref[...] = jnp.full((AQ, LANES), NEG_INF, F32)
    over_chunks(logits_chunk)
    m = jnp.max(mx_ref[...], axis=1, keepdims=True)
    acc_ref[...] = jnp.zeros_like(acc_ref)
    l_ref[...] = jnp.zeros_like(l_ref)

    def pv_chunk(c):
        p = jnp.exp2(lg_ref[c] - m)
        l_ref[...] += _lane_fold(p, jnp.add)
        acc_ref[...] += _dot(p.astype(BF16), vb_ref[pl.ds(pl.multiple_of(c * KC, KC), KC), :])

    over_chunks(pv_chunk)
    o_ref[...] = (acc_ref[...] / jnp.sum(l_ref[...], axis=1, keepdims=True)).astype(o_ref.dtype)


def dsa_attn_prompt(proj, mask, rel_bias, batch, seq):
    nq = seq // AQ
    nk = seq // KC
    hd = HEAD_DIM
    thresholds = _bucket_thresholds()
    assert thresholds[-1] <= KC + 1
    return pl.pallas_call(
        functools.partial(_dsa_attn_prompt_kernel, thresholds=thresholds, unroll=ATTN_UNROLL),
        grid=(batch, N_HEADS, nq),
        in_specs=[pl.BlockSpec(memory_space=pltpu.SMEM),
                  pl.BlockSpec((AQ, hd), lambda b, h, i: (b * nq + i, C_QD // hd + h)),
                  pl.BlockSpec((seq, hd), lambda b, h, i: (b, C_KD // hd + h)),
                  pl.BlockSpec((seq, hd), lambda b, h, i: (b, C_VD // hd + h)),
                  pl.BlockSpec((None, nk, AQ // QB, QB, KC), lambda b, h, i: (b * nq + i, 0, 0, 0, 0))],
        out_specs=pl.BlockSpec((AQ, hd), lambda b, h, i: (b * nq + i, h)),
        out_shape=jax.ShapeDtypeStruct((batch * seq, D_ATT), BF16),
        scratch_shapes=[pltpu.VMEM((3 * (KC // AQ), AQ, KC), F32), pltpu.VMEM((nk, AQ, KC), F32),
                        pltpu.VMEM((seq, hd), BF16), pltpu.VMEM((seq, hd), BF16),
                        pltpu.VMEM((AQ, hd), F32), pltpu.VMEM((AQ, LANES), F32), pltpu.VMEM((AQ, LANES), F32)],
        compiler_params=_cparams(3),
        name="dsa_attn_prompt",
    )(rel_bias, proj, proj, proj, mask)


SB_T = 256
SB_HEADS = 4
EXP_ZERO_BELOW = -104.0


def _suffix_sum_matrix(n):
    s = np.arange(n)
    tri = (s[:, None] > s[None, :]).astype(np.float32)
    return np.concatenate([tri, np.ones((n, n), np.float32)], axis=1)


def _sb_prompt_kernel(q_ref, k_ref, v_ref, r_ref, o_ref, kb_ref, vb_ref, acc_ref, run_ref):
    i = pl.program_id(2)

    @pl.when(i == 0)
    def _():
        kb_ref[...] = k_ref[...].astype(BF16)
        vb_ref[...] = v_ref[...].astype(BF16)

    q = q_ref[...].astype(BF16)
    acc_ref[...] = jnp.zeros_like(acc_ref)
    run_ref[...] = jnp.zeros_like(run_ref)
    row = lax.broadcasted_iota(I32, (SB_T, SB_T), 0)
    col = lax.broadcasted_iota(I32, (SB_T, SB_T), 1)

    def chunk(carry):
        step, _ = carry
        c = i - step
        r0 = pl.multiple_of(c * SB_T, SB_T)
        before = col < row + step * SB_T
        live = None
        for hh in range(SB_HEADS):
            cols = slice(hh * HEAD_DIM, (hh + 1) * HEAD_DIM)
            z = _dot_nt(q[:, cols], kb_ref[pl.ds(r0, SB_T), cols]) * ATTN_SCALE
            sp = _softplus(z)
            log_keep = jnp.where(before, -sp, 0.0)
            cs = _split_dot(log_keep, r_ref[...], 2)
            later = cs[:, :SB_T] + run_ref[hh]
            a = jnp.where(before, jnp.exp(z - sp + later), 0.0)
            acc_ref[:, cols] += _dot(a.astype(BF16), vb_ref[pl.ds(r0, SB_T), cols])
            run = run_ref[hh] + cs[:, SB_T:]
            run_ref[hh] = run
            top = jnp.max(run[:, :LANES])
            live = top if live is None else jnp.maximum(live, top)
        return step + 1, (live >= EXP_ZERO_BELOW).astype(I32)

    lax.while_loop(lambda carry: (carry[0] <= i) & (carry[1] > 0), chunk, (jnp.int32(0), jnp.int32(1)))
    o_ref[...] = acc_ref[...].astype(o_ref.dtype)


def sb_prompt(proj, batch, seq):
    nq = seq // SB_T
    w = SB_HEADS * HEAD_DIM
    r = jnp.asarray(_suffix_sum_matrix(SB_T), BF16)
    return pl.pallas_call(
        _sb_prompt_kernel,
        grid=(batch, N_HEADS // SB_HEADS, nq),
        in_specs=[pl.BlockSpec((SB_T, w), lambda b, h, i: (b * nq + i, C_QS // w + h)),
                  pl.BlockSpec((seq, w), lambda b, h, i: (b, C_KS // w + h)),
                  pl.BlockSpec((seq, w), lambda b, h, i: (b, C_VS // w + h)),
                  pl.BlockSpec((SB_T, 2 * SB_T), lambda b, h, i: (0, 0))],
        out_specs=pl.BlockSpec((SB_T, w), lambda b, h, i: (b * nq + i, h)),
        out_shape=jax.ShapeDtypeStruct((batch * seq, D_ATT), BF16),
        scratch_shapes=[pltpu.VMEM((seq, w), BF16), pltpu.VMEM((seq, w), BF16),
                        pltpu.VMEM((SB_T, w), F32), pltpu.VMEM((SB_HEADS, SB_T, SB_T), F32)],
        compiler_params=_cparams(3),
        name="sb_prompt",
    )(proj, proj, proj, r)


TOK_PAD = SUBLANES


def _pad_rows(x, rows):
    return jnp.concatenate([x, jnp.zeros((rows - x.shape[0], x.shape[1]), x.dtype)], axis=0)


def _head_rows(page_ref, h, page_size):
    return page_ref[pl.ds(h, page_size, stride=N_HEADS), :].astype(BF16)


def _page_specs(n_pages, rows, cols):
    return [pl.BlockSpec((None, rows, cols), functools.partial(lambda b, pt, p: (pt[b, p], 0, 0), p=p))
            for p in range(n_pages)]


def _sb_sample_kernel(pt_ref, q_ref, kn_ref, vn_ref, r_ref, k_hbm, v_hbm, o_ref, kbuf, vbuf, sems, run_ref, acc_ref,
                      *, n_pages, page_size, n_new, n_pre):
    b = pl.program_id(0)
    grp = N_HEADS * TOK_PAD
    spare = 2 * n_pre

    def page_copies(seq, page, slot):
        phys = pt_ref[seq, page]
        return (pltpu.make_async_copy(k_hbm.at[phys], kbuf.at[slot], sems.at[0, slot]),
                pltpu.make_async_copy(v_hbm.at[phys], vbuf.at[slot], sems.at[1, slot]))

    def prefetch(seq, action):
        for j in range(n_pre):
            for cp in page_copies(seq, n_pages - 1 - j, (seq % 2) * n_pre + j):
                action(cp)

    @pl.when(b == 0)
    def _():
        prefetch(b, lambda cp: cp.start())

    @pl.when(b + 1 < pl.num_programs(0))
    def _():
        prefetch(b + 1, lambda cp: cp.start())

    q = [q_ref[:, h * HEAD_DIM:(h + 1) * HEAD_DIM].astype(BF16) for h in range(N_HEADS)]
    row = lax.broadcasted_iota(I32, (grp, page_size), 0) % TOK_PAD
    lane = lax.broadcasted_iota(I32, (grp, page_size), 1)
    prefetch(b, lambda cp: cp.wait())

    def process(k_h, v_h, before, run, acc):
        z = jnp.concatenate([_dot_nt(q[h], k_h[h]) for h in range(N_HEADS)], axis=0) * ATTN_SCALE
        sp = _softplus(z)
        log_keep = -sp if before is None else jnp.where(before, -sp, 0.0)
        cs = _split_dot(log_keep, r_ref[...], 3)
        e = jnp.exp(z - sp + cs[:, :page_size] + run)
        a = e if before is None else jnp.where(before, e, 0.0)
        run = run + cs[:, page_size:]
        acc = [acc[h] + _dot(a[h * TOK_PAD:(h + 1) * TOK_PAD, :].astype(BF16), v_h[h]) for h in range(N_HEADS)]
        live = (jnp.max(jnp.where(row < n_new, run, NEG_INF)) >= EXP_ZERO_BELOW).astype(I32)
        return run, acc, live

    def process_slot(slot, run, acc):
        k_h = [kbuf[slot, pl.ds(h, page_size, stride=N_HEADS), :].astype(BF16) for h in range(N_HEADS)]
        v_h = [vbuf[slot, pl.ds(h, page_size, stride=N_HEADS), :].astype(BF16) for h in range(N_HEADS)]
        return process(k_h, v_h, None, run, acc)

    run, acc, live = process(
        [_pad_rows(kn_ref[:, h * HEAD_DIM:(h + 1) * HEAD_DIM], page_size).astype(BF16) for h in range(N_HEADS)],
        [_pad_rows(vn_ref[:, h * HEAD_DIM:(h + 1) * HEAD_DIM], page_size).astype(BF16) for h in range(N_HEADS)],
        lane < row, jnp.zeros((grp, page_size), F32), [jnp.zeros((TOK_PAD, HEAD_DIM), F32)] * N_HEADS)
    for j in range(n_pre):
        run, acc, live = process_slot((b % 2) * n_pre + j, run, acc)
    run_ref[...] = run
    for h in range(N_HEADS):
        acc_ref[h * TOK_PAD:(h + 1) * TOK_PAD, :] = acc[h]

    def fetch_and_process(carry):
        page, _ = carry
        for cp in page_copies(b, page, spare):
            cp.start()
        for cp in page_copies(b, page, spare):
            cp.wait()
        run, acc, live = process_slot(spare, run_ref[...],
                                      [acc_ref[h * TOK_PAD:(h + 1) * TOK_PAD, :] for h in range(N_HEADS)])
        run_ref[...] = run
        for h in range(N_HEADS):
            acc_ref[h * TOK_PAD:(h + 1) * TOK_PAD, :] = acc[h]
        return page - 1, live

    lax.while_loop(lambda carry: (carry[0] >= 0) & (carry[1] > 0), fetch_and_process,
                   (jnp.int32(n_pages - n_pre - 1), live))
    for h in range(N_HEADS):
        o_ref[:, h * HEAD_DIM:(h + 1) * HEAD_DIM] = acc_ref[h * TOK_PAD:(h + 1) * TOK_PAD, :]


def sb_sample(proj8, k_pool, v_pool, page_table, n_new):
    n_seq = proj8.shape[0]
    n_pages = page_table.shape[1]
    page_size = k_pool.shape[1] // N_HEADS
    assert page_size == LANES
    n_pre = min(2, n_pages)
    r = jnp.asarray(_suffix_sum_matrix(page_size), BF16)
    tok = lambda col: pl.BlockSpec((None, TOK_PAD, D_ATT), lambda b, pt: (b, 0, col // D_ATT))
    page_buf = pltpu.VMEM((2 * n_pre + 1, page_size * N_HEADS, HEAD_DIM), F32)
    grid_spec = pltpu.PrefetchScalarGridSpec(
        num_scalar_prefetch=1,
        grid=(n_seq,),
        in_specs=[tok(C_QS), tok(C_KS), tok(C_VS), pl.BlockSpec((page_size, 2 * page_size), lambda b, pt: (0, 0)),
                  pl.BlockSpec(memory_space=pl.ANY), pl.BlockSpec(memory_space=pl.ANY)],
        out_specs=pl.BlockSpec((None, TOK_PAD, D_ATT), lambda b, pt: (b, 0, 0)),
        scratch_shapes=[page_buf, page_buf, pltpu.SemaphoreType.DMA((2, 2 * n_pre + 1)),
                        pltpu.VMEM((N_HEADS * TOK_PAD, page_size), F32), pltpu.VMEM((N_HEADS * TOK_PAD, HEAD_DIM), F32)],
    )
    return pl.pallas_call(
        functools.partial(_sb_sample_kernel, n_pages=n_pages, page_size=page_size, n_new=n_new, n_pre=n_pre),
        grid_spec=grid_spec,
        out_shape=jax.ShapeDtypeStruct((n_seq, TOK_PAD, D_ATT), F32),
        compiler_params=_cparams(1),
        name="sb_sample",
    )(page_table, proj8, proj8, proj8, r, k_pool, v_pool)


SCORE_SEQS = 2


def _dsa_scores_sample_kernel(pt_ref, qi_ref, w_ref, kin_ref, *rest, n_pages, page_size, n_new, n_sub):
    o_ref = rest[n_sub * n_pages]
    o_ref[...] = jnp.zeros_like(o_ref)
    for s_i in range(n_sub):
        ki_pages = rest[s_i * n_pages:(s_i + 1) * n_pages]
        qi = qi_ref[s_i].astype(BF16)
        w = w_ref[s_i] * IDX_W_SCALE
        for g in range(n_pages + 1):
            ki_t = (kin_ref[s_i] if g == n_pages else ki_pages[g][...]).astype(BF16)
            s = jnp.maximum(_dot(qi, ki_t), 0.0) * w
            s = jnp.sum(s.reshape(n_new, N_IDX_HEADS, page_size), axis=1)
            o_ref[s_i, 0:n_new, g * page_size:(g + 1) * page_size] = s


def dsa_scores_sample(qi3, w_rep, ki_new_t, ki_pool_t, page_table):
    n_seq = qi3.shape[0]
    n_new = qi3.shape[1] // N_IDX_HEADS
    n_pages = page_table.shape[1]
    page_size = ki_pool_t.shape[2]
    assert page_size == LANES
    n_sub = SCORE_SEQS if n_seq % SCORE_SEQS == 0 else 1
    seq_blk = lambda r, c: pl.BlockSpec((n_sub, r, c), lambda b, pt: (b, 0, 0))
    pages = [pl.BlockSpec((None, IDX_DIM, page_size),
                          functools.partial(lambda b, pt, s_i, p: (pt[b * n_sub + s_i, p], 0, 0), s_i=s_i, p=p))
             for s_i in range(n_sub) for p in range(n_pages)]
    grid_spec = pltpu.PrefetchScalarGridSpec(
        num_scalar_prefetch=1,
        grid=(n_seq // n_sub,),
        in_specs=[seq_blk(n_new * N_IDX_HEADS, IDX_DIM), seq_blk(n_new * N_IDX_HEADS, LANES),
                  seq_blk(IDX_DIM, page_size)] + pages,
        out_specs=seq_blk(TOK_PAD, (n_pages + 1) * page_size),
    )
    return pl.pallas_call(
        functools.partial(_dsa_scores_sample_kernel, n_pages=n_pages, page_size=page_size, n_new=n_new, n_sub=n_sub),
        grid_spec=grid_spec,
        out_shape=jax.ShapeDtypeStruct((n_seq, TOK_PAD, (n_pages + 1) * page_size), F32),
        compiler_params=_cparams(1),
        name="dsa_scores_sample",
    )(page_table, qi3, w_rep, ki_new_t, *([ki_pool_t] * (n_sub * n_pages)))


MASK_ROWS = 256


def _dsa_mask_sample_kernel(s_ref, ones_ref, m_ref, *, n_past, n_new, top_k):
    n_chunks = s_ref.shape[1] // LANES
    shape = (s_ref.shape[0], LANES)
    tok = lax.broadcasted_iota(I32, shape, 0) % TOK_PAD
    lane = lax.broadcasted_iota(I32, shape, 1)
    ones = ones_ref[...]
    keys, pos, exists = [], [], []
    for c in range(n_chunks):
        s = s_ref[:, c * LANES:(c + 1) * LANES]
        k_pos = c * LANES + lane
        s = jnp.where(k_pos <= n_past + tok, s, NEG_INF)
        keys.append(_float_key(s))
        pos.append(k_pos)
        exists.append(None if (c + 1) * LANES <= n_past + n_new else k_pos < n_past + n_new)

    def count(pred_fn):
        part = jnp.zeros(shape, F32)
        for c in range(n_chunks):
            p = pred_fn(keys[c], pos[c])
            if exists[c] is not None:
                p = p & exists[c]
            part = part + jnp.where(p, 1.0, 0.0)
        return _dot(part.astype(BF16), ones)

    base = jnp.where(count(lambda k, p: k >= 0) >= top_k, 0, INT_MIN).astype(I32)

    def bit_body(t, base):
        cand = base | lax.shift_left(jnp.int32(1), 30 - t)
        return jnp.where(count(lambda k, p: k >= cand) >= top_k, cand, base)

    v = lax.fori_loop(0, 31, bit_body, base)
    need = top_k - count(lambda k, p: k > v)

    n_bits = (n_chunks * LANES - 1).bit_length()

    def idx_bit(t, m):
        cand = m | lax.shift_left(jnp.int32(1), n_bits - 1 - t)
        below = count(lambda k, p: (k == v) & (p < cand))
        return jnp.where(below < need, cand, m)

    mstar = lax.fori_loop(0, n_bits, idx_bit, jnp.zeros(shape, I32))
    for c in range(n_chunks):
        sel = (keys[c] > v) | ((keys[c] == v) & (pos[c] <= mstar))
        sel = sel & (pos[c] <= n_past + tok) & (tok < n_new)
        if exists[c] is not None:
            sel = sel & exists[c]
        m_ref[:, c * LANES:(c + 1) * LANES] = jnp.where(sel, 1.0, 0.0)


def dsa_mask_sample(scores2, n_past, n_new):
    rows, width = scores2.shape
    top_k = min(TOP_K_MAX, (n_past + n_new) // 4)
    tr = min(MASK_ROWS, rows)
    return pl.pallas_call(
        functools.partial(_dsa_mask_sample_kernel, n_past=n_past, n_new=n_new, top_k=top_k),
        grid=(rows // tr,),
        in_specs=[pl.BlockSpec((tr, width), lambda i: (i, 0)), pl.BlockSpec((LANES, LANES), lambda i: (0, 0))],
        out_specs=pl.BlockSpec((tr, width), lambda i: (i, 0)),
        out_shape=jax.ShapeDtypeStruct((rows, width), F32),
        compiler_params=_cparams(1),
        name="dsa_mask_sample",
    )(scores2, jnp.ones((LANES, LANES), BF16))


def _sample_bucket_tables(n_new, page_size):
    j = (np.arange(N_HEADS * TOK_PAD) % TOK_PAD)[:, None]
    u = np.arange(page_size)[None, :]
    assert _rel_bucket_np(np.array([page_size + 1]))[0] == N_REL_BUCKETS - 1
    return np.stack([_rel_bucket_np(page_size + j - u), _rel_bucket_np(j - u)]).astype(np.int32)


def _dsa_attn_sample_kernel(pt_ref, rb_ref, q_ref, kn_ref, vn_ref, m_ref, bkt_ref, *rest, n_pages, page_size, n_new):
    k_pages = rest[:n_pages]
    v_pages = rest[n_pages:2 * n_pages]
    o_ref = rest[2 * n_pages]
    tab_ref = rest[2 * n_pages + 1]
    lg_ref = rest[2 * n_pages + 2]
    grp = N_HEADS * TOK_PAD

    @pl.when(pl.program_id(0) == 0)
    def _():
        for h in range(N_HEADS):
            rows = slice(h * TOK_PAD, (h + 1) * TOK_PAD)
            for t in range(2):
                bk = bkt_ref[t, rows, :]
                acc = jnp.zeros(bk.shape, F32)
                for b in range(N_REL_BUCKETS):
                    acc = jnp.where(bk == b, rb_ref[b, h], acc)
                tab_ref[t, rows, :] = acc
            tab_ref[2, rows, :] = jnp.full((TOK_PAD, page_size), rb_ref[N_REL_BUCKETS - 1, h], F32)

    q = [q_ref[:, h * HEAD_DIM:(h + 1) * HEAD_DIM].astype(BF16) for h in range(N_HEADS)]
    mx = jnp.full((grp, page_size), NEG_INF, F32)
    for g in range(n_pages + 1):
        if g == n_pages:
            k_h = [_pad_rows(kn_ref[:, h * HEAD_DIM:(h + 1) * HEAD_DIM], page_size).astype(BF16) for h in range(N_HEADS)]
            bias = tab_ref[1]
        else:
            k_h = [_head_rows(k_pages[g], h, page_size) for h in range(N_HEADS)]
            bias = tab_ref[0] if g == n_pages - 1 else tab_ref[2]
        s = jnp.concatenate([_dot_nt(q[h], k_h[h]) for h in range(N_HEADS)], axis=0) * ATTN_SCALE + bias
        sel = jnp.tile(m_ref[:, g * page_size:(g + 1) * page_size], (N_HEADS, 1))
        s = jnp.where(sel > 0.5, s, NEG_INF)
        lg_ref[g] = s
        mx = jnp.maximum(mx, s)
    m = jnp.max(mx, axis=1, keepdims=True)
    l = jnp.zeros((grp, page_size), F32)
    acc = [jnp.zeros((TOK_PAD, HEAD_DIM), F32) for _ in range(N_HEADS)]
    for g in range(n_pages + 1):
        p = jnp.exp(lg_ref[g] - m)
        l = l + p
        for h in range(N_HEADS):
            if g == n_pages:
                v_h = _pad_rows(vn_ref[:, h * HEAD_DIM:(h + 1) * HEAD_DIM], page_size).astype(BF16)
            else:
                v_h = _head_rows(v_pages[g], h, page_size)
            acc[h] = acc[h] + _dot(p[h * TOK_PAD:(h + 1) * TOK_PAD, :].astype(BF16), v_h)
    inv = 1.0 / jnp.sum(l, axis=1, keepdims=True)
    for h in range(N_HEADS):
        o_ref[:, h * HEAD_DIM:(h + 1) * HEAD_DIM] = acc[h] * inv[h * TOK_PAD:(h + 1) * TOK_PAD, :]


def dsa_attn_sample(proj8, mask3, rel_bias, k_pool, v_pool, page_table, n_new):
    n_seq = proj8.shape[0]
    n_pages = page_table.shape[1]
    page_size = k_pool.shape[1] // N_HEADS
    assert page_size == LANES
    grp = N_HEADS * TOK_PAD
    width = (n_pages + 1) * page_size
    tok = lambda col: pl.BlockSpec((None, TOK_PAD, D_ATT), lambda b, pt: (b, 0, col // D_ATT))
    grid_spec = pltpu.PrefetchScalarGridSpec(
        num_scalar_prefetch=1,
        grid=(n_seq,),
        in_specs=[pl.BlockSpec(memory_space=pltpu.SMEM), tok(C_QD), tok(C_KD), tok(C_VD),
                  pl.BlockSpec((None, TOK_PAD, width), lambda b, pt: (b, 0, 0)),
                  pl.BlockSpec((2, grp, page_size), lambda b, pt: (0, 0, 0))]
                 + _page_specs(n_pages, page_size * N_HEADS, HEAD_DIM)
                 + _page_specs(n_pages, page_size * N_HEADS, HEAD_DIM),
        out_specs=pl.BlockSpec((None, TOK_PAD, D_ATT), lambda b, pt: (b, 0, 0)),
        scratch_shapes=[pltpu.VMEM((3, grp, page_size), F32), pltpu.VMEM((n_pages + 1, grp, page_size), F32)],
    )
    return pl.pallas_call(
        functools.partial(_dsa_attn_sample_kernel, n_pages=n_pages, page_size=page_size, n_new=n_new),
        grid_spec=grid_spec,
        out_shape=jax.ShapeDtypeStruct((n_seq, TOK_PAD, D_ATT), F32),
        compiler_params=_cparams(1, vmem_mb=56),
        name="dsa_attn_sample",
    )(page_table, rel_bias, proj8, proj8, proj8, mask3, jnp.asarray(_sample_bucket_tables(n_new, page_size)),
      *([k_pool] * n_pages), *([v_pool] * n_pages))


def _tail(x, proj, o_a, o_b, weights, gains):
    w1, w2, wo, wu, wd = weights
    g_attn_post, g_mlp_pre, g_mlp_post = gains
    y = merge_branches(o_a, o_b, w1, w2, proj, min(1024, x.shape[0]))
    x1, h2 = proj_norm(y, wo, x, g_attn_post, g_mlp_pre)
    return mlp(h2, wu, wd, x1, g_mlp_post)


def kernel(x_prompt, x_sample, cache_k_dsa, cache_v_dsa, cache_k_idx, cache_k_sb, cache_v_sb, page_table, rel_bias,
           w_in, w_out_dsa, w_out_sb, w_o, w_up, w_down, g_attn_pre, g_attn_post, g_mlp_pre, g_mlp_post):
    batch, seq, d_model = x_prompt.shape
    n_seq, n_new, _ = x_sample.shape
    depth = w_in.shape[0]
    n_pool, page_size = cache_k_idx.shape[1], cache_k_idx.shape[2]
    n_past = page_table.shape[1] * page_size

    xp = x_prompt.reshape(batch * seq, d_model)
    xs = jnp.pad(x_sample, ((0, 0), (0, TOK_PAD - n_new), (0, 0))).reshape(n_seq * TOK_PAD, d_model)
    rows_p, rows_s = [], []
    for l in range(depth):
        w_in_t = jnp.swapaxes(w_in[l], 0, 1)
        weights = tuple(w[l].astype(BF16) for w in (w_out_dsa, w_out_sb, w_o, w_up, w_down))
        gains = (g_attn_post[l], g_mlp_pre[l], g_mlp_post[l])

        hp = rms_cast(xp, g_attn_pre[l])
        proj, c_small = in_proj(hp, w_in_t, tm=min(1024, hp.shape[0]))
        mask = dsa_mask_prompt(proj, batch, seq, c_small)
        o_a = dsa_attn_prompt(proj, mask, rel_bias, batch, seq)
        o_b = sb_prompt(proj, batch, seq)
        xp_new = _tail(xp, proj, o_a, o_b, weights, gains)
        heads = lambda c: proj[:, c:c + D_ATT].reshape(batch, seq, N_HEADS, HEAD_DIM)
        rows_p.append((heads(C_KD), heads(C_VD), proj[:, c_small:c_small + IDX_DIM].reshape(batch, seq, IDX_DIM),
                       heads(C_KS), heads(C_VS)))
        xp = xp_new

        hs = rms_cast(xs, g_attn_pre[l])
        proj_s, _ = in_proj(hs, w_in_t, tm=min(1024, hs.shape[0]))
        proj8 = proj_s.reshape(n_seq, TOK_PAD, proj_s.shape[1])
        real = lambda c, w: proj8[:, :n_new, c:c + w]
        qi3 = real(C_QI, N_IDX_HEADS * IDX_DIM).reshape(n_seq, n_new * N_IDX_HEADS, IDX_DIM)
        w_rep = jnp.broadcast_to(real(c_small + IDX_DIM, N_IDX_HEADS).reshape(n_seq, n_new * N_IDX_HEADS, 1),
                                 (n_seq, n_new * N_IDX_HEADS, LANES))
        ki_new = real(c_small, IDX_DIM)
        ki_new_t = jnp.pad(jnp.swapaxes(ki_new, 1, 2), ((0, 0), (0, 0), (0, page_size - n_new)))
        pool2 = lambda c: c[l].reshape(n_pool, page_size * N_HEADS, HEAD_DIM)
        scores = dsa_scores_sample(qi3, w_rep, ki_new_t, jnp.swapaxes(cache_k_idx[l], 1, 2), page_table)
        mask_s = dsa_mask_sample(scores.reshape(n_seq * TOK_PAD, scores.shape[2]), n_past, n_new)
        o_a = dsa_attn_sample(proj8, mask_s.reshape(scores.shape), rel_bias, pool2(cache_k_dsa), pool2(cache_v_dsa),
                              page_table, n_new)
        o_b = sb_sample(proj8, pool2(cache_k_sb), pool2(cache_v_sb), page_table, n_new)
        flat = lambda o: o.reshape(n_seq * TOK_PAD, D_ATT)
        xs_new = _tail(xs, proj_s, flat(o_a), flat(o_b), weights, gains)
        heads_s = lambda c: real(c, D_ATT).reshape(n_seq, n_new, N_HEADS, HEAD_DIM)
        rows_s.append((heads_s(C_KD), heads_s(C_VD), ki_new, heads_s(C_KS), heads_s(C_VS)))
        xs = xs_new

    outs_p = [jnp.stack(r, axis=0) for r in zip(*rows_p)]
    outs_s = [jnp.stack(r, axis=0) for r in zip(*rows_s)]
    return (xp.reshape(batch, seq, d_model), xs.reshape(n_seq, TOK_PAD, d_model)[:, :n_new], *outs_p, *outs_s)
```
